```python
import math
import jax, jax.numpy as jnp
from jax import lax
import numpy as np

D_MODEL = 1024
BATCH = 2
SEQ = 16384
DEPTH = 4

N_MIXERS = 4
EPS = 1e-6
D_FF = 2816
REL_BUCKETS = 32
REL_MAX_DIST = 2048
REL_HEADS = 8
A_HEADS = 8
A_HEAD_DIM = D_MODEL // A_HEADS
IDX_HEADS = 8
IDX_DIM = 64
TOPK_MAX = 256
Q_BLOCK = 128
A_SPLITS = (D_MODEL, 2 * D_MODEL, 3 * D_MODEL,
            3 * D_MODEL + IDX_HEADS * IDX_DIM,
            3 * D_MODEL + IDX_HEADS * IDX_DIM + IDX_DIM)
A_IN = 3 * D_MODEL + IDX_HEADS * IDX_DIM + IDX_DIM + IDX_HEADS
B_CHUNK = 128
B_HALF = 3 * D_MODEL
B_GROUPS = 8
C_KERNEL = 31
D_PAIRS = ((128, 1), (512, 4), (2048, 16))
D_NGROUPS = len(D_PAIRS)
D_HEADS = 8
D_HEAD_DIM = 64
D_BLOCK = 128
D_SPAN = D_BLOCK * max(dil for _, dil in D_PAIRS)
D_IN = 3 * D_NGROUPS * D_HEADS * D_HEAD_DIM
D_OUT = D_HEADS * D_HEAD_DIM


def n_uses(m):
    return (DEPTH - m + N_MIXERS - 1) // N_MIXERS


kernel_name = "hybrid_interleaved_dsa_gmlp_conv_dilated"


def rmsnorm(x, g):
    xf = x.astype(jnp.float32)
    y = xf * lax.rsqrt(jnp.mean(xf * xf, axis=-1, keepdims=True) + EPS)
    return (y * g.astype(jnp.float32)).astype(x.dtype)


def layernorm(x, g, b):
    xf = x.astype(jnp.float32)
    mu = jnp.mean(xf, axis=-1, keepdims=True)
    var = jnp.mean(jnp.square(xf - mu), axis=-1, keepdims=True)
    y = (xf - mu) * lax.rsqrt(var + EPS)
    return (y * g.astype(jnp.float32) + b.astype(jnp.float32)).astype(x.dtype)


def swiglu(x, w_in, w_out):
    gate, up = jnp.split(x @ w_in, 2, axis=-1)
    return (jax.nn.silu(gate) * up) @ w_out


def rel_bucket(dist):
    max_exact = REL_BUCKETS // 2
    d = jnp.maximum(dist, 0)
    df = jnp.maximum(d, 1).astype(jnp.float32)
    large = max_exact + (jnp.log(df / max_exact) / math.log(REL_MAX_DIST / max_exact)
                         * (REL_BUCKETS - max_exact)).astype(jnp.int32)
    large = jnp.minimum(large, REL_BUCKETS - 1)
    return jnp.where(d < max_exact, d, large)


def mixer_a(h, w_in, w_out, rel_table):
    B_, S, _ = h.shape
    q, k, v, qi, ki, wi = jnp.split(h @ w_in, A_SPLITS, axis=-1)
    q = q.reshape(B_, S, A_HEADS, A_HEAD_DIM)
    k = k.reshape(B_, S, A_HEADS, A_HEAD_DIM)
    v = v.reshape(B_, S, A_HEADS, A_HEAD_DIM)
    qi = qi.reshape(B_, S, IDX_HEADS, IDX_DIM)
    wi = wi * (IDX_HEADS ** -0.5)
    top_k = min(TOPK_MAX, S // 4)
    nb = S // Q_BLOCK
    key_pos = jnp.arange(S)
    scale = A_HEAD_DIM ** -0.5
    gather = jax.vmap(lambda arr, ix: arr[ix])

    def block(i):
        start = i * Q_BLOCK
        q_pos = start + jnp.arange(Q_BLOCK)
        qb = lax.dynamic_slice_in_dim(q, start, Q_BLOCK, axis=1)
        qib = lax.dynamic_slice_in_dim(qi, start, Q_BLOCK, axis=1)
        wib = lax.dynamic_slice_in_dim(wi, start, Q_BLOCK, axis=1)
        rel = jax.nn.relu(jnp.einsum('bthd,bsd->bths', qib, ki).astype(jnp.float32) * (IDX_DIM ** -0.5))
        score = jnp.einsum('bth,bths->bts', wib.astype(jnp.float32), rel)
        causal = key_pos[None, :] <= q_pos[:, None]
        score = jnp.where(causal[None], score, -jnp.inf)
        _, idx = lax.top_k(score, top_k)
        valid = idx <= q_pos[None, :, None]
        kg = gather(k, idx)
        vg = gather(v, idx)
        bias = rel_table[rel_bucket(q_pos[None, :, None] - idx)]
        logits = (jnp.einsum('bthe,btkhe->bhtk', qb, kg).astype(jnp.float32) * scale
                  + jnp.transpose(bias, (0, 3, 1, 2)).astype(jnp.float32))
        logits = jnp.where(valid[:, None], logits, -jnp.inf)
        p = jax.nn.softmax(logits, axis=-1)
        o = jnp.einsum('bhtk,btkhe->bthe', p, vg)
        return o.reshape(B_, Q_BLOCK, D_MODEL)

    out = lax.map(block, jnp.arange(nb))
    out = jnp.transpose(out, (1, 0, 2, 3)).reshape(B_, S, D_MODEL).astype(h.dtype)
    return out @ w_out


def mixer_b(h, w_in, b_in, ln_g, ln_b, w_sp, b_sp, w_out):
    B_, S, _ = h.shape
    z = jax.nn.gelu(h @ w_in + b_in)
    u, v = jnp.split(z, 2, axis=-1)
    v = layernorm(v, ln_g, ln_b)
    nc = S // B_CHUNK
    v = v.reshape(B_, nc, B_CHUNK, B_GROUPS, B_HALF // B_GROUPS)
    tri = jnp.tril(jnp.ones((B_CHUNK, B_CHUNK), dtype=bool))
    w = jnp.where(tri[None], w_sp, jnp.zeros((), w_sp.dtype))
    sv = jnp.einsum('gts,bcsgd->bctgd', w, v) + jnp.transpose(b_sp)[None, None, :, :, None]
    y = u * sv.reshape(B_, S, B_HALF)
    return y @ w_out


def mixer_c(h, w_pw1, b_pw1, w_dw, b_dw, ln_g, ln_b, w_pw2, b_pw2):
    a, g = jnp.split(h @ w_pw1 + b_pw1, 2, axis=-1)
    y = a * jax.nn.sigmoid(g)
    y = lax.conv_general_dilated(y, w_dw[:, None, :], window_strides=(1,),
                                 padding=((C_KERNEL - 1, 0),),
                                 dimension_numbers=('NWC', 'WIO', 'NWC'),
                                 feature_group_count=D_MODEL) + b_dw
    y = jax.nn.silu(layernorm(y, ln_g, ln_b))
    return y @ w_pw2 + b_pw2


def dilated_group(q, k, v, window, dil, rel_table):
    B_, Sp, H, E = q.shape
    n = Sp // dil
    nb = n // D_BLOCK

    def to_sub(t):
        t = jnp.swapaxes(t.reshape(B_, n, dil, H, E), 1, 2)
        return t.reshape(B_, dil, nb, D_BLOCK, H, E)

    def band(t):
        prev = jnp.pad(t, ((0, 0), (0, 0), (1, 0), (0, 0), (0, 0), (0, 0)))[:, :, :-1]
        return jnp.concatenate([prev, t], axis=3)

    qs = to_sub(q)
    kb = band(to_sub(k))
    vb = band(to_sub(v))
    steps = window // dil
    p_loc = jnp.arange(D_BLOCK)[:, None]
    j_loc = jnp.arange(2 * D_BLOCK)[None, :]
    m = p_loc + D_BLOCK - j_loc
    in_win = (m >= 0) & (m <= steps)
    first = (jnp.arange(nb) == 0)[:, None, None] & (j_loc < D_BLOCK)[None]
    mask = in_win[None] & ~first
    bias = jnp.transpose(rel_table[rel_bucket(m * dil)], (2, 0, 1)).astype(jnp.float32)
    logits = jnp.einsum('brnqhe,brnkhe->brnhqk', qs, kb).astype(jnp.float32) * (E ** -0.5) + bias
    logits = jnp.where(mask[None, None, :, None], logits, -jnp.inf)
    mx = jnp.max(logits, axis=-1, keepdims=True)
    e = jnp.exp(logits - mx)
    l = jnp.sum(e, axis=-1, keepdims=True)
    o = jnp.einsum('brnhqk,brnkhe->brnqhe', e / l, vb)
    lse = jnp.swapaxes((mx + jnp.log(l))[..., 0], -1, -2)

    def from_sub(t):
        rest = t.shape[4:]
        t = t.reshape((B_, dil, n) + rest)
        return jnp.swapaxes(t, 1, 2).reshape((B_, Sp) + rest)

    return from_sub(o), from_sub(lse)


def mixer_d(h, w_in, w_out, rel_table):
    B_, S, _ = h.shape
    Sp = -(-S // D_SPAN) * D_SPAN
    proj = jnp.pad(h @ w_in, ((0, 0), (0, Sp - S), (0, 0)))
    proj = proj.reshape(B_, Sp, 3, D_NGROUPS, D_HEADS, D_HEAD_DIM)
    outs, lses = [], []
    for g, (window, dil) in enumerate(D_PAIRS):
        o, lse = dilated_group(proj[:, :, 0, g], proj[:, :, 1, g], proj[:, :, 2, g],
                               window, dil, rel_table)
        outs.append(o)
        lses.append(lse)
    alpha = jax.nn.softmax(jnp.stack(lses, 0), axis=0)
    y = jnp.sum(alpha[..., None] * jnp.stack(outs, 0), axis=0)
    y = y[:, :S].reshape(B_, S, D_OUT).astype(h.dtype)
    return y @ w_out


def setup_inputs(seed: int = 0) -> dict:
    key = jax.random.key(seed)
    ks = jax.random.split(key, 32)
    nrm = lambda k, shape, s: jax.random.normal(k, shape, jnp.float32) * s
    nA, nB, nC, nD = n_uses(0), n_uses(1), n_uses(2), n_uses(3)
    D = D_MODEL
    return {
        "x": nrm(ks[0], (BATCH, SEQ, D), 1.0),
        "norm_g": 1.0 + nrm(ks[1], (DEPTH, 3, D), 0.05),
        "final_g": 1.0 + nrm(ks[2], (D,), 0.05),
        "ffn_w_in": nrm(ks[3], (DEPTH, 2, D, 2 * D_FF), D ** -0.5),
        "ffn_w_out": nrm(ks[4], (DEPTH, 2, D_FF, D), D_FF ** -0.5),
        "rel_table": nrm(ks[5], (REL_BUCKETS, REL_HEADS), 0.5),
        "a_w_in": nrm(ks[6], (nA, D, A_IN), D ** -0.5),
        "a_w_out": nrm(ks[7], (nA, D, D), D ** -0.5),
        "b_w_in": nrm(ks[8], (nB, D, 2 * B_HALF), D ** -0.5),
        "b_b_in": nrm(ks[9], (nB, 2 * B_HALF), 0.02),
        "b_ln_g": 1.0 + nrm(ks[10], (nB, B_HALF), 0.05),
        "b_ln_b": nrm(ks[11], (nB, B_HALF), 0.02),
        "b_w_sp": nrm(ks[12], (nB, B_GROUPS, B_CHUNK, B_CHUNK), B_CHUNK ** -0.5),
        "b_b_sp": 1.0 + nrm(ks[13], (nB, B_GROUPS, B_CHUNK), 0.05),
        "b_w_out": nrm(ks[14], (nB, B_HALF, D), B_HALF ** -0.5),
        "c_w_pw1": nrm(ks[15], (nC, D, 2 * D), D ** -0.5),
        "c_b_pw1": nrm(ks[16], (nC, 2 * D), 0.02),
        "c_w_dw": nrm(ks[17], (nC, C_KERNEL, D), C_KERNEL ** -0.5),
        "c_b_dw": nrm(ks[18], (nC, D), 0.02),
        "c_ln_g": 1.0 + nrm(ks[19], (nC, D), 0.05),
        "c_ln_b": nrm(ks[20], (nC, D), 0.02),
        "c_w_pw2": nrm(ks[21], (nC, D, D), D ** -0.5),
        "c_b_pw2": nrm(ks[22], (nC, D), 0.02),
        "d_w_in": nrm(ks[23], (nD, D, D_IN), D ** -0.5),
        "d_w_out": nrm(ks[24], (nD, D_OUT, D), D_OUT ** -0.5),
    }


def reference(x, norm_g, final_g, ffn_w_in, ffn_w_out, rel_table,
              a_w_in, a_w_out,
              b_w_in, b_b_in, b_ln_g, b_ln_b, b_w_sp, b_b_sp, b_w_out,
              c_w_pw1, c_b_pw1, c_w_dw, c_b_dw, c_ln_g, c_ln_b, c_w_pw2, c_b_pw2,
              d_w_in, d_w_out):
    h = x
    for i in range(DEPTH):
        kind, j = i % N_MIXERS, i // N_MIXERS
        h = h + 0.5 * swiglu(rmsnorm(h, norm_g[i, 0]), ffn_w_in[i, 0], ffn_w_out[i, 0])
        hn = rmsnorm(h, norm_g[i, 1])
        if kind == 0:
            y = mixer_a(hn, a_w_in[j], a_w_out[j], rel_table)
        elif kind == 1:
            y = mixer_b(hn, b_w_in[j], b_b_in[j], b_ln_g[j], b_ln_b[j], b_w_sp[j], b_b_sp[j], b_w_out[j])
        elif kind == 2:
            y = mixer_c(hn, c_w_pw1[j], c_b_pw1[j], c_w_dw[j], c_b_dw[j], c_ln_g[j], c_ln_b[j],
                        c_w_pw2[j], c_b_pw2[j])
        else:
            y = mixer_d(hn, d_w_in[j], d_w_out[j], rel_table)
        h = h + y
        h = h + 0.5 * swiglu(rmsnorm(h, norm_g[i, 2]), ffn_w_in[i, 1], ffn_w_out[i, 1])
    return rmsnorm(h, final_g)
```

```python
import functools
import math

import numpy as np
import jax
import jax.numpy as jnp
from jax import lax
from jax.experimental import pallas as pl
from jax.experimental.pallas import tpu as pltpu

F32, BF16, I32 = jnp.float32, jnp.bfloat16, jnp.int32

D_MODEL = 1024
EPS = 1e-6
D_FF = 2816
REL_BUCKETS = 32
REL_MAX_DIST = 2048
A_HEADS = 8
A_HEAD_DIM = 128
IDX_HEADS = 8
IDX_DIM = 64
TOPK_MAX = 256
B_CHUNK = 128
B_HALF = 3 * D_MODEL
B_GROUPS = 8
C_KERNEL = 31
D_PAIRS = ((128, 1), (512, 4), (2048, 16))
D_HEADS = 8
D_HEAD_DIM = 64
D_BLOCK = 128

LANES = 128
VMEM_LIMIT = 56 * 1024 * 1024
NEG = -1e30
INT_MIN = -2 ** 31

_NT = (((1,), (1,)), ((), ()))


def _cp(*sem):
    return pltpu.CompilerParams(dimension_semantics=sem, vmem_limit_bytes=VMEM_LIMIT)


def _rms_bf16(x, g):
    ms = jnp.mean(x * x, axis=-1, keepdims=True)
    return (x * lax.rsqrt(ms + EPS) * g).astype(BF16)


def _layernorm(x, g, b):
    mu = jnp.mean(x, axis=-1, keepdims=True)
    xc = x - mu
    var = jnp.mean(xc * xc, axis=-1, keepdims=True)
    return xc * lax.rsqrt(var + EPS) * g + b


def _ffn_kernel(x_ref, g_ref, wg_ref, wu_ref, wo_ref, o_ref, xn_ref, acc_ref):
    j = pl.program_id(1)

    @pl.when(j == 0)
    def _():
        xn_ref[...] = _rms_bf16(x_ref[...], g_ref[...])
        acc_ref[...] = jnp.zeros_like(acc_ref)

    xn = xn_ref[...]
    gate = jnp.dot(xn, wg_ref[...], preferred_element_type=F32)
    up = jnp.dot(xn, wu_ref[...], preferred_element_type=F32)
    a = (gate * jax.nn.sigmoid(gate) * up).astype(BF16)
    acc_ref[...] += jnp.dot(a, wo_ref[...], preferred_element_type=F32)

    @pl.when(j == pl.num_programs(1) - 1)
    def _():
        o_ref[...] = x_ref[...] + 0.5 * acc_ref[...]


def _ffn(h, g, w_in, w_out, *, tm=512, tf=1408):
    n, d = h.shape
    nf = D_FF // tf
    return pl.pallas_call(
        _ffn_kernel,
        out_shape=jax.ShapeDtypeStruct((n, d), F32),
        grid=(n // tm, nf),
        in_specs=[
            pl.BlockSpec((tm, d), lambda i, j: (i, 0)),
            pl.BlockSpec((1, d), lambda i, j: (0, 0)),
            pl.BlockSpec((d, tf), lambda i, j: (0, j)),
            pl.BlockSpec((d, tf), lambda i, j: (0, nf + j)),
            pl.BlockSpec((tf, d), lambda i, j: (j, 0)),
        ],
        out_specs=pl.BlockSpec((tm, d), lambda i, j: (i, 0)),
        scratch_shapes=[pltpu.VMEM((tm, d), BF16), pltpu.VMEM((tm, d), F32)],
        compiler_params=_cp("parallel", "arbitrary"),
        name="ffn",
    )(h, g.reshape(1, d), w_in, w_in, w_out)


def _nm_kernel(mode, has_bias, x_ref, g_ref, *refs):
    nw = 2 if mode == "glu" else 1
    w_refs = refs[:nw]
    b_refs = refs[nw:nw + (nw if has_bias else 0)]
    o_ref, xn_ref = refs[-2], refs[-1]

    @pl.when(pl.program_id(1) == 0)
    def _():
        xn_ref[...] = _rms_bf16(x_ref[...], g_ref[...])

    xn = xn_ref[...]
    ys = []
    for k in range(nw):
        y = jnp.dot(xn, w_refs[k][...], preferred_element_type=F32)
        if has_bias:
            y = y + b_refs[k][...]
        ys.append(y)
    if mode == "plain":
        out = ys[0]
    elif mode == "gelu":
        out = jax.nn.gelu(ys[0])
    else:
        out = ys[0] * jax.nn.sigmoid(ys[1])
    o_ref[...] = out.astype(o_ref.dtype)


def _norm_matmul(h, g, w, b=None, *, mode="plain", out_dtype=BF16, tm=512, tn=512, name="proj"):
    n, d = h.shape
    m_total = w.shape[1]
    m_out = m_total // 2 if mode == "glu" else m_total
    nj = m_out // tn
    has_bias = b is not None
    in_specs = [pl.BlockSpec((tm, d), lambda i, j: (i, 0)),
                pl.BlockSpec((1, d), lambda i, j: (0, 0)),
                pl.BlockSpec((d, tn), lambda i, j: (0, j))]
    args = [h, g.reshape(1, d), w]
    if mode == "glu":
        in_specs.append(pl.BlockSpec((d, tn), lambda i, j: (0, nj + j)))
        args.append(w)
    if has_bias:
        b2 = b.reshape(1, m_total)
        in_specs.append(pl.BlockSpec((1, tn), lambda i, j: (0, j)))
        args.append(b2)
        if mode == "glu":
            in_specs.append(pl.BlockSpec((1, tn), lambda i, j: (0, nj + j)))
            args.append(b2)
    return pl.pallas_call(
        functools.partial(_nm_kernel, mode, has_bias),
        out_shape=jax.ShapeDtypeStruct((n, m_out), out_dtype),
        grid=(n // tm, nj),
        in_specs=in_specs,
        out_specs=pl.BlockSpec((tm, tn), lambda i, j: (i, j)),
        scratch_shapes=[pltpu.VMEM((tm, d), BF16)],
        compiler_params=_cp("parallel", "arbitrary"),
        name=name,
    )(*args)


def _op_kernel(has_bias, y_ref, w_ref, *refs):
    if has_bias:
        b_ref, r_ref, o_ref = refs
    else:
        r_ref, o_ref = refs
    acc = jnp.dot(y_ref[...], w_ref[...], preferred_element_type=F32)
    if has_bias:
        acc = acc + b_ref[...]
    o_ref[...] = r_ref[...] + acc


def _out_proj(y, w, resid, b=None, *, tm=512, name="out_proj"):
    n, k = y.shape
    d = w.shape[1]
    has_bias = b is not None
    in_specs = [pl.BlockSpec((tm, k), lambda i: (i, 0)),
                pl.BlockSpec((k, d), lambda i: (0, 0))]
    args = [y, w]
    if has_bias:
        in_specs.append(pl.BlockSpec((1, d), lambda i: (0, 0)))
        args.append(b.reshape(1, d))
    in_specs.append(pl.BlockSpec((tm, d), lambda i: (i, 0)))
    args.append(resid)
    return pl.pallas_call(
        functools.partial(_op_kernel, has_bias),
        out_shape=jax.ShapeDtypeStruct((n, d), F32),
        grid=(n // tm,),
        in_specs=in_specs,
        out_specs=pl.BlockSpec((tm, d), lambda i: (i, 0)),
        compiler_params=_cp("parallel"),
        name=name,
    )(*args)


def _rel_bucket_np(dist):
    max_exact = REL_BUCKETS // 2
    d = np.maximum(dist, 0)
    df = np.maximum(d, 1).astype(np.float32)
    large = max_exact + (np.log(df / np.float32(max_exact)) / np.float32(math.log(REL_MAX_DIST / max_exact))
                         * np.float32(REL_BUCKETS - max_exact)).astype(np.int32)
    large = np.minimum(large, REL_BUCKETS - 1)
    return np.where(d < max_exact, d, large).astype(np.int32)


def _bias_kernel(tab_ref, idx_ref, o_ref):
    idx = idx_ref[...]
    for h in range(o_ref.shape[1]):
        acc = jnp.zeros(idx.shape, F32)
        for b in range(REL_BUCKETS):
            acc = jnp.where(idx == b, tab_ref[b, h], acc)
        o_ref[0, h] = jnp.where(idx < 0, NEG, acc).astype(o_ref.dtype)


def _bias_tiles(rel_table, bucket_idx, out_dtype, *, tile, tr):
    r, c = bucket_idx.shape
    nh = rel_table.shape[1]
    per = tile // tr
    return pl.pallas_call(
        _bias_kernel,
        out_shape=jax.ShapeDtypeStruct((r // tile, nh, tile, c), out_dtype),
        grid=(r // tr,),
        in_specs=[pl.BlockSpec(memory_space=pltpu.SMEM),
                  pl.BlockSpec((tr, c), lambda i: (i, 0))],
        out_specs=pl.BlockSpec((1, nh, tr, c), lambda i: (i // per, 0, i % per, 0)),
        compiler_params=_cp("parallel"),
        name="bias_tiles",
    )(rel_table, jnp.asarray(bucket_idx))


def _asel_kernel(qi_ref, wi_ref, ki_ref, m_ref, keys_ref, wb_ref, x_ref, *, tq, cw, s_len, topk):
    qb = pl.program_id(1)
    spc = cw // LANES
    n_chunks = (qb * tq + tq + cw - 1) // cw
    n_slabs = n_chunks * spc
    nslab_all = s_len // LANES

    wi = wi_ref[0] * (IDX_HEADS ** -0.5 * IDX_DIM ** -0.5)
    for h in range(IDX_HEADS):
        wb_ref[h] = jnp.broadcast_to(wi[:, h:h + 1], (tq, LANES))
    q_pos = qb * tq + lax.broadcasted_iota(I32, (tq, cw), 0)
    lane_pos = lax.broadcasted_iota(I32, (tq, cw), 1)

    def build(c, carry):
        k0 = pl.multiple_of(c * cw, cw)
        kc = ki_ref[0, pl.ds(k0, cw), :]
        sc = jnp.zeros((tq, cw), F32)
        for h in range(IDX_HEADS):
            r = lax.dot_general(qi_ref[0, :, h * IDX_DIM:(h + 1) * IDX_DIM], kc, _NT,
                                preferred_element_type=F32)
            sc = sc + jnp.maximum(r, 0.0) * pltpu.repeat(wb_ref[h], spc, axis=1)
        sc = sc + 0.0
        bits = pltpu.bitcast(sc, I32)
        key = bits ^ ((bits >> 31) & 0x7FFFFFFF)
        key = jnp.where(k0 + lane_pos <= q_pos, key, INT_MIN)
        for j in range(spc):
            keys_ref[c * spc + j] = key[:, j * LANES:(j + 1) * LANES]
        return carry

    lax.fori_loop(0, n_chunks, build, 0)

    def lane_sum(acc):
        return jnp.sum(acc.astype(F32), axis=1, keepdims=True)

    def pass_body(i, theta):
        cand = theta + lax.shift_left(jnp.int32(1), 31 - i)

        def body(c, acc):
            return acc + jnp.where(keys_ref[c] >= cand, 1, 0)

        cnt = lane_sum(lax.fori_loop(0, n_slabs, body, jnp.zeros((tq, LANES), I32)))
        return jnp.where(cnt >= topk, cand, theta)

    tau = lax.fori_loop(0, 32, pass_body, jnp.full((tq, LANES), INT_MIN, I32))

    def ge_body(c, carry):
        g_acc, e_acc = carry
        k = keys_ref[c]
        return (g_acc + jnp.where(k > tau, 1, 0), e_acc + jnp.where(k == tau, 1, 0))

    zero = jnp.zeros((tq, LANES), I32)
    g_acc, e_acc = lax.fori_loop(0, n_slabs, ge_body, (zero, zero))
    r_need = topk - lane_sum(g_acc)
    excess = lane_sum(e_acc) - r_need
    has_real_tau = tau > INT_MIN
    x_ref[...] = jnp.where(has_real_tau, s_len, -1)
    lane = lax.broadcasted_iota(I32, (tq, LANES), 1)

    @pl.when(jnp.max(jnp.where(has_real_tau[:, :1], excess, 0.0)) > 0.0)
    def _():
        nbits = int(math.log2(s_len)) + 1

        def tie_pass(i, x):
            cand = x + lax.shift_left(jnp.int32(1), nbits - 1 - i)

            def body(c, acc):
                idx = c * LANES + lane
                return acc + jnp.where((keys_ref[c] == tau) & (idx < cand), 1, 0)

            cnt = lane_sum(lax.fori_loop(0, n_slabs, body, zero))
            return jnp.where(cnt < r_need, cand, x)

        x = lax.fori_loop(0, nbits, tie_pass, zero)
        x_ref[...] = jnp.where(has_real_tau, x, -1)

    x_cut = x_ref[...]

    def emit(c, carry):
        k = keys_ref[c]
        idx = c * LANES + lane
        sel = (k > tau) | ((k == tau) & (idx <= x_cut))
        m_ref[0, c] = jnp.where(sel, 0.0, NEG).astype(m_ref.dtype)
        return carry

    lax.fori_loop(0, n_slabs, emit, 0)

    def fill(c, carry):
        m_ref[0, c] = jnp.full((tq, LANES), NEG, m_ref.dtype)
        return carry

    lax.fori_loop(n_slabs, nslab_all, fill, 0)


def _a_select(qi, wi, ki, *, tq=128, cw=512):
    bsz, s_len, _ = qi.shape
    topk = min(TOPK_MAX, s_len // 4)
    nslab = s_len // LANES
    return pl.pallas_call(
        functools.partial(_asel_kernel, tq=tq, cw=cw, s_len=s_len, topk=float(topk)),
        out_shape=jax.ShapeDtypeStruct((bsz, nslab, s_len, LANES), BF16),
        grid=(bsz, s_len // tq),
        in_specs=[pl.BlockSpec((1, tq, IDX_HEADS * IDX_DIM), lambda b, i: (b, i, 0)),
                  pl.BlockSpec((1, tq, IDX_HEADS), lambda b, i: (b, i, 0)),
                  pl.BlockSpec((1, s_len, IDX_DIM), lambda b, i: (b, 0, 0))],
        out_specs=pl.BlockSpec((1, nslab, tq, LANES), lambda b, i: (b, 0, i, 0)),
        scratch_shapes=[pltpu.VMEM((nslab, tq, LANES), I32),
                        pltpu.VMEM((IDX_HEADS, tq, LANES), F32),
                        pltpu.VMEM((tq, LANES), I32)],
        compiler_params=_cp("parallel", "arbitrary"),
        name="a_select",
    )(qi, wi, ki)


def _attn_kernel(qt_ref, kt_ref, q_ref, k_ref, v_ref, msk_ref, bias_ref, o_ref, m_sc, l_sc, acc_sc, *, t):
    step = pl.program_id(1)
    qi = qt_ref[step]
    ki = kt_ref[step]

    @pl.when(ki == 0)
    def _():
        m_sc[...] = jnp.full(m_sc.shape, NEG, F32)
        l_sc[...] = jnp.zeros_like(l_sc)
        acc_sc[...] = jnp.zeros_like(acc_sc)

    mask = jnp.concatenate([msk_ref[0, j] for j in range(t // LANES)], axis=1).astype(F32)
    scale = A_HEAD_DIM ** -0.5
    for h in range(A_HEADS):
        hs = slice(h * A_HEAD_DIM, (h + 1) * A_HEAD_DIM)
        s = lax.dot_general(q_ref[0, :, hs], k_ref[0, :, hs], _NT, preferred_element_type=F32)
        s = s * scale + bias_ref[0, h].astype(F32) + mask
        m_prev = m_sc[h]
        m_new = jnp.maximum(m_prev, jnp.max(s, axis=-1, keepdims=True))
        alpha = jnp.exp(m_prev - m_new)
        p = jnp.exp(s - m_new)
        l_sc[h] = alpha * l_sc[h] + jnp.sum(p, axis=-1, keepdims=True)
        acc_sc[:, hs] = alpha * acc_sc[:, hs] + jnp.dot(p.astype(BF16), v_ref[0, :, hs],
                                                        preferred_element_type=F32)
        m_sc[h] = m_new

    @pl.when(ki == qi)
    def _():
        for h in range(A_HEADS):
            hs = slice(h * A_HEAD_DIM, (h + 1) * A_HEAD_DIM)
            o_ref[0, :, hs] = (acc_sc[:, hs] / l_sc[h]).astype(o_ref.dtype)


def _a_attention(q, k, v, mask4, bias, *, t=512):
    bsz, s_len, d = q.shape
    nq = s_len // t
    nd = bias.shape[0]
    qt = np.concatenate([np.full(i + 1, i, np.int32) for i in range(nq)])
    kt = np.concatenate([np.arange(i + 1, dtype=np.int32) for i in range(nq)])
    spt = t // LANES
    grid_spec = pltpu.PrefetchScalarGridSpec(
        num_scalar_prefetch=2,
        grid=(bsz, len(qt)),
        in_specs=[
            pl.BlockSpec((1, t, d), lambda b, s, qt, kt: (b, qt[s], 0)),
            pl.BlockSpec((1, t, d), lambda b, s, qt, kt: (b, kt[s], 0)),
            pl.BlockSpec((1, t, d), lambda b, s, qt, kt: (b, kt[s], 0)),
            pl.BlockSpec((1, spt, t, LANES), lambda b, s, qt, kt: (b, kt[s], qt[s], 0)),
            pl.BlockSpec((1, A_HEADS, t, t),
                         lambda b, s, qt, kt: (jnp.minimum(qt[s] - kt[s], nd - 1), 0, 0, 0)),
        ],
        out_specs=pl.BlockSpec((1, t, d), lambda b, s, qt, kt: (b, qt[s], 0)),
        scratch_shapes=[pltpu.VMEM((A_HEADS, t, 1), F32),
                        pltpu.VMEM((A_HEADS, t, 1), F32),
                        pltpu.VMEM((t, d), F32)],
    )
    return pl.pallas_call(
        functools.partial(_attn_kernel, t=t),
        out_shape=jax.ShapeDtypeStruct((bsz, s_len, d), BF16),
        grid_spec=grid_spec,
        compiler_params=_cp("parallel", "arbitrary"),
        name="a_attention",
    )(jnp.asarray(qt), jnp.asarray(kt), q, k, v, mask4, bias)


def _mixer_a(h, g, w_in, w_out, rel_table, bsz, s_len):
    n = h.shape[0]
    d = D_MODEL
    n_idx = IDX_HEADS * IDX_DIM + IDX_DIM + IDX_HEADS
    w_qkv = w_in[:, :3 * d].astype(BF16)
    w_idx = jnp.pad(w_in[:, 3 * d:], ((0, 0), (0, 640 - n_idx))).astype(BF16)
    qkv = _norm_matmul(h, g, w_qkv, out_dtype=BF16, tn=1024, name="a_qkv")
    idx = _norm_matmul(h, g, w_idx, out_dtype=F32, tn=640, name="a_idx")
    qkv = qkv.reshape(bsz, s_len, 3 * d)
    idx = idx.reshape(bsz, s_len, 640)
    nqi = IDX_HEADS * IDX_DIM
    qi = idx[..., :nqi].astype(BF16)
    ki = idx[..., nqi:nqi + IDX_DIM].astype(BF16)
    wi = idx[..., nqi + IDX_DIM:n_idx]
    mask4 = _a_select(qi, wi, ki)

    t = min(512, s_len)
    nd_far = 0
    while _rel_bucket_np(np.array(nd_far * t - (t - 1))) < REL_BUCKETS - 1:
        nd_far += 1
    nd = min(nd_far, s_len // t - 1) + 1
    ii = np.arange(t)[:, None] - np.arange(t)[None, :]
    dist = (np.arange(nd)[:, None, None] * t + ii[None]).reshape(nd * t, t)
    bias = _bias_tiles(rel_table, _rel_bucket_np(dist), BF16, tile=t, tr=t // 2)
    o = _a_attention(qkv[..., :d], qkv[..., d:2 * d], qkv[..., 2 * d:], mask4, bias, t=t)
    return _out_proj(o.reshape(n, d), w_out.astype(BF16), h, name="a_out")


def _bsp_kernel(u_ref, v_ref, lg_ref, lb_ref, wsp_ref, bsp_ref, o_ref, wt_ref, *, rows):
    @pl.when(pl.program_id(0) == 0)
    def _():
        r = lax.broadcasted_iota(I32, (B_CHUNK, B_CHUNK), 0)
        c = lax.broadcasted_iota(I32, (B_CHUNK, B_CHUNK), 1)
        for g in range(B_GROUPS):
            wt_ref[g] = jnp.where(r >= c, wsp_ref[g], 0.0).astype(BF16)

    vn = _layernorm(v_ref[...].astype(F32), lg_ref[...], lb_ref[...]).astype(BF16)
    gw = B_HALF // B_GROUPS
    for c in range(rows // B_CHUNK):
        rs = slice(c * B_CHUNK, (c + 1) * B_CHUNK)
        for g in range(B_GROUPS):
            cs = slice(g * gw, (g + 1) * gw)
            sv = jnp.dot(wt_ref[g], vn[rs, cs], preferred_element_type=F32) + bsp_ref[g]
            o_ref[rs, cs] = (u_ref[rs, cs].astype(F32) * sv).astype(o_ref.dtype)


def _b_spatial(z, ln_g, ln_b, w_sp, b_sp, *, rows=512):
    n = z.shape[0]
    return pl.pallas_call(
        functools.partial(_bsp_kernel, rows=rows),
        out_shape=jax.ShapeDtypeStruct((n, B_HALF), BF16),
        grid=(n // rows,),
        in_specs=[pl.BlockSpec((rows, B_HALF), lambda i: (i, 0)),
                  pl.BlockSpec((rows, B_HALF), lambda i: (i, 1)),
                  pl.BlockSpec((1, B_HALF), lambda i: (0, 0)),
                  pl.BlockSpec((1, B_HALF), lambda i: (0, 0)),
                  pl.BlockSpec((B_GROUPS, B_CHUNK, B_CHUNK), lambda i: (0, 0, 0)),
                  pl.BlockSpec((B_GROUPS, B_CHUNK, 1), lambda i: (0, 0, 0))],
        out_specs=pl.BlockSpec((rows, B_HALF), lambda i: (i, 0)),
        scratch_shapes=[pltpu.VMEM((B_GROUPS, B_CHUNK, B_CHUNK), BF16)],
        compiler_params=_cp("arbitrary"),
        name="b_spatial",
    )(z, z, ln_g.reshape(1, B_HALF), ln_b.reshape(1, B_HALF), w_sp, b_sp.reshape(B_GROUPS, B_CHUNK, 1))


def _mixer_b(h, g, w_in, b_in, ln_g, ln_b, w_sp, b_sp, w_out):
    z = _norm_matmul(h, g, w_in.astype(BF16), b_in, mode="gelu", out_dtype=BF16, tn=1024, name="b_in")
    y = _b_spatial(z, ln_g, ln_b, w_sp, b_sp)
    return _out_proj(y, w_out.astype(BF16), h, name="b_out")


C_HALO = 32


def _conv_kernel(cur_ref, halo_ref, wdw_ref, bdw_ref, lg_ref, lb_ref, w2_ref, b2_ref, r_ref, o_ref,
                 ext_ref, yc_ref, *, ts, rt, ct):
    i = pl.program_id(1)
    ext_ref[C_HALO:, :] = cur_ref[0]

    @pl.when(i == 0)
    def _():
        ext_ref[:C_HALO, :] = jnp.zeros((C_HALO, D_MODEL), F32)

    @pl.when(i > 0)
    def _():
        ext_ref[:C_HALO, :] = halo_ref[0]

    off = C_HALO - (C_KERNEL - 1)

    def tile(it, carry):
        r0 = pl.multiple_of((it // (D_MODEL // ct)) * rt, rt)
        c0 = pl.multiple_of((it % (D_MODEL // ct)) * ct, ct)
        nwin = rt + C_HALO
        win = ext_ref[pl.ds(r0, nwin), pl.ds(c0, ct)]
        acc = jnp.zeros((rt, ct), F32) + bdw_ref[:, pl.ds(c0, ct)]
        for b in range(8):
            rolled = pltpu.roll(win, nwin - (off + b), axis=0)
            for a in range((C_KERNEL - b + 7) // 8):
                j = 8 * a + b
                acc = acc + rolled[8 * a:8 * a + rt] * wdw_ref[j:j + 1, pl.ds(c0, ct)]
        yc_ref[pl.ds(r0, rt), pl.ds(c0, ct)] = acc
        return carry

    lax.fori_loop(0, (ts // rt) * (D_MODEL // ct), tile, 0)
    y = _layernorm(yc_ref[...], lg_ref[...], lb_ref[...])
    y = (y * jax.nn.sigmoid(y)).astype(BF16)
    o_ref[0] = r_ref[0] + jnp.dot(y, w2_ref[...], preferred_element_type=F32) + b2_ref[...]


def _c_conv(y1, w_dw, b_dw, ln_g, ln_b, w2, b2, resid, *, ts=512, rt=32, ct=512):
    bsz, s_len, d = y1.shape
    hb = ts // C_HALO
    vec = lambda a: a.reshape(1, d)
    return pl.pallas_call(
        functools.partial(_conv_kernel, ts=ts, rt=rt, ct=ct),
        out_shape=jax.ShapeDtypeStruct((bsz, s_len, d), F32),
        grid=(bsz, s_len // ts),
        in_specs=[pl.BlockSpec((1, ts, d), lambda b, i: (b, i, 0)),
                  pl.BlockSpec((1, C_HALO, d), lambda b, i: (b, jnp.maximum(i * hb - 1, 0), 0)),
                  pl.BlockSpec((C_KERNEL, d), lambda b, i: (0, 0)),
                  pl.BlockSpec((1, d), lambda b, i: (0, 0)),
                  pl.BlockSpec((1, d), lambda b, i: (0, 0)),
                  pl.BlockSpec((1, d), lambda b, i: (0, 0)),
                  pl.BlockSpec((d, d), lambda b, i: (0, 0)),
                  pl.BlockSpec((1, d), lambda b, i: (0, 0)),
                  pl.BlockSpec((1, ts, d), lambda b, i: (b, i, 0))],
        out_specs=pl.BlockSpec((1, ts, d), lambda b, i: (b, i, 0)),
        scratch_shapes=[pltpu.VMEM((ts + C_HALO, d), F32), pltpu.VMEM((ts, d), F32)],
        compiler_params=_cp("parallel", "parallel"),
        name="c_conv",
    )(y1, y1, w_dw, vec(b_dw), vec(ln_g), vec(ln_b), w2, vec(b2), resid)


def _mixer_c(h, g, w_pw1, b_pw1, w_dw, b_dw, ln_g, ln_b, w_pw2, b_pw2, bsz, s_len):
    n, d = h.shape
    y1 = _norm_matmul(h, g, w_pw1.astype(BF16), b_pw1, mode="glu", out_dtype=F32, tn=512, name="c_pw1")
    out = _c_conv(y1.reshape(bsz, s_len, d), w_dw, b_dw, ln_g, ln_b, w_pw2.astype(BF16), b_pw2,
                  h.reshape(bsz, s_len, d))
    return out.reshape(n, d)


def _dattn_kernel(q_ref, kc_ref, kp_ref, vc_ref, vp_ref, bias_ref, o_ref, lse_ref):
    nb = pl.program_id(2)
    first = jnp.where(nb == 0, NEG, 0.0)
    e = D_HEAD_DIM
    for h in range(D_HEADS):
        hs = slice(h * e, (h + 1) * e)
        q = q_ref[0, :, hs]
        kk = jnp.concatenate([kp_ref[0, :, hs], kc_ref[0, :, hs]], axis=0)
        vv = jnp.concatenate([vp_ref[0, :, hs], vc_ref[0, :, hs]], axis=0)
        s = lax.dot_general(q, kk, _NT, preferred_element_type=F32) * (e ** -0.5) + bias_ref[h]
        col = lax.broadcasted_iota(I32, s.shape, 1)
        s = s + jnp.where(col < D_BLOCK, first, 0.0)
        m = jnp.max(s, axis=-1, keepdims=True)
        p = jnp.exp(s - m)
        l = jnp.sum(p, axis=-1, keepdims=True)
        o = jnp.dot(p.astype(BF16), vv, preferred_element_type=F32) / l
        o_ref[0, :, hs] = o
        lse_ref[0, :, hs] = jnp.broadcast_to(m + jnp.log(l), (D_BLOCK, e))


def _d_group(proj, bias_g, gidx, dil):
    bsz, s_len, c = proj.shape
    n_sub = s_len // dil
    nb = n_sub // D_BLOCK
    w = D_HEADS * D_HEAD_DIM
    ng = len(D_PAIRS)
    cb = c // w
    view = proj.reshape(bsz, n_sub, dil * c)
    qcol = lambda r: r * cb + gidx
    kcol = lambda r: r * cb + ng + gidx
    vcol = lambda r: r * cb + 2 * ng + gidx
    blk = (1, D_BLOCK, w)
    prev = lambda n: jnp.maximum(n - 1, 0)
    out_sds = jax.ShapeDtypeStruct((bsz, n_sub, dil * w), F32)
    o, lse = pl.pallas_call(
        _dattn_kernel,
        out_shape=(out_sds, out_sds),
        grid=(bsz, dil, nb),
        in_specs=[pl.BlockSpec(blk, lambda b, r, n: (b, n, qcol(r))),
                  pl.BlockSpec(blk, lambda b, r, n: (b, n, kcol(r))),
                  pl.BlockSpec(blk, lambda b, r, n: (b, prev(n), kcol(r))),
                  pl.BlockSpec(blk, lambda b, r, n: (b, n, vcol(r))),
                  pl.BlockSpec(blk, lambda b, r, n: (b, prev(n), vcol(r))),
                  pl.BlockSpec((D_HEADS, D_BLOCK, 2 * D_BLOCK), lambda b, r, n: (0, 0, 0))],
        out_specs=(pl.BlockSpec(blk, lambda b, r, n: (b, n, r)),
                   pl.BlockSpec(blk, lambda b, r, n: (b, n, r))),
        compiler_params=_cp("parallel", "parallel", "arbitrary"),
        name=f"d_attn_g{gidx}",
    )(view, view, view, view, view, bias_g)
    return o.reshape(bsz * s_len, w), lse.reshape(bsz * s_len, w)


def _dmerge_kernel(o0, o1, o2, l0, l1, l2, w_ref, r_ref, out_ref):
    a0, a1, a2 = l0[...], l1[...], l2[...]
    mx = jnp.maximum(jnp.maximum(a0, a1), a2)
    e0, e1, e2 = jnp.exp(a0 - mx), jnp.exp(a1 - mx), jnp.exp(a2 - mx)
    y = (e0 * o0[...] + e1 * o1[...] + e2 * o2[...]) / (e0 + e1 + e2)
    out_ref[...] = r_ref[...] + jnp.dot(y.astype(BF16), w_ref[...], preferred_element_type=F32)


def _d_merge(outs, lses, w_out, resid, *, tm=512):
    n, w = outs[0].shape
    d = resid.shape[1]
    row = pl.BlockSpec((tm, w), lambda i: (i, 0))
    return pl.pallas_call(
        _dmerge_kernel,
        out_shape=jax.ShapeDtypeStruct((n, d), F32),
        grid=(n // tm,),
        in_specs=[row] * 6 + [pl.BlockSpec((w, d), lambda i: (0, 0)),
                              pl.BlockSpec((tm, d), lambda i: (i, 0))],
        out_specs=pl.BlockSpec((tm, d), lambda i: (i, 0)),
        compiler_params=_cp("parallel"),
        name="d_merge",
    )(*outs, *lses, w_out, resid)


def _mixer_d(h, g, w_in, w_out, rel_table, bsz, s_len):
    proj = _norm_matmul(h, g, w_in.astype(BF16), out_dtype=BF16, tn=1536, name="d_in")
    proj = proj.reshape(bsz, s_len, proj.shape[1])
    p_loc = np.arange(D_BLOCK)[:, None]
    j_loc = np.arange(2 * D_BLOCK)[None, :]
    m = p_loc + D_BLOCK - j_loc
    idx = []
    for window, dil in D_PAIRS:
        steps = window // dil
        idx.append(np.where((m >= 0) & (m <= steps), _rel_bucket_np(m * dil), -1))
    bias = _bias_tiles(rel_table, np.concatenate(idx, 0).astype(np.int32), F32, tile=D_BLOCK, tr=D_BLOCK)
    outs, lses = [], []
    for gi, (_, dil) in enumerate(D_PAIRS):
        o, lse = _d_group(proj, bias[gi], gi, dil)
        outs.append(o)
        lses.append(lse)
    return _d_merge(outs, lses, w_out.astype(BF16), h)


def _fnorm_kernel(x_ref, g_ref, o_ref):
    x = x_ref[...]
    ms = jnp.mean(x * x, axis=-1, keepdims=True)
    o_ref[...] = x * lax.rsqrt(ms + EPS) * g_ref[...]


def _final_norm(h, g, *, tm=1024):
    n, d = h.shape
    return pl.pallas_call(
        _fnorm_kernel,
        out_shape=jax.ShapeDtypeStruct((n, d), F32),
        grid=(n // tm,),
        in_specs=[pl.BlockSpec((tm, d), lambda i: (i, 0)), pl.BlockSpec((1, d), lambda i: (0, 0))],
        out_specs=pl.BlockSpec((tm, d), lambda i: (i, 0)),
        compiler_params=_cp("parallel"),
        name="final_norm",
    )(h, g.reshape(1, d))


def kernel(x, norm_g, final_g, ffn_w_in, ffn_w_out, rel_table, a_w_in, a_w_out, b_w_in, b_b_in, b_ln_g, b_ln_b, b_w_sp, b_b_sp, b_w_out, c_w_pw1, c_b_pw1, c_w_dw, c_b_dw, c_ln_g, c_ln_b, c_w_pw2, c_b_pw2, d_w_in, d_w_out):
    bsz, s_len, d = x.shape
    depth = norm_g.shape[0]
    h = x.reshape(bsz * s_len, d)
    for i in range(depth):
        kind, j = i % 4, i // 4
        h = _ffn(h, norm_g[i, 0], ffn_w_in[i, 0].astype(BF16), ffn_w_out[i, 0].astype(BF16))
        g = norm_g[i, 1]
        if kind == 0:
            h = _mixer_a(h, g, a_w_in[j], a_w_out[j], rel_table, bsz, s_len)
        elif kind == 1:
            h = _mixer_b(h, g, b_w_in[j], b_b_in[j], b_ln_g[j], b_ln_b[j], b_w_sp[j], b_b_sp[j], b_w_out[j])
        elif kind == 2:
            h = _mixer_c(h, g, c_w_pw1[j], c_b_pw1[j], c_w_dw[j], c_b_dw[j], c_ln_g[j], c_ln_b[j],
                         c_w_pw2[j], c_b_pw2[j], bsz, s_len)
        else:
            h = _mixer_d(h, g, d_w_in[j], d_w_out[j], rel_table, bsz, s_len)
        h = _ffn(h, norm_g[i, 2], ffn_w_in[i, 1].astype(BF16), ffn_w_out[i, 1].astype(BF16))
    return _final_norm(h, final_g).reshape(bsz, s_len, d)
```

```python
import functools
import math

import numpy as np
import jax
import jax.numpy as jnp
from jax import lax
from jax.experimental import pallas as pl
from jax.experimental.pallas import tpu as pltpu

F32, BF16, I32 = jnp.float32, jnp.bfloat16, jnp.int32

D_MODEL = 1024
EPS = 1e-6
D_FF = 2816
REL_BUCKETS = 32
REL_MAX_DIST = 2048
A_HEADS = 8
A_HEAD_DIM = 128
IDX_HEADS = 8
IDX_DIM = 64
TOPK_MAX = 256
B_CHUNK = 128
B_HALF = 3 * D_MODEL
B_GROUPS = 8
C_KERNEL = 31
D_PAIRS = ((128, 1), (512, 4), (2048, 16))
D_HEADS = 8
D_HEAD_DIM = 64
D_BLOCK = 128

LANES = 128
VMEM_LIMIT = 56 * 1024 * 1024
NEG = -1e30
INT_MIN = -2 ** 31
LOG2E = math.log2(math.e)

_NT = (((1,), (1,)), ((), ()))


def _cp(*sem, flags=None):
    return pltpu.CompilerParams(dimension_semantics=sem, vmem_limit_bytes=VMEM_LIMIT, flags=flags)


def _rms_bf16(x, g):
    ms = jnp.mean(x * x, axis=-1, keepdims=True)
    return (x * lax.rsqrt(ms + EPS) * g).astype(BF16)


def _layernorm(x, g, b):
    mu = jnp.mean(x, axis=-1, keepdims=True)
    xc = x - mu
    var = jnp.mean(xc * xc, axis=-1, keepdims=True)
    return xc * lax.rsqrt(var + EPS) * g + b


def _ffn_kernel(x_ref, g_ref, wg_ref, wu_ref, wo_ref, o_ref, xn_ref, acc_ref):
    j = pl.program_id(1)

    @pl.when(j == 0)
    def _():
        xn_ref[...] = _rms_bf16(x_ref[...], g_ref[...])
        acc_ref[...] = jnp.zeros_like(acc_ref)

    xn = xn_ref[...]
    gate = jnp.dot(xn, wg_ref[...], preferred_element_type=F32)
    up = jnp.dot(xn, wu_ref[...], preferred_element_type=F32)
    a = (gate * jax.nn.sigmoid(gate) * up).astype(BF16)
    acc_ref[...] += jnp.dot(a, wo_ref[...], preferred_element_type=F32)

    @pl.when(j == pl.num_programs(1) - 1)
    def _():
        o_ref[...] = x_ref[...] + 0.5 * acc_ref[...]


def _ffn(h, g, w_in, w_out, *, tm=512, tf=1408):
    n, d = h.shape
    nf = D_FF // tf
    return pl.pallas_call(
        _ffn_kernel,
        out_shape=jax.ShapeDtypeStruct((n, d), F32),
        grid=(n // tm, nf),
        in_specs=[
            pl.BlockSpec((tm, d), lambda i, j: (i, 0)),
            pl.BlockSpec((1, d), lambda i, j: (0, 0)),
            pl.BlockSpec((d, tf), lambda i, j: (0, j)),
            pl.BlockSpec((d, tf), lambda i, j: (0, nf + j)),
            pl.BlockSpec((tf, d), lambda i, j: (j, 0)),
        ],
        out_specs=pl.BlockSpec((tm, d), lambda i, j: (i, 0)),
        scratch_shapes=[pltpu.VMEM((tm, d), BF16), pltpu.VMEM((tm, d), F32)],
        compiler_params=_cp("parallel", "arbitrary"),
        name="ffn",
    )(h, g.reshape(1, d), w_in, w_in, w_out)


def _nm_kernel(mode, has_bias, x_ref, g_ref, *refs):
    nw = 2 if mode == "glu" else 1
    w_refs = refs[:nw]
    b_refs = refs[nw:nw + (nw if has_bias else 0)]
    o_ref, xn_ref = refs[-2], refs[-1]

    @pl.when(pl.program_id(1) == 0)
    def _():
        xn_ref[...] = _rms_bf16(x_ref[...], g_ref[...])

    xn = xn_ref[...]
    ys = []
    for k in range(nw):
        y = jnp.dot(xn, w_refs[k][...], preferred_element_type=F32)
        if has_bias:
            y = y + b_refs[k][...]
        ys.append(y)
    if mode == "plain":
        out = ys[0]
    elif mode == "gelu":
        out = jax.nn.gelu(ys[0])
    else:
        out = ys[0] * jax.nn.sigmoid(ys[1])
    if len(o_ref.shape) == 3:
        for hh in range(o_ref.shape[0]):
            o_ref[hh] = out[:, hh * LANES:(hh + 1) * LANES].astype(o_ref.dtype)
    else:
        o_ref[...] = out.astype(o_ref.dtype)


def _norm_matmul(h, g, w, b=None, *, mode="plain", out_dtype=BF16, tm=512, tn=512, head_major=False, name="proj"):
    n, d = h.shape
    m_total = w.shape[1]
    m_out = m_total // 2 if mode == "glu" else m_total
    nj = m_out // tn
    has_bias = b is not None
    if head_major:
        out_shape = jax.ShapeDtypeStruct((m_out // LANES, n, LANES), out_dtype)
        out_spec = pl.BlockSpec((tn // LANES, tm, LANES), lambda i, j: (j, i, 0))
    else:
        out_shape = jax.ShapeDtypeStruct((n, m_out), out_dtype)
        out_spec = pl.BlockSpec((tm, tn), lambda i, j: (i, j))
    in_specs = [pl.BlockSpec((tm, d), lambda i, j: (i, 0)),
                pl.BlockSpec((1, d), lambda i, j: (0, 0)),
                pl.BlockSpec((d, tn), lambda i, j: (0, j))]
    args = [h, g.reshape(1, d), w]
    if mode == "glu":
        in_specs.append(pl.BlockSpec((d, tn), lambda i, j: (0, nj + j)))
        args.append(w)
    if has_bias:
        b2 = b.reshape(1, m_total)
        in_specs.append(pl.BlockSpec((1, tn), lambda i, j: (0, j)))
        args.append(b2)
        if mode == "glu":
            in_specs.append(pl.BlockSpec((1, tn), lambda i, j: (0, nj + j)))
            args.append(b2)
    return pl.pallas_call(
        functools.partial(_nm_kernel, mode, has_bias),
        out_shape=out_shape,
        grid=(n // tm, nj),
        in_specs=in_specs,
        out_specs=out_spec,
        scratch_shapes=[pltpu.VMEM((tm, d), BF16)],
        compiler_params=_cp("parallel", "arbitrary"),
        name=name,
    )(*args)


def _op_kernel(has_bias, y_ref, w_ref, *refs):
    if has_bias:
        b_ref, r_ref, o_ref = refs
    else:
        r_ref, o_ref = refs
    if len(y_ref.shape) == 3:
        y = jnp.concatenate([y_ref[hh] for hh in range(y_ref.shape[0])], axis=1)
    else:
        y = y_ref[...]
    acc = jnp.dot(y, w_ref[...], preferred_element_type=F32)
    if has_bias:
        acc = acc + b_ref[...]
    o_ref[...] = r_ref[...] + acc


def _out_proj(y, w, resid, b=None, *, tm=512, name="out_proj"):
    k, d = w.shape
    n = resid.shape[0]
    has_bias = b is not None
    if y.ndim == 3:
        y_spec = pl.BlockSpec((k // LANES, tm, LANES), lambda i: (0, i, 0))
    else:
        y_spec = pl.BlockSpec((tm, k), lambda i: (i, 0))
    in_specs = [y_spec, pl.BlockSpec((k, d), lambda i: (0, 0))]
    args = [y, w]
    if has_bias:
        in_specs.append(pl.BlockSpec((1, d), lambda i: (0, 0)))
        args.append(b.reshape(1, d))
    in_specs.append(pl.BlockSpec((tm, d), lambda i: (i, 0)))
    args.append(resid)
    return pl.pallas_call(
        functools.partial(_op_kernel, has_bias),
        out_shape=jax.ShapeDtypeStruct((n, d), F32),
        grid=(n // tm,),
        in_specs=in_specs,
        out_specs=pl.BlockSpec((tm, d), lambda i: (i, 0)),
        compiler_params=_cp("parallel"),
        name=name,
    )(*args)


def _rel_bucket_np(dist):
    max_exact = REL_BUCKETS // 2
    d = np.maximum(dist, 0)
    df = np.maximum(d, 1).astype(np.float32)
    large = max_exact + (np.log(df / np.float32(max_exact)) / np.float32(math.log(REL_MAX_DIST / max_exact))
                         * np.float32(REL_BUCKETS - max_exact)).astype(np.int32)
    large = np.minimum(large, REL_BUCKETS - 1)
    return np.where(d < max_exact, d, large).astype(np.int32)


def _bias_kernel(scale, tab_ref, idx_ref, o_ref):
    idx = idx_ref[...]
    for h in range(o_ref.shape[1]):
        acc = jnp.zeros(idx.shape, F32)
        for b in range(REL_BUCKETS):
            acc = jnp.where(idx == b, tab_ref[b, h] * scale, acc)
        o_ref[0, h] = jnp.where(idx < 0, NEG, acc).astype(o_ref.dtype)


def _bias_tiles(rel_table, bucket_idx, out_dtype, *, tile, tr, scale=1.0):
    r, c = bucket_idx.shape
    nh = rel_table.shape[1]
    per = tile // tr
    return pl.pallas_call(
        functools.partial(_bias_kernel, scale),
        out_shape=jax.ShapeDtypeStruct((r // tile, nh, tile, c), out_dtype),
        grid=(r // tr,),
        in_specs=[pl.BlockSpec(memory_space=pltpu.SMEM),
                  pl.BlockSpec((tr, c), lambda i: (i, 0))],
        out_specs=pl.BlockSpec((1, nh, tr, c), lambda i: (i // per, 0, i % per, 0)),
        compiler_params=_cp("parallel"),
        name="bias_tiles",
    )(rel_table, jnp.asarray(bucket_idx))


def _asel_kernel(qi_ref, wi_ref, ki_ref, m_ref, keys_ref, wb_ref, x_ref, *, tq, cw, s_len, topk):
    qb = pl.program_id(1)
    spc = cw // LANES
    n_chunks = (qb * tq + tq + cw - 1) // cw
    n_slabs = n_chunks * spc
    nslab_all = s_len // LANES

    wi = wi_ref[0] * (IDX_HEADS ** -0.5 * IDX_DIM ** -0.5)
    for h in range(IDX_HEADS):
        wb_ref[h] = jnp.broadcast_to(wi[:, h:h + 1], (tq, LANES))
    q_pos = qb * tq + lax.broadcasted_iota(I32, (tq, cw), 0)
    lane_pos = lax.broadcasted_iota(I32, (tq, cw), 1)

    def build(c, carry):
        k0 = pl.multiple_of(c * cw, cw)
        kc = ki_ref[0, pl.ds(k0, cw), :]
        sc = jnp.zeros((tq, cw), F32)
        for h in range(IDX_HEADS):
            r = lax.dot_general(qi_ref[0, :, h * IDX_DIM:(h + 1) * IDX_DIM], kc, _NT,
                                preferred_element_type=F32)
            sc = sc + jnp.maximum(r, 0.0) * pltpu.repeat(wb_ref[h], spc, axis=1)
        sc = sc + 0.0
        bits = pltpu.bitcast(sc, I32)
        key = bits ^ ((bits >> 31) & 0x7FFFFFFF)
        key = jnp.where(k0 + lane_pos <= q_pos, key, INT_MIN)
        for j in range(spc):
            keys_ref[c * spc + j] = key[:, j * LANES:(j + 1) * LANES]
        return carry

    lax.fori_loop(0, n_chunks, build, 0)

    def lane_sum(acc):
        return jnp.sum(acc.astype(F32), axis=1, keepdims=True)

    def slab_loop(body, init):
        def chunk(c, carry):
            for j in range(spc):
                carry = body(c * spc + j, carry)
            return carry
        return lax.fori_loop(0, n_chunks, chunk, init)

    def pass_body(i, theta):
        cand = theta + lax.shift_left(jnp.int32(1), 31 - i)

        def body(c, acc):
            return acc + jnp.where(keys_ref[c] >= cand, 1, 0)

        cnt = lane_sum(slab_loop(body, jnp.zeros((tq, LANES), I32)))
        return jnp.where(cnt >= topk, cand, theta)

    tau = lax.fori_loop(0, 32, pass_body, jnp.full((tq, LANES), INT_MIN, I32))

    def ge_body(c, carry):
        g_acc, e_acc = carry
        k = keys_ref[c]
        return (g_acc + jnp.where(k > tau, 1, 0), e_acc + jnp.where(k == tau, 1, 0))

    zero = jnp.zeros((tq, LANES), I32)
    g_acc, e_acc = slab_loop(ge_body, (zero, zero))
    r_need = topk - lane_sum(g_acc)
    excess = lane_sum(e_acc) - r_need
    has_real_tau = tau > INT_MIN
    x_ref[...] = jnp.where(has_real_tau, s_len, -1)
    lane = lax.broadcasted_iota(I32, (tq, LANES), 1)

    @pl.when(jnp.max(jnp.where(has_real_tau[:, :1], excess, 0.0)) > 0.0)
    def _():
        nbits = int(math.log2(s_len)) + 1

        def tie_pass(i, x):
            cand = x + lax.shift_left(jnp.int32(1), nbits - 1 - i)

            def body(c, acc):
                idx = c * LANES + lane
                return acc + jnp.where((keys_ref[c] == tau) & (idx < cand), 1, 0)

            cnt = lane_sum(slab_loop(body, zero))
            return jnp.where(cnt < r_need, cand, x)

        x = lax.fori_loop(0, nbits, tie_pass, zero)
        x_ref[...] = jnp.where(has_real_tau, x, -1)

    x_cut = x_ref[...]

    def emit(c, carry):
        k = keys_ref[c]
        idx = c * LANES + lane
        sel = (k > tau) | ((k == tau) & (idx <= x_cut))
        m_ref[0, c] = jnp.where(sel, 0.0, NEG).astype(m_ref.dtype)
        return carry

    slab_loop(emit, 0)

    def fill(c, carry):
        m_ref[0, c] = jnp.full((tq, LANES), NEG, m_ref.dtype)
        return carry

    lax.fori_loop(n_slabs, nslab_all, fill, 0)


def _a_select(qi, wi, ki, *, tq=128, cw=1024):
    bsz, s_len, _ = qi.shape
    topk = min(TOPK_MAX, s_len // 4)
    nslab = s_len // LANES
    return pl.pallas_call(
        functools.partial(_asel_kernel, tq=tq, cw=cw, s_len=s_len, topk=float(topk)),
        out_shape=jax.ShapeDtypeStruct((bsz, nslab, s_len, LANES), BF16),
        grid=(bsz, s_len // tq),
        in_specs=[pl.BlockSpec((1, tq, IDX_HEADS * IDX_DIM), lambda b, i: (b, i, 0)),
                  pl.BlockSpec((1, tq, IDX_HEADS), lambda b, i: (b, i, 0)),
                  pl.BlockSpec((1, s_len, IDX_DIM), lambda b, i: (b, 0, 0))],
        out_specs=pl.BlockSpec((1, nslab, tq, LANES), lambda b, i: (b, 0, i, 0)),
        scratch_shapes=[pltpu.VMEM((nslab, tq, LANES), I32),
                        pltpu.VMEM((IDX_HEADS, tq, LANES), F32),
                        pltpu.VMEM((tq, LANES), I32)],
        compiler_params=_cp("parallel", "arbitrary"),
        name="a_select",
    )(qi, wi, ki)


def _attn_kernel(qt_ref, kt_ref, tab_ref, q_ref, k_ref, v_ref, msk_ref, bias_ref, o_ref,
                 m_sc, l_sc, acc_sc, mk_sc, s0_sc, s1_sc, p0_sc, p1_sc, al0_sc, al1_sc, *, t, nd_far):
    step = pl.program_id(1)
    qi = qt_ref[step]
    ki = kt_ref[step]

    @pl.when(ki == 0)
    def _():
        m_sc[...] = jnp.full(m_sc.shape, NEG, F32)
        l_sc[...] = jnp.zeros_like(l_sc)
        acc_sc[...] = jnp.zeros_like(acc_sc)

    mk_sc[...] = jnp.concatenate([msk_ref[0, j] for j in range(t // LANES)], axis=1).astype(F32)
    s_sc, p_sc, al_sc = (s0_sc, s1_sc), (p0_sc, p1_sc), (al0_sc, al1_sc)

    def scores(h, far):
        s = lax.dot_general(q_ref[h, 0], k_ref[h, 0], _NT, preferred_element_type=F32) + mk_sc[...]
        if not far:
            s = s + bias_ref[0, h].astype(F32)
        s_sc[h & 1][...] = s

    def softmax(h, far):
        s = s_sc[h & 1][...]
        c = tab_ref[REL_BUCKETS - 1, h] * LOG2E if far else 0.0
        m_prev = m_sc[h]
        m_cur = jnp.broadcast_to(jnp.max(s, axis=-1, keepdims=True), (t, LANES))
        m_new = jnp.maximum(m_prev, m_cur + c)
        alpha = jnp.exp2(m_prev - m_new)
        p = jnp.exp2(s - pltpu.repeat(m_new - c, t // LANES, axis=1))
        p_part = p[:, :LANES]
        for j in range(1, t // LANES):
            p_part = p_part + p[:, j * LANES:(j + 1) * LANES]
        l_sc[h] = alpha * l_sc[h] + p_part
        m_sc[h] = m_new
        al_sc[h & 1][...] = alpha
        p_sc[h & 1][...] = p.astype(BF16)

    def values(h):
        acc_sc[h] = al_sc[h & 1][...] * acc_sc[h] + jnp.dot(p_sc[h & 1][...], v_ref[h, 0],
                                                            preferred_element_type=F32)

    one = qt_ref[0] + 1

    def heads(far):
        for st in range(A_HEADS + 2):
            def stage(i, carry, st=st):
                if st < A_HEADS:
                    scores(st, far)
                if 1 <= st <= A_HEADS:
                    softmax(st - 1, far)
                if st >= 2:
                    values(st - 2)
                return carry

            lax.fori_loop(0, one, stage, 0)

    @pl.when(qi - ki >= nd_far)
    def _():
        heads(True)

    @pl.when(qi - ki < nd_far)
    def _():
        heads(False)

    @pl.when(ki == qi)
    def _():
        for h in range(A_HEADS):
            l = jnp.sum(l_sc[h], axis=-1, keepdims=True)
            o_ref[h, 0] = (acc_sc[h] * (1.0 / l)).astype(o_ref.dtype)


def _a_attention(qkv, mask4, bias, rel_table, *, t, nd_far):
    _, bsz, s_len, e = qkv.shape
    nq = s_len // t
    nd = bias.shape[0]
    qt = np.concatenate([np.full(i + 1, i, np.int32) for i in range(nq)])
    kt = np.concatenate([np.arange(i + 1, dtype=np.int32) for i in range(nq)])
    spt = t // LANES
    grid_spec = pltpu.PrefetchScalarGridSpec(
        num_scalar_prefetch=2,
        grid=(bsz, len(qt)),
        in_specs=[
            pl.BlockSpec(memory_space=pltpu.SMEM),
            pl.BlockSpec((A_HEADS, 1, t, e), lambda b, s, qt, kt: (0, b, qt[s], 0)),
            pl.BlockSpec((A_HEADS, 1, t, e), lambda b, s, qt, kt: (1, b, kt[s], 0)),
            pl.BlockSpec((A_HEADS, 1, t, e), lambda b, s, qt, kt: (2, b, kt[s], 0)),
            pl.BlockSpec((1, spt, t, LANES), lambda b, s, qt, kt: (b, kt[s], qt[s], 0)),
            pl.BlockSpec((1, A_HEADS, t, t),
                         lambda b, s, qt, kt: (jnp.minimum(qt[s] - kt[s], nd - 1), 0, 0, 0)),
        ],
        out_specs=pl.BlockSpec((A_HEADS, 1, t, e), lambda b, s, qt, kt: (0, b, qt[s], 0)),
        scratch_shapes=[pltpu.VMEM((A_HEADS, t, LANES), F32),
                        pltpu.VMEM((A_HEADS, t, LANES), F32),
                        pltpu.VMEM((A_HEADS, t, e), F32),
                        pltpu.VMEM((t, t), F32),
                        pltpu.VMEM((t, t), F32), pltpu.VMEM((t, t), F32),
                        pltpu.VMEM((t, t), BF16), pltpu.VMEM((t, t), BF16),
                        pltpu.VMEM((t, LANES), F32), pltpu.VMEM((t, LANES), F32)],
    )
    return pl.pallas_call(
        functools.partial(_attn_kernel, t=t, nd_far=nd_far),
        out_shape=jax.ShapeDtypeStruct((A_HEADS, bsz, s_len, e), BF16),
        grid_spec=grid_spec,
        compiler_params=_cp("parallel", "arbitrary"),
        name="a_attention",
    )(jnp.asarray(qt), jnp.asarray(kt), rel_table, qkv, qkv, qkv, mask4, bias)


def _mixer_a(h, g, w_in, w_out, rel_table, bsz, s_len):
    n = h.shape[0]
    d = D_MODEL
    n_idx = IDX_HEADS * IDX_DIM + IDX_DIM + IDX_HEADS
    w_qkv = jnp.concatenate([w_in[:, :d] * (A_HEAD_DIM ** -0.5 * LOG2E), w_in[:, d:3 * d]], axis=1).astype(BF16)
    w_idx = jnp.pad(w_in[:, 3 * d:], ((0, 0), (0, 640 - n_idx))).astype(BF16)
    qkv = _norm_matmul(h, g, w_qkv, out_dtype=BF16, tn=1024, head_major=True, name="a_qkv")
    idx = _norm_matmul(h, g, w_idx, out_dtype=F32, tn=640, name="a_idx")
    qkv = qkv.reshape(3 * A_HEADS, bsz, s_len, A_HEAD_DIM)
    idx = idx.reshape(bsz, s_len, 640)
    nqi = IDX_HEADS * IDX_DIM
    qi = idx[..., :nqi].astype(BF16)
    ki = idx[..., nqi:nqi + IDX_DIM].astype(BF16)
    wi = idx[..., nqi + IDX_DIM:n_idx]
    mask4 = _a_select(qi, wi, ki)

    t = min(512, s_len)
    nd_far = 0
    while _rel_bucket_np(np.array(nd_far * t - (t - 1))) < REL_BUCKETS - 1:
        nd_far += 1
    nd = min(nd_far, s_len // t)
    ii = np.arange(t)[:, None] - np.arange(t)[None, :]
    dist = (np.arange(nd)[:, None, None] * t + ii[None]).reshape(nd * t, t)
    bias = _bias_tiles(rel_table, _rel_bucket_np(dist), BF16, tile=t, tr=t // 2, scale=LOG2E)
    o = _a_attention(qkv, mask4, bias, rel_table, t=t, nd_far=nd_far)
    return _out_proj(o.reshape(A_HEADS, n, A_HEAD_DIM), w_out.astype(BF16), h, name="a_out")


def _bsp_kernel(u_ref, v_ref, lg_ref, lb_ref, wsp_ref, bsp_ref, o_ref, wt_ref, *, rows):
    @pl.when(pl.program_id(0) == 0)
    def _():
        r = lax.broadcasted_iota(I32, (B_CHUNK, B_CHUNK), 0)
        c = lax.broadcasted_iota(I32, (B_CHUNK, B_CHUNK), 1)
        for g in range(B_GROUPS):
            wt_ref[g] = jnp.where(r >= c, wsp_ref[g], 0.0).astype(BF16)

    vn = _layernorm(v_ref[...].astype(F32), lg_ref[...], lb_ref[...]).astype(BF16)
    gw = B_HALF // B_GROUPS
    for c in range(rows // B_CHUNK):
        rs = slice(c * B_CHUNK, (c + 1) * B_CHUNK)
        for g in range(B_GROUPS):
            cs = slice(g * gw, (g + 1) * gw)
            sv = jnp.dot(wt_ref[g], vn[rs, cs], preferred_element_type=F32) + bsp_ref[g]
            o_ref[rs, cs] = (u_ref[rs, cs].astype(F32) * sv).astype(o_ref.dtype)


def _b_spatial(z, ln_g, ln_b, w_sp, b_sp, *, rows=512):
    n = z.shape[0]
    return pl.pallas_call(
        functools.partial(_bsp_kernel, rows=rows),
        out_shape=jax.ShapeDtypeStruct((n, B_HALF), BF16),
        grid=(n // rows,),
        in_specs=[pl.BlockSpec((rows, B_HALF), lambda i: (i, 0)),
                  pl.BlockSpec((rows, B_HALF), lambda i: (i, 1)),
                  pl.BlockSpec((1, B_HALF), lambda i: (0, 0)),
                  pl.BlockSpec((1, B_HALF), lambda i: (0, 0)),
                  pl.BlockSpec((B_GROUPS, B_CHUNK, B_CHUNK), lambda i: (0, 0, 0)),
                  pl.BlockSpec((B_GROUPS, B_CHUNK, 1), lambda i: (0, 0, 0))],
        out_specs=pl.BlockSpec((rows, B_HALF), lambda i: (i, 0)),
        scratch_shapes=[pltpu.VMEM((B_GROUPS, B_CHUNK, B_CHUNK), BF16)],
        compiler_params=_cp("arbitrary"),
        name="b_spatial",
    )(z, z, ln_g.reshape(1, B_HALF), ln_b.reshape(1, B_HALF), w_sp, b_sp.reshape(B_GROUPS, B_CHUNK, 1))


def _mixer_b(h, g, w_in, b_in, ln_g, ln_b, w_sp, b_sp, w_out):
    z = _norm_matmul(h, g, w_in.astype(BF16), b_in, mode="gelu", out_dtype=BF16, tn=1024, name="b_in")
    y = _b_spatial(z, ln_g, ln_b, w_sp, b_sp)
    return _out_proj(y, w_out.astype(BF16), h, name="b_out")


C_HALO = 32


def _conv_kernel(cur_ref, halo_ref, wdw_ref, bdw_ref, lg_ref, lb_ref, w2_ref, b2_ref, r_ref, o_ref,
                 ext_ref, yc_ref, *, ts, rt, ct):
    i = pl.program_id(1)
    ext_ref[C_HALO:, :] = cur_ref[0]

    @pl.when(i == 0)
    def _():
        ext_ref[:C_HALO, :] = jnp.zeros((C_HALO, D_MODEL), F32)

    @pl.when(i > 0)
    def _():
        ext_ref[:C_HALO, :] = halo_ref[0]

    off = C_HALO - (C_KERNEL - 1)

    def tile(it, carry):
        r0 = pl.multiple_of((it // (D_MODEL // ct)) * rt, rt)
        c0 = pl.multiple_of((it % (D_MODEL // ct)) * ct, ct)
        nwin = rt + C_HALO
        win = ext_ref[pl.ds(r0, nwin), pl.ds(c0, ct)]
        acc = jnp.zeros((rt, ct), F32) + bdw_ref[:, pl.ds(c0, ct)]
        for b in range(8):
            rolled = pltpu.roll(win, nwin - (off + b), axis=0)
            for a in range((C_KERNEL - b + 7) // 8):
                j = 8 * a + b
                acc = acc + rolled[8 * a:8 * a + rt] * wdw_ref[j:j + 1, pl.ds(c0, ct)]
        yc_ref[pl.ds(r0, rt), pl.ds(c0, ct)] = acc
        return carry

    lax.fori_loop(0, (ts // rt) * (D_MODEL // ct), tile, 0)
    y = _layernorm(yc_ref[...], lg_ref[...], lb_ref[...])
    y = (y * jax.nn.sigmoid(y)).astype(BF16)
    o_ref[0] = r_ref[0] + jnp.dot(y, w2_ref[...], preferred_element_type=F32) + b2_ref[...]


def _c_conv(y1, w_dw, b_dw, ln_g, ln_b, w2, b2, resid, *, ts=512, rt=32, ct=512):
    bsz, s_len, d = y1.shape
    hb = ts // C_HALO
    vec = lambda a: a.reshape(1, d)
    return pl.pallas_call(
        functools.partial(_conv_kernel, ts=ts, rt=rt, ct=ct),
        out_shape=jax.ShapeDtypeStruct((bsz, s_len, d), F32),
        grid=(bsz, s_len // ts),
        in_specs=[pl.BlockSpec((1, ts, d), lambda b, i: (b, i, 0)),
                  pl.BlockSpec((1, C_HALO, d), lambda b, i: (b, jnp.maximum(i * hb - 1, 0), 0)),
                  pl.BlockSpec((C_KERNEL, d), lambda b, i: (0, 0)),
                  pl.BlockSpec((1, d), lambda b, i: (0, 0)),
                  pl.BlockSpec((1, d), lambda b, i: (0, 0)),
                  pl.BlockSpec((1, d), lambda b, i: (0, 0)),
                  pl.BlockSpec((d, d), lambda b, i: (0, 0)),
                  pl.BlockSpec((1, d), lambda b, i: (0, 0)),
                  pl.BlockSpec((1, ts, d), lambda b, i: (b, i, 0))],
        out_specs=pl.BlockSpec((1, ts, d), lambda b, i: (b, i, 0)),
        scratch_shapes=[pltpu.VMEM((ts + C_HALO, d), F32), pltpu.VMEM((ts, d), F32)],
        compiler_params=_cp("parallel", "parallel"),
        name="c_conv",
    )(y1, y1, w_dw, vec(b_dw), vec(ln_g), vec(ln_b), w2, vec(b2), resid)


def _mixer_c(h, g, w_pw1, b_pw1, w_dw, b_dw, ln_g, ln_b, w_pw2, b_pw2, bsz, s_len):
    n, d = h.shape
    y1 = _norm_matmul(h, g, w_pw1.astype(BF16), b_pw1, mode="glu", out_dtype=F32, tn=512, name="c_pw1")
    out = _c_conv(y1.reshape(bsz, s_len, d), w_dw, b_dw, ln_g, ln_b, w_pw2.astype(BF16), b_pw2,
                  h.reshape(bsz, s_len, d))
    return out.reshape(n, d)


def _dattn_kernel(q_ref, kc_ref, kp_ref, vc_ref, vp_ref, bias_ref, o_ref, lse_ref):
    nb = pl.program_id(2)
    first = jnp.where(nb == 0, NEG, 0.0)
    e = D_HEAD_DIM
    for h in range(D_HEADS):
        hs = slice(h * e, (h + 1) * e)
        q = q_ref[0, :, hs]
        kk = jnp.concatenate([kp_ref[0, :, hs], kc_ref[0, :, hs]], axis=0)
        vv = jnp.concatenate([vp_ref[0, :, hs], vc_ref[0, :, hs]], axis=0)
        s = lax.dot_general(q, kk, _NT, preferred_element_type=F32) * (e ** -0.5) + bias_ref[h]
        col = lax.broadcasted_iota(I32, s.shape, 1)
        s = s + jnp.where(col < D_BLOCK, first, 0.0)
        m = jnp.max(s, axis=-1, keepdims=True)
        p = jnp.exp(s - m)
        l = jnp.sum(p, axis=-1, keepdims=True)
        o = jnp.dot(p.astype(BF16), vv, preferred_element_type=F32) / l
        o_ref[0, :, hs] = o
        lse_ref[0, :, hs] = jnp.broadcast_to(m + jnp.log(l), (D_BLOCK, e))


def _d_group(proj, bias_g, gidx, dil):
    bsz, s_len, c = proj.shape
    n_sub = s_len // dil
    nb = n_sub // D_BLOCK
    w = D_HEADS * D_HEAD_DIM
    ng = len(D_PAIRS)
    cb = c // w
    view = proj.reshape(bsz, n_sub, dil * c)
    qcol = lambda r: r * cb + gidx
    kcol = lambda r: r * cb + ng + gidx
    vcol = lambda r: r * cb + 2 * ng + gidx
    blk = (1, D_BLOCK, w)
    prev = lambda n: jnp.maximum(n - 1, 0)
    out_sds = jax.ShapeDtypeStruct((bsz, n_sub, dil * w), F32)
    o, lse = pl.pallas_call(
        _dattn_kernel,
        out_shape=(out_sds, out_sds),
        grid=(bsz, dil, nb),
        in_specs=[pl.BlockSpec(blk, lambda b, r, n: (b, n, qcol(r))),
                  pl.BlockSpec(blk, lambda b, r, n: (b, n, kcol(r))),
                  pl.BlockSpec(blk, lambda b, r, n: (b, prev(n), kcol(r))),
                  pl.BlockSpec(blk, lambda b, r, n: (b, n, vcol(r))),
                  pl.BlockSpec(blk, lambda b, r, n: (b, prev(n), vcol(r))),
                  pl.BlockSpec((D_HEADS, D_BLOCK, 2 * D_BLOCK), lambda b, r, n: (0, 0, 0))],
        out_specs=(pl.BlockSpec(blk, lambda b, r, n: (b, n, r)),
                   pl.BlockSpec(blk, lambda b, r, n: (b, n, r))),
        compiler_params=_cp("parallel", "parallel", "arbitrary"),
        name=f"d_attn_g{gidx}",
    )(view, view, view, view, view, bias_g)
    return o.reshape(bsz * s_len, w), lse.reshape(bsz * s_len, w)


def _dmerge_kernel(o0, o1, o2, l0, l1, l2, w_ref, r_ref, out_ref):
    a0, a1, a2 = l0[...], l1[...], l2[...]
    mx = jnp.maximum(jnp.maximum(a0, a1), a2)
    e0, e1, e2 = jnp.exp(a0 - mx), jnp.exp(a1 - mx), jnp.exp(a2 - mx)
    y = (e0 * o0[...] + e1 * o1[...] + e2 * o2[...]) / (e0 + e1 + e2)
    out_ref[...] = r_ref[...] + jnp.dot(y.astype(BF16), w_ref[...], preferred_element_type=F32)


def _d_merge(outs, lses, w_out, resid, *, tm=512):
    n, w = outs[0].shape
    d = resid.shape[1]
    row = pl.BlockSpec((tm, w), lambda i: (i, 0))
    return pl.pallas_call(
        _dmerge_kernel,
        out_shape=jax.ShapeDtypeStruct((n, d), F32),
        grid=(n // tm,),
        in_specs=[row] * 6 + [pl.BlockSpec((w, d), lambda i: (0, 0)),
                              pl.BlockSpec((tm, d), lambda i: (i, 0))],
        out_specs=pl.BlockSpec((tm, d), lambda i: (i, 0)),
        compiler_params=_cp("parallel"),
        name="d_merge",
    )(*outs, *lses, w_out, resid)


def _mixer_d(h, g, w_in, w_out, rel_table, bsz, s_len):
    proj = _norm_matmul(h, g, w_in.astype(BF16), out_dtype=BF16, tn=1536, name="d_in")
    proj = proj.reshape(bsz, s_len, proj.shape[1])
    p_loc = np.arange(D_BLOCK)[:, None]
    j_loc = np.arange(2 * D_BLOCK)[None, :]
    m = p_loc + D_BLOCK - j_loc
    idx = []
    for window, dil in D_PAIRS:
        steps = window // dil
        idx.append(np.where((m >= 0) & (m <= steps), _rel_bucket_np(m * dil), -1))
    bias = _bias_tiles(rel_table, np.concatenate(idx, 0).astype(np.int32), F32, tile=D_BLOCK, tr=D_BLOCK)
    outs, lses = [], []
    for gi, (_, dil) in enumerate(D_PAIRS):
        o, lse = _d_group(proj, bias[gi], gi, dil)
        outs.append(o)
        lses.append(lse)
    return _d_merge(outs, lses, w_out.astype(BF16), h)


def _fnorm_kernel(x_ref, g_ref, o_ref):
    x = x_ref[...]
    ms = jnp.mean(x * x, axis=-1, keepdims=True)
    o_ref[...] = x * lax.rsqrt(ms + EPS) * g_ref[...]


def _final_norm(h, g, *, tm=1024):
    n, d = h.shape
    return pl.pallas_call(
        _fnorm_kernel,
        out_shape=jax.ShapeDtypeStruct((n, d), F32),
        grid=(n // tm,),
        in_specs=[pl.BlockSpec((tm, d), lambda i: (i, 0)), pl.BlockSpec((1, d), lambda i: (0, 0))],
        out_specs=pl.BlockSpec((tm, d), lambda i: (i, 0)),
        compiler_params=_cp("parallel"),
        name="final_norm",
    )(h, g.reshape(1, d))


def kernel(x, norm_g, final_g, ffn_w_in, ffn_w_out, rel_table, a_w_in, a_w_out, b_w_in, b_b_in, b_ln_g, b_ln_b, b_w_sp, b_b_sp, b_w_out, c_w_pw1, c_b_pw1, c_w_dw, c_b_dw, c_ln_g, c_ln_b, c_w_pw2, c_b_pw2, d_w_in, d_w_out):
    bsz, s_len, d = x.shape
    depth = norm_g.shape[0]
    h = x.reshape(bsz * s_len, d)
    for i in range(depth):
        kind, j = i % 4, i // 4
        h = _ffn(h, norm_g[i, 0], ffn_w_in[i, 0].astype(BF16), ffn_w_out[i, 0].astype(BF16))
        g = norm_g[i, 1]
        if kind == 0:
            h = _mixer_a(h, g, a_w_in[j], a_w_out[j], rel_table, bsz, s_len)
        elif kind == 1:
            h = _mixer_b(h, g, b_w_in[j], b_b_in[j], b_ln_g[j], b_ln_b[j], b_w_sp[j], b_b_sp[j], b_w_out[j])
        elif kind == 2:
            h = _mixer_c(h, g, c_w_pw1[j], c_b_pw1[j], c_w_dw[j], c_b_dw[j], c_ln_g[j], c_ln_b[j],
                         c_w_pw2[j], c_b_pw2[j], bsz, s_len)
        else:
            h = _mixer_d(h, g, d_w_in[j], d_w_out[j], rel_table, bsz, s_len)
        h = _ffn(h, norm_g[i, 2], ffn_w_in[i, 1].astype(BF16), ffn_w_out[i, 1].astype(BF16))
    return _final_norm(h, final_g).reshape(bsz, s_len, d)
```

```python
import functools
import math

import numpy as np
import jax
import jax.numpy as jnp
from jax import lax
from jax.experimental import pallas as pl
from jax.experimental.pallas import tpu as pltpu

F32, BF16, I32 = jnp.float32, jnp.bfloat16, jnp.int32

D_MODEL = 1024
EPS = 1e-6
D_FF = 2816
REL_BUCKETS = 32
REL_MAX_DIST = 2048
A_HEADS = 8
A_HEAD_DIM = 128
IDX_HEADS = 8
IDX_DIM = 64
TOPK_MAX = 256
B_CHUNK = 128
B_HALF = 3 * D_MODEL
B_GROUPS = 8
C_KERNEL = 31
D_PAIRS = ((128, 1), (512, 4), (2048, 16))
D_HEADS = 8
D_HEAD_DIM = 64
D_BLOCK = 128

LANES = 128
VMEM_LIMIT = 56 * 1024 * 1024
NEG = -1e30
INT_MIN = -2 ** 31
LOG2E = math.log2(math.e)

_NT = (((1,), (1,)), ((), ()))


def _cp(*sem, flags=None):
    return pltpu.CompilerParams(dimension_semantics=sem, vmem_limit_bytes=VMEM_LIMIT, flags=flags)


def _rms_bf16(x, g):
    ms = jnp.mean(x * x, axis=-1, keepdims=True)
    return (x * lax.rsqrt(ms + EPS) * g).astype(BF16)


def _layernorm(x, g, b):
    mu = jnp.mean(x, axis=-1, keepdims=True)
    xc = x - mu
    var = jnp.mean(xc * xc, axis=-1, keepdims=True)
    return xc * lax.rsqrt(var + EPS) * g + b


def _ffn_kernel(x_ref, g_ref, wg_ref, wu_ref, wo_ref, o_ref, xn_ref, acc_ref):
    j = pl.program_id(1)

    @pl.when(j == 0)
    def _():
        xn_ref[...] = _rms_bf16(x_ref[...], g_ref[...])
        acc_ref[...] = jnp.zeros_like(acc_ref)

    xn = xn_ref[...]
    gate = jnp.dot(xn, wg_ref[...], preferred_element_type=F32)
    up = jnp.dot(xn, wu_ref[...], preferred_element_type=F32)
    a = (gate * jax.nn.sigmoid(gate) * up).astype(BF16)
    acc_ref[...] += jnp.dot(a, wo_ref[...], preferred_element_type=F32)

    @pl.when(j == pl.num_programs(1) - 1)
    def _():
        o_ref[...] = x_ref[...] + 0.5 * acc_ref[...]


def _ffn(h, g, w_in, w_out, *, tm=512, tf=1408):
    n, d = h.shape
    nf = D_FF // tf
    return pl.pallas_call(
        _ffn_kernel,
        out_shape=jax.ShapeDtypeStruct((n, d), F32),
        grid=(n // tm, nf),
        in_specs=[
            pl.BlockSpec((tm, d), lambda i, j: (i, 0)),
            pl.BlockSpec((1, d), lambda i, j: (0, 0)),
            pl.BlockSpec((d, tf), lambda i, j: (0, j)),
            pl.BlockSpec((d, tf), lambda i, j: (0, nf + j)),
            pl.BlockSpec((tf, d), lambda i, j: (j, 0)),
        ],
        out_specs=pl.BlockSpec((tm, d), lambda i, j: (i, 0)),
        scratch_shapes=[pltpu.VMEM((tm, d), BF16), pltpu.VMEM((tm, d), F32)],
        compiler_params=_cp("parallel", "arbitrary"),
        name="ffn",
    )(h, g.reshape(1, d), w_in, w_in, w_out)


def _nm_kernel(mode, has_bias, x_ref, g_ref, *refs):
    nw = 2 if mode == "glu" else 1
    w_refs = refs[:nw]
    b_refs = refs[nw:nw + (nw if has_bias else 0)]
    o_ref, xn_ref = refs[-2], refs[-1]

    @pl.when(pl.program_id(1) == 0)
    def _():
        xn_ref[...] = _rms_bf16(x_ref[...], g_ref[...])

    xn = xn_ref[...]
    ys = []
    for k in range(nw):
        y = jnp.dot(xn, w_refs[k][...], preferred_element_type=F32)
        if has_bias:
            y = y + b_refs[k][...]
        ys.append(y)
    if mode == "plain":
        out = ys[0]
    elif mode == "gelu":
        out = jax.nn.gelu(ys[0])
    else:
        out = ys[0] * jax.nn.sigmoid(ys[1])
    if len(o_ref.shape) == 3:
        for hh in range(o_ref.shape[0]):
            o_ref[hh] = out[:, hh * LANES:(hh + 1) * LANES].astype(o_ref.dtype)
    else:
        o_ref[...] = out.astype(o_ref.dtype)


def _norm_matmul(h, g, w, b=None, *, mode="plain", out_dtype=BF16, tm=512, tn=512, head_major=False, name="proj"):
    n, d = h.shape
    m_total = w.shape[1]
    m_out = m_total // 2 if mode == "glu" else m_total
    nj = m_out // tn
    has_bias = b is not None
    if head_major:
        out_shape = jax.ShapeDtypeStruct((m_out // LANES, n, LANES), out_dtype)
        out_spec = pl.BlockSpec((tn // LANES, tm, LANES), lambda i, j: (j, i, 0))
    else:
        out_shape = jax.ShapeDtypeStruct((n, m_out), out_dtype)
        out_spec = pl.BlockSpec((tm, tn), lambda i, j: (i, j))
    in_specs = [pl.BlockSpec((tm, d), lambda i, j: (i, 0)),
                pl.BlockSpec((1, d), lambda i, j: (0, 0)),
                pl.BlockSpec((d, tn), lambda i, j: (0, j))]
    args = [h, g.reshape(1, d), w]
    if mode == "glu":
        in_specs.append(pl.BlockSpec((d, tn), lambda i, j: (0, nj + j)))
        args.append(w)
    if has_bias:
        b2 = b.reshape(1, m_total)
        in_specs.append(pl.BlockSpec((1, tn), lambda i, j: (0, j)))
        args.append(b2)
        if mode == "glu":
            in_specs.append(pl.BlockSpec((1, tn), lambda i, j: (0, nj + j)))
            args.append(b2)
    return pl.pallas_call(
        functools.partial(_nm_kernel, mode, has_bias),
        out_shape=out_shape,
        grid=(n // tm, nj),
        in_specs=in_specs,
        out_specs=out_spec,
        scratch_shapes=[pltpu.VMEM((tm, d), BF16)],
        compiler_params=_cp("parallel", "arbitrary"),
        name=name,
    )(*args)


def _op_kernel(has_bias, y_ref, w_ref, *refs):
    if has_bias:
        b_ref, r_ref, o_ref = refs
    else:
        r_ref, o_ref = refs
    if len(y_ref.shape) == 3:
        y = jnp.concatenate([y_ref[hh] for hh in range(y_ref.shape[0])], axis=1)
    else:
        y = y_ref[...]
    acc = jnp.dot(y, w_ref[...], preferred_element_type=F32)
    if has_bias:
        acc = acc + b_ref[...]
    o_ref[...] = r_ref[...] + acc


def _out_proj(y, w, resid, b=None, *, tm=512, name="out_proj"):
    k, d = w.shape
    n = resid.shape[0]
    has_bias = b is not None
    if y.ndim == 3:
        y_spec = pl.BlockSpec((k // LANES, tm, LANES), lambda i: (0, i, 0))
    else:
        y_spec = pl.BlockSpec((tm, k), lambda i: (i, 0))
    in_specs = [y_spec, pl.BlockSpec((k, d), lambda i: (0, 0))]
    args = [y, w]
    if has_bias:
        in_specs.append(pl.BlockSpec((1, d), lambda i: (0, 0)))
        args.append(b.reshape(1, d))
    in_specs.append(pl.BlockSpec((tm, d), lambda i: (i, 0)))
    args.append(resid)
    return pl.pallas_call(
        functools.partial(_op_kernel, has_bias),
        out_shape=jax.ShapeDtypeStruct((n, d), F32),
        grid=(n // tm,),
        in_specs=in_specs,
        out_specs=pl.BlockSpec((tm, d), lambda i: (i, 0)),
        compiler_params=_cp("parallel"),
        name=name,
    )(*args)


def _rel_bucket_np(dist):
    max_exact = REL_BUCKETS // 2
    d = np.maximum(dist, 0)
    df = np.maximum(d, 1).astype(np.float32)
    large = max_exact + (np.log(df / np.float32(max_exact)) / np.float32(math.log(REL_MAX_DIST / max_exact))
                         * np.float32(REL_BUCKETS - max_exact)).astype(np.int32)
    large = np.minimum(large, REL_BUCKETS - 1)
    return np.where(d < max_exact, d, large).astype(np.int32)


def _bias_kernel(scale, tab_ref, idx_ref, o_ref):
    idx = idx_ref[...]
    for h in range(o_ref.shape[1]):
        acc = jnp.zeros(idx.shape, F32)
        for b in range(REL_BUCKETS):
            acc = jnp.where(idx == b, tab_ref[b, h] * scale, acc)
        o_ref[0, h] = jnp.where(idx < 0, NEG, acc).astype(o_ref.dtype)


def _bias_tiles(rel_table, bucket_idx, out_dtype, *, tile, tr, scale=1.0):
    r, c = bucket_idx.shape
    nh = rel_table.shape[1]
    per = tile // tr
    return pl.pallas_call(
        functools.partial(_bias_kernel, scale),
        out_shape=jax.ShapeDtypeStruct((r // tile, nh, tile, c), out_dtype),
        grid=(r // tr,),
        in_specs=[pl.BlockSpec(memory_space=pltpu.SMEM),
                  pl.BlockSpec((tr, c), lambda i: (i, 0))],
        out_specs=pl.BlockSpec((1, nh, tr, c), lambda i: (i // per, 0, i % per, 0)),
        compiler_params=_cp("parallel"),
        name="bias_tiles",
    )(rel_table, jnp.asarray(bucket_idx))


def _asel_kernel(qi_ref, wi_ref, ki_ref, m_ref, keys_ref, wb_ref, x_ref, si_ref, sf_ref, *, tq, cw, s_len, topk):
    qb = pl.program_id(1)
    spc = cw // LANES
    n_chunks = (qb * tq + tq + cw - 1) // cw
    n_slabs = n_chunks * spc
    nslab_all = s_len // LANES

    wi = wi_ref[0] * (IDX_HEADS ** -0.5 * IDX_DIM ** -0.5)
    for h in range(IDX_HEADS):
        wb_ref[h] = jnp.broadcast_to(wi[:, h:h + 1], (tq, LANES))
    q_pos = qb * tq + lax.broadcasted_iota(I32, (tq, cw), 0)
    lane_pos = lax.broadcasted_iota(I32, (tq, cw), 1)

    def to_key(x):
        bits = pltpu.bitcast(x, I32)
        return bits ^ ((bits >> 31) & 0x7FFFFFFF)

    def build(c, mx):
        k0 = pl.multiple_of(c * cw, cw)
        kc = ki_ref[0, pl.ds(k0, cw), :]
        sc = jnp.zeros((tq, cw), F32)
        for h in range(IDX_HEADS):
            r = lax.dot_general(qi_ref[0, :, h * IDX_DIM:(h + 1) * IDX_DIM], kc, _NT,
                                preferred_element_type=F32)
            sc = sc + jnp.maximum(r, 0.0) * jnp.concatenate([wb_ref[h]] * spc, axis=1)
        sc = sc + 0.0
        causal = k0 + lane_pos <= q_pos
        key = jnp.where(causal, to_key(sc), INT_MIN)
        scm = jnp.where(causal, sc, -jnp.inf)
        for j in range(spc):
            keys_ref[c * spc + j] = key[:, j * LANES:(j + 1) * LANES]
            mx = jnp.maximum(mx, scm[:, j * LANES:(j + 1) * LANES])
        return mx

    mx = lax.fori_loop(0, n_chunks, build, jnp.full((tq, LANES), -jnp.inf, F32))
    key_max = to_key(jnp.broadcast_to(jnp.max(mx, axis=1, keepdims=True), (tq, LANES)))

    def lane_sum(acc):
        return jnp.broadcast_to(jnp.sum(acc.astype(F32), axis=1, keepdims=True), (tq, LANES))

    def slab_loop(body, init):
        def chunk(c, carry):
            for j in range(spc):
                carry = body(c * spc + j, carry)
            return carry
        return lax.fori_loop(0, n_chunks, chunk, init)

    zero = jnp.zeros((tq, LANES), I32)

    def count_ge(cand):
        return lane_sum(slab_loop(lambda c, acc: acc + jnp.where(keys_ref[c] >= cand, 1, 0), zero))

    n_real = (q_pos[:, :LANES] + 1).astype(F32)
    lo0 = jnp.full((tq, LANES), INT_MIN + 1, I32)
    si_ref[0] = lo0
    si_ref[1] = jnp.where(n_real > topk, key_max + 1, lo0 + 1)
    sf_ref[0] = n_real
    sf_ref[1] = jnp.zeros((tq, LANES), F32)

    def midpoint(lo, hi):
        return lo + lax.shift_right_logical(hi - lo, 1)

    def search_pass(cand):
        lo, hi, clo, chi = si_ref[0], si_ref[1], sf_ref[0], sf_ref[1]
        cnt = count_ge(cand)
        ge = cnt >= topk
        lo, clo = jnp.where(ge, cand, lo), jnp.where(ge, cnt, clo)
        hi = jnp.where(ge, hi, jnp.maximum(cand, lo + 1))
        hi = jnp.where(clo == topk, lo + 1, hi)
        si_ref[0], si_ref[1] = lo, hi
        sf_ref[0], sf_ref[1] = clo, jnp.where(ge, chi, cnt)
        open_rows = jnp.where((hi - lo) != 1, 1.0, 0.0)
        return jnp.max(open_rows, axis=0, keepdims=True)[0, 0]

    lo, hi = si_ref[0], si_ref[1]
    search_pass(jnp.where(((hi - lo) != 1) & (hi > 1), 1, midpoint(lo, hi)))
    lo, hi = si_ref[0], si_ref[1]
    mid = midpoint(lo, hi)
    probe = hi - (1 << 25)
    cand = jnp.where((lo == 1) & (probe > mid), probe, mid)
    pending = search_pass(jnp.where(((hi - lo) != 1) & (hi == 1), 0, cand))

    def search_body(carry):
        it, _ = carry
        return it + 1, search_pass(midpoint(si_ref[0], si_ref[1]))

    lax.while_loop(lambda carry: (carry[0] < 40) & (carry[1] > 0.0), search_body, (jnp.int32(0), pending))
    tau = si_ref[0]
    clo, chi = sf_ref[0], sf_ref[1]

    r_need = topk - chi
    tied = clo > topk
    x_ref[...] = jnp.full((tq, LANES), s_len, I32)
    lane = lax.broadcasted_iota(I32, (tq, LANES), 1)

    @pl.when(jnp.max(jnp.where(tied, 1.0, 0.0)) > 0.0)
    def _():
        nbits = int(math.log2(s_len)) + 1

        def tie_pass(i, x):
            cand = x + lax.shift_left(jnp.int32(1), nbits - 1 - i)

            def body(c, acc):
                idx = c * LANES + lane
                return acc + jnp.where((keys_ref[c] == tau) & (idx < cand), 1, 0)

            cnt = lane_sum(slab_loop(body, zero))
            return jnp.where(cnt < r_need, cand, x)

        x = lax.fori_loop(0, nbits, tie_pass, zero)
        x_ref[...] = jnp.where(tied, x, s_len)

    x_cut = x_ref[...]

    def emit(c, carry):
        k = keys_ref[c]
        idx = c * LANES + lane
        sel = (k > tau) | ((k == tau) & (idx <= x_cut))
        m_ref[0, c] = jnp.where(sel, 0.0, NEG).astype(m_ref.dtype)
        return carry

    slab_loop(emit, 0)

    def fill(c, carry):
        m_ref[0, c] = jnp.full((tq, LANES), NEG, m_ref.dtype)
        return carry

    lax.fori_loop(n_slabs, nslab_all, fill, 0)


def _a_select(qi, wi, ki, *, tq=128, cw=1024):
    bsz, s_len, _ = qi.shape
    topk = min(TOPK_MAX, s_len // 4)
    nslab = s_len // LANES
    return pl.pallas_call(
        functools.partial(_asel_kernel, tq=tq, cw=cw, s_len=s_len, topk=float(topk)),
        out_shape=jax.ShapeDtypeStruct((bsz, nslab, s_len, LANES), BF16),
        grid=(bsz, s_len // tq),
        in_specs=[pl.BlockSpec((1, tq, IDX_HEADS * IDX_DIM), lambda b, i: (b, i, 0)),
                  pl.BlockSpec((1, tq, IDX_HEADS), lambda b, i: (b, i, 0)),
                  pl.BlockSpec((1, s_len, IDX_DIM), lambda b, i: (b, 0, 0))],
        out_specs=pl.BlockSpec((1, nslab, tq, LANES), lambda b, i: (b, 0, i, 0)),
        scratch_shapes=[pltpu.VMEM((nslab, tq, LANES), I32),
                        pltpu.VMEM((IDX_HEADS, tq, LANES), F32),
                        pltpu.VMEM((tq, LANES), I32),
                        pltpu.VMEM((2, tq, LANES), I32),
                        pltpu.VMEM((2, tq, LANES), F32)],
        compiler_params=_cp("parallel", "arbitrary"),
        name="a_select",
    )(qi, wi, ki)


def _attn_kernel(qt_ref, kt_ref, tab_ref, q_ref, k_ref, v_ref, msk_ref, bias_ref, o_ref,
                 m_sc, l_sc, acc_sc, mk_sc, *slots, t, nd_far, hpg):
    step = pl.program_id(1)
    qi = qt_ref[step]
    ki = kt_ref[step]

    @pl.when(ki == 0)
    def _():
        m_sc[...] = jnp.full(m_sc.shape, NEG, F32)
        l_sc[...] = jnp.zeros_like(l_sc)
        acc_sc[...] = jnp.zeros_like(acc_sc)

    mk_sc[...] = jnp.concatenate([msk_ref[0, j] for j in range(t // LANES)], axis=1).astype(F32)
    ns = 2 * hpg
    s_sc, p_sc, al_sc = slots[:ns], slots[ns:2 * ns], slots[2 * ns:]

    def slot(h):
        return (h // hpg % 2) * hpg + h % hpg

    def scores(h, far):
        s = lax.dot_general(q_ref[h, 0], k_ref[h, 0], _NT, preferred_element_type=F32) + mk_sc[...]
        if not far:
            s = s + bias_ref[0, h].astype(F32)
        s_sc[slot(h)][...] = s

    def softmax(h, far):
        s = s_sc[slot(h)][...]
        c = tab_ref[REL_BUCKETS - 1, h] * LOG2E if far else 0.0
        m_prev = m_sc[h]
        m_cur = jnp.broadcast_to(jnp.max(s, axis=-1, keepdims=True), (t, LANES))
        m_new = jnp.maximum(m_prev, m_cur + c)
        alpha = jnp.exp2(m_prev - m_new)
        p = jnp.exp2(s - jnp.concatenate([m_new - c] * (t // LANES), axis=1))
        p_part = p[:, :LANES]
        for j in range(1, t // LANES):
            p_part = p_part + p[:, j * LANES:(j + 1) * LANES]
        l_sc[h] = alpha * l_sc[h] + p_part
        m_sc[h] = m_new
        al_sc[slot(h)][...] = alpha
        p_sc[slot(h)][...] = p.astype(BF16)

    def values(h):
        acc_sc[h] = al_sc[slot(h)][...] * acc_sc[h] + jnp.dot(p_sc[slot(h)][...], v_ref[h, 0],
                                                            preferred_element_type=F32)

    one = qt_ref[0] + 1

    def heads(far):
        n_grp = A_HEADS // hpg
        for st in range(n_grp + 2):
            def stage(i, carry, st=st):
                for j in range(hpg):
                    if st < n_grp:
                        scores(st * hpg + j, far)
                    if 1 <= st <= n_grp:
                        softmax((st - 1) * hpg + j, far)
                    if st >= 2:
                        values((st - 2) * hpg + j)
                return carry

            lax.fori_loop(0, one, stage, 0)

    @pl.when(qi - ki >= nd_far)
    def _():
        heads(True)

    @pl.when(qi - ki < nd_far)
    def _():
        heads(False)

    @pl.when(ki == qi)
    def _():
        for h in range(A_HEADS):
            l = jnp.sum(l_sc[h], axis=-1, keepdims=True)
            o_ref[h, 0] = (acc_sc[h] * (1.0 / l)).astype(o_ref.dtype)


def _a_attention(qkv, mask4, bias, rel_table, *, t, nd_far, hpg=4):
    _, bsz, s_len, e = qkv.shape
    nq = s_len // t
    nd = bias.shape[0]
    qt = np.concatenate([np.full(i + 1, i, np.int32) for i in range(nq)])
    kt = np.concatenate([np.arange(i + 1, dtype=np.int32) for i in range(nq)])
    spt = t // LANES
    grid_spec = pltpu.PrefetchScalarGridSpec(
        num_scalar_prefetch=2,
        grid=(bsz, len(qt)),
        in_specs=[
            pl.BlockSpec(memory_space=pltpu.SMEM),
            pl.BlockSpec((A_HEADS, 1, t, e), lambda b, s, qt, kt: (0, b, qt[s], 0)),
            pl.BlockSpec((A_HEADS, 1, t, e), lambda b, s, qt, kt: (1, b, kt[s], 0)),
            pl.BlockSpec((A_HEADS, 1, t, e), lambda b, s, qt, kt: (2, b, kt[s], 0)),
            pl.BlockSpec((1, spt, t, LANES), lambda b, s, qt, kt: (b, kt[s], qt[s], 0)),
            pl.BlockSpec((1, A_HEADS, t, t),
                         lambda b, s, qt, kt: (jnp.minimum(qt[s] - kt[s], nd - 1), 0, 0, 0)),
        ],
        out_specs=pl.BlockSpec((A_HEADS, 1, t, e), lambda b, s, qt, kt: (0, b, qt[s], 0)),
        scratch_shapes=[pltpu.VMEM((A_HEADS, t, LANES), F32),
                        pltpu.VMEM((A_HEADS, t, LANES), F32),
                        pltpu.VMEM((A_HEADS, t, e), F32),
                        pltpu.VMEM((t, t), F32)]
        + [pltpu.VMEM((t, t), F32)] * (2 * hpg)
        + [pltpu.VMEM((t, t), BF16)] * (2 * hpg)
        + [pltpu.VMEM((t, LANES), F32)] * (2 * hpg),
    )
    return pl.pallas_call(
        functools.partial(_attn_kernel, t=t, nd_far=nd_far, hpg=hpg),
        out_shape=jax.ShapeDtypeStruct((A_HEADS, bsz, s_len, e), BF16),
        grid_spec=grid_spec,
        compiler_params=_cp("parallel", "arbitrary"),
        name="a_attention",
    )(jnp.asarray(qt), jnp.asarray(kt), rel_table, qkv, qkv, qkv, mask4, bias)


def _mixer_a(h, g, w_in, w_out, rel_table, bsz, s_len):
    n = h.shape[0]
    d = D_MODEL
    n_idx = IDX_HEADS * IDX_DIM + IDX_DIM + IDX_HEADS
    w_qkv = jnp.concatenate([w_in[:, :d] * (A_HEAD_DIM ** -0.5 * LOG2E), w_in[:, d:3 * d]], axis=1).astype(BF16)
    w_idx = jnp.pad(w_in[:, 3 * d:], ((0, 0), (0, 640 - n_idx))).astype(BF16)
    qkv = _norm_matmul(h, g, w_qkv, out_dtype=BF16, tn=1024, head_major=True, name="a_qkv")
    idx = _norm_matmul(h, g, w_idx, out_dtype=F32, tn=640, name="a_idx")
    qkv = qkv.reshape(3 * A_HEADS, bsz, s_len, A_HEAD_DIM)
    idx = idx.reshape(bsz, s_len, 640)
    nqi = IDX_HEADS * IDX_DIM
    qi = idx[..., :nqi].astype(BF16)
    ki = idx[..., nqi:nqi + IDX_DIM].astype(BF16)
    wi = idx[..., nqi + IDX_DIM:n_idx]
    mask4 = _a_select(qi, wi, ki)

    t = min(512, s_len)
    nd_far = 0
    while _rel_bucket_np(np.array(nd_far * t - (t - 1))) < REL_BUCKETS - 1:
        nd_far += 1
    nd = min(nd_far, s_len // t)
    ii = np.arange(t)[:, None] - np.arange(t)[None, :]
    dist = (np.arange(nd)[:, None, None] * t + ii[None]).reshape(nd * t, t)
    bias = _bias_tiles(rel_table, _rel_bucket_np(dist), BF16, tile=t, tr=t // 2, scale=LOG2E)
    o = _a_attention(qkv, mask4, bias, rel_table, t=t, nd_far=nd_far)
    return _out_proj(o.reshape(A_HEADS, n, A_HEAD_DIM), w_out.astype(BF16), h, name="a_out")


def _bsp_kernel(u_ref, v_ref, lg_ref, lb_ref, wsp_ref, bsp_ref, o_ref, wt_ref, *, rows):
    @pl.when(pl.program_id(0) == 0)
    def _():
        r = lax.broadcasted_iota(I32, (B_CHUNK, B_CHUNK), 0)
        c = lax.broadcasted_iota(I32, (B_CHUNK, B_CHUNK), 1)
        for g in range(B_GROUPS):
            wt_ref[g] = jnp.where(r >= c, wsp_ref[g], 0.0).astype(BF16)

    vn = _layernorm(v_ref[...].astype(F32), lg_ref[...], lb_ref[...]).astype(BF16)
    gw = B_HALF // B_GROUPS
    for c in range(rows // B_CHUNK):
        rs = slice(c * B_CHUNK, (c + 1) * B_CHUNK)
        for g in range(B_GROUPS):
            cs = slice(g * gw, (g + 1) * gw)
            sv = jnp.dot(wt_ref[g], vn[rs, cs], preferred_element_type=F32) + bsp_ref[g]
            o_ref[rs, cs] = (u_ref[rs, cs].astype(F32) * sv).astype(o_ref.dtype)


def _b_spatial(z, ln_g, ln_b, w_sp, b_sp, *, rows=512):
    n = z.shape[0]
    return pl.pallas_call(
        functools.partial(_bsp_kernel, rows=rows),
        out_shape=jax.ShapeDtypeStruct((n, B_HALF), BF16),
        grid=(n // rows,),
        in_specs=[pl.BlockSpec((rows, B_HALF), lambda i: (i, 0)),
                  pl.BlockSpec((rows, B_HALF), lambda i: (i, 1)),
                  pl.BlockSpec((1, B_HALF), lambda i: (0, 0)),
                  pl.BlockSpec((1, B_HALF), lambda i: (0, 0)),
                  pl.BlockSpec((B_GROUPS, B_CHUNK, B_CHUNK), lambda i: (0, 0, 0)),
                  pl.BlockSpec((B_GROUPS, B_CHUNK, 1), lambda i: (0, 0, 0))],
        out_specs=pl.BlockSpec((rows, B_HALF), lambda i: (i, 0)),
        scratch_shapes=[pltpu.VMEM((B_GROUPS, B_CHUNK, B_CHUNK), BF16)],
        compiler_params=_cp("arbitrary"),
        name="b_spatial",
    )(z, z, ln_g.reshape(1, B_HALF), ln_b.reshape(1, B_HALF), w_sp, b_sp.reshape(B_GROUPS, B_CHUNK, 1))


def _mixer_b(h, g, w_in, b_in, ln_g, ln_b, w_sp, b_sp, w_out):
    z = _norm_matmul(h, g, w_in.astype(BF16), b_in, mode="gelu", out_dtype=BF16, tn=1024, name="b_in")
    y = _b_spatial(z, ln_g, ln_b, w_sp, b_sp)
    return _out_proj(y, w_out.astype(BF16), h, name="b_out")


C_HALO = 32


def _conv_kernel(cur_ref, halo_ref, wdw_ref, bdw_ref, lg_ref, lb_ref, w2_ref, b2_ref, r_ref, o_ref,
                 ext_ref, yc_ref, *, ts, rt, ct):
    i = pl.program_id(1)
    ext_ref[C_HALO:, :] = cur_ref[0]

    @pl.when(i == 0)
    def _():
        ext_ref[:C_HALO, :] = jnp.zeros((C_HALO, D_MODEL), F32)

    @pl.when(i > 0)
    def _():
        ext_ref[:C_HALO, :] = halo_ref[0]

    off = C_HALO - (C_KERNEL - 1)

    def tile(it, carry):
        r0 = pl.multiple_of((it // (D_MODEL // ct)) * rt, rt)
        c0 = pl.multiple_of((it % (D_MODEL // ct)) * ct, ct)
        nwin = rt + C_HALO
        win = ext_ref[pl.ds(r0, nwin), pl.ds(c0, ct)]
        acc = jnp.zeros((rt, ct), F32) + bdw_ref[:, pl.ds(c0, ct)]
        for b in range(8):
            rolled = pltpu.roll(win, nwin - (off + b), axis=0)
            for a in range((C_KERNEL - b + 7) // 8):
                j = 8 * a + b
                acc = acc + rolled[8 * a:8 * a + rt] * wdw_ref[j:j + 1, pl.ds(c0, ct)]
        yc_ref[pl.ds(r0, rt), pl.ds(c0, ct)] = acc
        return carry

    lax.fori_loop(0, (ts // rt) * (D_MODEL // ct), tile, 0)
    y = _layernorm(yc_ref[...], lg_ref[...], lb_ref[...])
    y = (y * jax.nn.sigmoid(y)).astype(BF16)
    o_ref[0] = r_ref[0] + jnp.dot(y, w2_ref[...], preferred_element_type=F32) + b2_ref[...]


def _c_conv(y1, w_dw, b_dw, ln_g, ln_b, w2, b2, resid, *, ts=512, rt=32, ct=512):
    bsz, s_len, d = y1.shape
    hb = ts // C_HALO
    vec = lambda a: a.reshape(1, d)
    return pl.pallas_call(
        functools.partial(_conv_kernel, ts=ts, rt=rt, ct=ct),
        out_shape=jax.ShapeDtypeStruct((bsz, s_len, d), F32),
        grid=(bsz, s_len // ts),
        in_specs=[pl.BlockSpec((1, ts, d), lambda b, i: (b, i, 0)),
                  pl.BlockSpec((1, C_HALO, d), lambda b, i: (b, jnp.maximum(i * hb - 1, 0), 0)),
                  pl.BlockSpec((C_KERNEL, d), lambda b, i: (0, 0)),
                  pl.BlockSpec((1, d), lambda b, i: (0, 0)),
                  pl.BlockSpec((1, d), lambda b, i: (0, 0)),
                  pl.BlockSpec((1, d), lambda b, i: (0, 0)),
                  pl.BlockSpec((d, d), lambda b, i: (0, 0)),
                  pl.BlockSpec((1, d), lambda b, i: (0, 0)),
                  pl.BlockSpec((1, ts, d), lambda b, i: (b, i, 0))],
        out_specs=pl.BlockSpec((1, ts, d), lambda b, i: (b, i, 0)),
        scratch_shapes=[pltpu.VMEM((ts + C_HALO, d), F32), pltpu.VMEM((ts, d), F32)],
        compiler_params=_cp("parallel", "parallel"),
        name="c_conv",
    )(y1, y1, w_dw, vec(b_dw), vec(ln_g), vec(ln_b), w2, vec(b2), resid)


def _mixer_c(h, g, w_pw1, b_pw1, w_dw, b_dw, ln_g, ln_b, w_pw2, b_pw2, bsz, s_len):
    n, d = h.shape
    y1 = _norm_matmul(h, g, w_pw1.astype(BF16), b_pw1, mode="glu", out_dtype=F32, tn=512, name="c_pw1")
    out = _c_conv(y1.reshape(bsz, s_len, d), w_dw, b_dw, ln_g, ln_b, w_pw2.astype(BF16), b_pw2,
                  h.reshape(bsz, s_len, d))
    return out.reshape(n, d)


def _dattn_kernel(q_ref, kc_ref, kp_ref, vc_ref, vp_ref, bias_ref, o_ref, lse_ref):
    nb = pl.program_id(2)
    first = jnp.where(nb == 0, NEG, 0.0)
    e = D_HEAD_DIM
    for h in range(D_HEADS):
        hs = slice(h * e, (h + 1) * e)
        q = q_ref[0, :, hs]
        kk = jnp.concatenate([kp_ref[0, :, hs], kc_ref[0, :, hs]], axis=0)
        vv = jnp.concatenate([vp_ref[0, :, hs], vc_ref[0, :, hs]], axis=0)
        s = lax.dot_general(q, kk, _NT, preferred_element_type=F32) * (e ** -0.5) + bias_ref[h]
        col = lax.broadcasted_iota(I32, s.shape, 1)
        s = s + jnp.where(col < D_BLOCK, first, 0.0)
        m = jnp.max(s, axis=-1, keepdims=True)
        p = jnp.exp(s - m)
        l = jnp.sum(p, axis=-1, keepdims=True)
        o = jnp.dot(p.astype(BF16), vv, preferred_element_type=F32) / l
        o_ref[0, :, hs] = o
        lse_ref[0, :, hs] = jnp.broadcast_to(m + jnp.log(l), (D_BLOCK, e))


def _d_group(proj, bias_g, gidx, dil):
    bsz, s_len, c = proj.shape
    n_sub = s_len // dil
    nb = n_sub // D_BLOCK
    w = D_HEADS * D_HEAD_DIM
    ng = len(D_PAIRS)
    cb = c // w
    view = proj.reshape(bsz, n_sub, dil * c)
    qcol = lambda r: r * cb + gidx
    kcol = lambda r: r * cb + ng + gidx
    vcol = lambda r: r * cb + 2 * ng + gidx
    blk = (1, D_BLOCK, w)
    prev = lambda n: jnp.maximum(n - 1, 0)
    out_sds = jax.ShapeDtypeStruct((bsz, n_sub, dil * w), F32)
    o, lse = pl.pallas_call(
        _dattn_kernel,
        out_shape=(out_sds, out_sds),
        grid=(bsz, dil, nb),
        in_specs=[pl.BlockSpec(blk, lambda b, r, n: (b, n, qcol(r))),
                  pl.BlockSpec(blk, lambda b, r, n: (b, n, kcol(r))),
                  pl.BlockSpec(blk, lambda b, r, n: (b, prev(n), kcol(r))),
                  pl.BlockSpec(blk, lambda b, r, n: (b, n, vcol(r))),
                  pl.BlockSpec(blk, lambda b, r, n: (b, prev(n), vcol(r))),
                  pl.BlockSpec((D_HEADS, D_BLOCK, 2 * D_BLOCK), lambda b, r, n: (0, 0, 0))],
        out_specs=(pl.BlockSpec(blk, lambda b, r, n: (b, n, r)),
                   pl.BlockSpec(blk, lambda b, r, n: (b, n, r))),
        compiler_params=_cp("parallel", "parallel", "arbitrary"),
        name=f"d_attn_g{gidx}",
    )(view, view, view, view, view, bias_g)
    return o.reshape(bsz * s_len, w), lse.reshape(bsz * s_len, w)


def _dmerge_kernel(o0, o1, o2, l0, l1, l2, w_ref, r_ref, out_ref):
    a0, a1, a2 = l0[...], l1[...], l2[...]
    mx = jnp.maximum(jnp.maximum(a0, a1), a2)
    e0, e1, e2 = jnp.exp(a0 - mx), jnp.exp(a1 - mx), jnp.exp(a2 - mx)
    y = (e0 * o0[...] + e1 * o1[...] + e2 * o2[...]) / (e0 + e1 + e2)
    out_ref[...] = r_ref[...] + jnp.dot(y.astype(BF16), w_ref[...], preferred_element_type=F32)


def _d_merge(outs, lses, w_out, resid, *, tm=512):
    n, w = outs[0].shape
    d = resid.shape[1]
    row = pl.BlockSpec((tm, w), lambda i: (i, 0))
    return pl.pallas_call(
        _dmerge_kernel,
        out_shape=jax.ShapeDtypeStruct((n, d), F32),
        grid=(n // tm,),
        in_specs=[row] * 6 + [pl.BlockSpec((w, d), lambda i: (0, 0)),
                              pl.BlockSpec((tm, d), lambda i: (i, 0))],
        out_specs=pl.BlockSpec((tm, d), lambda i: (i, 0)),
        compiler_params=_cp("parallel"),
        name="d_merge",
    )(*outs, *lses, w_out, resid)


def _mixer_d(h, g, w_in, w_out, rel_table, bsz, s_len):
    proj = _norm_matmul(h, g, w_in.astype(BF16), out_dtype=BF16, tn=1536, name="d_in")
    proj = proj.reshape(bsz, s_len, proj.shape[1])
    p_loc = np.arange(D_BLOCK)[:, None]
    j_loc = np.arange(2 * D_BLOCK)[None, :]
    m = p_loc + D_BLOCK - j_loc
    idx = []
    for window, dil in D_PAIRS:
        steps = window // dil
        idx.append(np.where((m >= 0) & (m <= steps), _rel_bucket_np(m * dil), -1))
    bias = _bias_tiles(rel_table, np.concatenate(idx, 0).astype(np.int32), F32, tile=D_BLOCK, tr=D_BLOCK)
    outs, lses = [], []
    for gi, (_, dil) in enumerate(D_PAIRS):
        o, lse = _d_group(proj, bias[gi], gi, dil)
        outs.append(o)
        lses.append(lse)
    return _d_merge(outs, lses, w_out.astype(BF16), h)


def _fnorm_kernel(x_ref, g_ref, o_ref):
    x = x_ref[...]
    ms = jnp.mean(x * x, axis=-1, keepdims=True)
    o_ref[...] = x * lax.rsqrt(ms + EPS) * g_ref[...]


def _final_norm(h, g, *, tm=1024):
    n, d = h.shape
    return pl.pallas_call(
        _fnorm_kernel,
        out_shape=jax.ShapeDtypeStruct((n, d), F32),
        grid=(n // tm,),
        in_specs=[pl.BlockSpec((tm, d), lambda i: (i, 0)), pl.BlockSpec((1, d), lambda i: (0, 0))],
        out_specs=pl.BlockSpec((tm, d), lambda i: (i, 0)),
        compiler_params=_cp("parallel"),
        name="final_norm",
    )(h, g.reshape(1, d))


def kernel(x, norm_g, final_g, ffn_w_in, ffn_w_out, rel_table, a_w_in, a_w_out, b_w_in, b_b_in, b_ln_g, b_ln_b, b_w_sp, b_b_sp, b_w_out, c_w_pw1, c_b_pw1, c_w_dw, c_b_dw, c_ln_g, c_ln_b, c_w_pw2, c_b_pw2, d_w_in, d_w_out):
    bsz, s_len, d = x.shape
    depth = norm_g.shape[0]
    h = x.reshape(bsz * s_len, d)
    for i in range(depth):
        kind, j = i % 4, i // 4
        h = _ffn(h, norm_g[i, 0], ffn_w_in[i, 0].astype(BF16), ffn_w_out[i, 0].astype(BF16))
        g = norm_g[i, 1]
        if kind == 0:
            h = _mixer_a(h, g, a_w_in[j], a_w_out[j], rel_table, bsz, s_len)
        elif kind == 1:
            h = _mixer_b(h, g, b_w_in[j], b_b_in[j], b_ln_g[j], b_ln_b[j], b_w_sp[j], b_b_sp[j], b_w_out[j])
        elif kind == 2:
            h = _mixer_c(h, g, c_w_pw1[j], c_b_pw1[j], c_w_dw[j], c_b_dw[j], c_ln_g[j], c_ln_b[j],
                         c_w_pw2[j], c_b_pw2[j], bsz, s_len)
        else:
            h = _mixer_d(h, g, d_w_in[j], d_w_out[j], rel_table, bsz, s_len)
        h = _ffn(h, norm_g[i, 2], ffn_w_in[i, 1].astype(BF16), ffn_w_out[i, 1].astype(BF16))
    return _final_norm(h, final_g).reshape(bsz, s_len, d)
```

```python
import functools
import math

import numpy as np
import jax
import jax.numpy as jnp
from jax import lax
from jax.experimental import pallas as pl
from jax.experimental.pallas import tpu as pltpu

F32, BF16, I32 = jnp.float32, jnp.bfloat16, jnp.int32

D_MODEL = 1024
EPS = 1e-6
D_FF = 2816
REL_BUCKETS = 32
REL_MAX_DIST = 2048
A_HEADS = 8
A_HEAD_DIM = 128
IDX_HEADS = 8
IDX_DIM = 64
TOPK_MAX = 256
B_CHUNK = 128
B_HALF = 3 * D_MODEL
B_GROUPS = 8
C_KERNEL = 31
D_PAIRS = ((128, 1), (512, 4), (2048, 16))
D_HEADS = 8
D_HEAD_DIM = 64
D_BLOCK = 128

LANES = 128
VMEM_LIMIT = 56 * 1024 * 1024
NEG = -1e30
INT_MIN = -2 ** 31
LOG2E = math.log2(math.e)

_NT = (((1,), (1,)), ((), ()))


def _cp(*sem, flags=None):
    return pltpu.CompilerParams(dimension_semantics=sem, vmem_limit_bytes=VMEM_LIMIT, flags=flags)


def _rms_bf16(x, g):
    ms = jnp.mean(x * x, axis=-1, keepdims=True)
    return (x * lax.rsqrt(ms + EPS) * g).astype(BF16)


def _layernorm(x, g, b):
    mu = jnp.mean(x, axis=-1, keepdims=True)
    xc = x - mu
    var = jnp.mean(xc * xc, axis=-1, keepdims=True)
    return xc * lax.rsqrt(var + EPS) * g + b


def _ffn_kernel(x_ref, xnext_ref, g_ref, wi_ref, wo_ref, o_ref, xn_ref, *, tf):
    i = pl.program_id(0)
    slot = i % 2

    @pl.when(i == 0)
    def _():
        xn_ref[0] = _rms_bf16(x_ref[...], g_ref[...])

    xn = xn_ref[slot]
    acc = None
    for c in range(D_FF // tf):
        gate = jnp.dot(xn, wi_ref[:, c * tf:(c + 1) * tf], preferred_element_type=F32)
        up = jnp.dot(xn, wi_ref[:, D_FF + c * tf:D_FF + (c + 1) * tf], preferred_element_type=F32)
        a = (gate * jax.nn.sigmoid(gate) * up).astype(BF16)
        part = jnp.dot(a, wo_ref[c * tf:(c + 1) * tf, :], preferred_element_type=F32)
        acc = part if acc is None else acc + part
    o_ref[...] = x_ref[...] + 0.5 * acc
    xn_ref[1 - slot] = _rms_bf16(xnext_ref[...], g_ref[...])


def _ffn(h, g, w_in, w_out, *, tm=512, tf=1408):
    n, d = h.shape
    last = n // tm - 1
    resident = dict(pipeline_mode=pl.Buffered(1))
    return pl.pallas_call(
        functools.partial(_ffn_kernel, tf=tf),
        out_shape=jax.ShapeDtypeStruct((n, d), F32),
        grid=(n // tm,),
        in_specs=[
            pl.BlockSpec((tm, d), lambda i: (i, 0)),
            pl.BlockSpec((tm, d), lambda i: (jnp.minimum(i + 1, last), 0)),
            pl.BlockSpec((1, d), lambda i: (0, 0)),
            pl.BlockSpec((d, 2 * D_FF), lambda i: (0, 0), **resident),
            pl.BlockSpec((D_FF, d), lambda i: (0, 0), **resident),
        ],
        out_specs=pl.BlockSpec((tm, d), lambda i: (i, 0)),
        scratch_shapes=[pltpu.VMEM((2, tm, d), BF16)],
        compiler_params=_cp("arbitrary"),
        name="ffn",
    )(h, h, g.reshape(1, d), w_in, w_out)


def _nm_kernel(mode, has_bias, tn, x_ref, g_ref, w_ref, *refs):
    b_ref = refs[0] if has_bias else None
    o_ref, xn_ref = refs[-2], refs[-1]
    s = pl.program_id(0)
    head_major = len(o_ref.shape) == 3
    m_out = o_ref.shape[0] * LANES if head_major else o_ref.shape[1]

    @pl.when(s == 0)
    def _():
        xn_ref[0] = _rms_bf16(x_ref[...], g_ref[...])

    @pl.when(s > 0)
    def _():
        xn = xn_ref[(s - 1) % 2]

        def column_block(c0):
            y = jnp.dot(xn, w_ref[:, c0:c0 + tn], preferred_element_type=F32)
            return y + b_ref[:, c0:c0 + tn] if has_bias else y

        for j in range(m_out // tn):
            y = column_block(j * tn)
            if mode == "gelu":
                y = jax.nn.gelu(y)
            elif mode == "glu":
                y = y * jax.nn.sigmoid(column_block(m_out + j * tn))
            if head_major:
                for hh in range(tn // LANES):
                    o_ref[j * (tn // LANES) + hh] = y[:, hh * LANES:(hh + 1) * LANES].astype(o_ref.dtype)
            else:
                o_ref[:, j * tn:(j + 1) * tn] = y.astype(o_ref.dtype)
        xn_ref[s % 2] = _rms_bf16(x_ref[...], g_ref[...])


def _norm_matmul(h, g, w, b=None, *, mode="plain", out_dtype=BF16, tm=512, tn=512, head_major=False, name="proj"):
    n, d = h.shape
    m_total = w.shape[1]
    m_out = m_total // 2 if mode == "glu" else m_total
    nt = n // tm
    has_bias = b is not None
    prev = lambda s: jnp.maximum(s - 1, 0)
    if head_major:
        out_shape = jax.ShapeDtypeStruct((m_out // LANES, n, LANES), out_dtype)
        out_spec = pl.BlockSpec((m_out // LANES, tm, LANES), lambda s: (0, prev(s), 0))
    else:
        out_shape = jax.ShapeDtypeStruct((n, m_out), out_dtype)
        out_spec = pl.BlockSpec((tm, m_out), lambda s: (prev(s), 0))
    resident = dict(pipeline_mode=pl.Buffered(1))
    in_specs = [pl.BlockSpec((tm, d), lambda s: (jnp.minimum(s, nt - 1), 0)),
                pl.BlockSpec((1, d), lambda s: (0, 0)),
                pl.BlockSpec((d, m_total), lambda s: (0, 0), **resident)]
    args = [h, g.reshape(1, d), w]
    if has_bias:
        in_specs.append(pl.BlockSpec((1, m_total), lambda s: (0, 0)))
        args.append(b.reshape(1, m_total))
    return pl.pallas_call(
        functools.partial(_nm_kernel, mode, has_bias, tn),
        out_shape=out_shape,
        grid=(nt + 1,),
        in_specs=in_specs,
        out_specs=out_spec,
        scratch_shapes=[pltpu.VMEM((2, tm, d), BF16)],
        compiler_params=_cp("arbitrary"),
        name=name,
    )(*args)


def _op_kernel(has_bias, y_ref, w_ref, *refs):
    if has_bias:
        b_ref, r_ref, o_ref = refs
    else:
        r_ref, o_ref = refs
    if len(y_ref.shape) == 3:
        y = jnp.concatenate([y_ref[hh] for hh in range(y_ref.shape[0])], axis=1)
    else:
        y = y_ref[...]
    acc = jnp.dot(y, w_ref[...], preferred_element_type=F32)
    if has_bias:
        acc = acc + b_ref[...]
    o_ref[...] = r_ref[...] + acc


def _out_proj(y, w, resid, b=None, *, tm=512, name="out_proj"):
    k, d = w.shape
    n = resid.shape[0]
    has_bias = b is not None
    if y.ndim == 3:
        y_spec = pl.BlockSpec((k // LANES, tm, LANES), lambda i: (0, i, 0))
    else:
        y_spec = pl.BlockSpec((tm, k), lambda i: (i, 0))
    in_specs = [y_spec, pl.BlockSpec((k, d), lambda i: (0, 0))]
    args = [y, w]
    if has_bias:
        in_specs.append(pl.BlockSpec((1, d), lambda i: (0, 0)))
        args.append(b.reshape(1, d))
    in_specs.append(pl.BlockSpec((tm, d), lambda i: (i, 0)))
    args.append(resid)
    return pl.pallas_call(
        functools.partial(_op_kernel, has_bias),
        out_shape=jax.ShapeDtypeStruct((n, d), F32),
        grid=(n // tm,),
        in_specs=in_specs,
        out_specs=pl.BlockSpec((tm, d), lambda i: (i, 0)),
        compiler_params=_cp("parallel"),
        name=name,
    )(*args)


def _rel_bucket_np(dist):
    max_exact = REL_BUCKETS // 2
    d = np.maximum(dist, 0)
    df = np.maximum(d, 1).astype(np.float32)
    large = max_exact + (np.log(df / np.float32(max_exact)) / np.float32(math.log(REL_MAX_DIST / max_exact))
                         * np.float32(REL_BUCKETS - max_exact)).astype(np.int32)
    large = np.minimum(large, REL_BUCKETS - 1)
    return np.where(d < max_exact, d, large).astype(np.int32)


def _bias_kernel(scale, tab_ref, idx_ref, o_ref):
    idx = idx_ref[...]
    for h in range(o_ref.shape[1]):
        acc = jnp.zeros(idx.shape, F32)
        for b in range(REL_BUCKETS):
            acc = jnp.where(idx == b, tab_ref[b, h] * scale, acc)
        o_ref[0, h] = jnp.where(idx < 0, NEG, acc).astype(o_ref.dtype)


def _bias_tiles(rel_table, bucket_idx, out_dtype, *, tile, tr, scale=1.0):
    r, c = bucket_idx.shape
    nh = rel_table.shape[1]
    per = tile // tr
    return pl.pallas_call(
        functools.partial(_bias_kernel, scale),
        out_shape=jax.ShapeDtypeStruct((r // tile, nh, tile, c), out_dtype),
        grid=(r // tr,),
        in_specs=[pl.BlockSpec(memory_space=pltpu.SMEM),
                  pl.BlockSpec((tr, c), lambda i: (i, 0))],
        out_specs=pl.BlockSpec((1, nh, tr, c), lambda i: (i // per, 0, i % per, 0)),
        compiler_params=_cp("parallel"),
        name="bias_tiles",
    )(rel_table, jnp.asarray(bucket_idx))


def _asel_kernel(qi_ref, wi_ref, ki_ref, m_ref, keys_ref, wb_ref, x_ref, si_ref, sf_ref, *, tq, cw, s_len, topk):
    qb = pl.program_id(1)
    spc = cw // LANES
    n_chunks = (qb * tq + tq + cw - 1) // cw
    n_slabs = n_chunks * spc
    nslab_all = s_len // LANES

    wi = wi_ref[0] * (IDX_HEADS ** -0.5 * IDX_DIM ** -0.5)
    for h in range(IDX_HEADS):
        wb_ref[h] = jnp.broadcast_to(wi[:, h:h + 1], (tq, LANES))
    q_pos = qb * tq + lax.broadcasted_iota(I32, (tq, cw), 0)
    lane_pos = lax.broadcasted_iota(I32, (tq, cw), 1)

    def to_key(x):
        bits = pltpu.bitcast(x, I32)
        return bits ^ ((bits >> 31) & 0x7FFFFFFF)

    def build(c, mx):
        k0 = pl.multiple_of(c * cw, cw)
        kc = ki_ref[0, pl.ds(k0, cw), :]
        sc = jnp.zeros((tq, cw), F32)
        for h in range(IDX_HEADS):
            r = lax.dot_general(qi_ref[0, :, h * IDX_DIM:(h + 1) * IDX_DIM], kc, _NT,
                                preferred_element_type=F32)
            sc = sc + jnp.maximum(r, 0.0) * jnp.concatenate([wb_ref[h]] * spc, axis=1)
        sc = sc + 0.0
        causal = k0 + lane_pos <= q_pos
        key = jnp.where(causal, to_key(sc), INT_MIN)
        scm = jnp.where(causal, sc, -jnp.inf)
        for j in range(spc):
            keys_ref[c * spc + j] = key[:, j * LANES:(j + 1) * LANES]
            mx = jnp.maximum(mx, scm[:, j * LANES:(j + 1) * LANES])
        return mx

    mx = lax.fori_loop(0, n_chunks, build, jnp.full((tq, LANES), -jnp.inf, F32))
    key_max = to_key(jnp.broadcast_to(jnp.max(mx, axis=1, keepdims=True), (tq, LANES)))

    def lane_sum(acc):
        return jnp.broadcast_to(jnp.sum(acc.astype(F32), axis=1, keepdims=True), (tq, LANES))

    def slab_loop(body, init):
        def chunk(c, carry):
            for j in range(spc):
                carry = body(c * spc + j, carry)
            return carry
        return lax.fori_loop(0, n_chunks, chunk, init)

    zero = jnp.zeros((tq, LANES), I32)

    def count_ge(cand):
        parts = []
        for r0 in range(0, tq, 128):
            rows = slice(r0, r0 + 128)
            cand_r = cand[rows]
            parts.append(slab_loop(lambda c, acc: acc + jnp.where(keys_ref[c, rows, :] >= cand_r, 1, 0), zero[rows]))
        return lane_sum(parts[0] if len(parts) == 1 else jnp.concatenate(parts, axis=0))

    n_real = (q_pos[:, :LANES] + 1).astype(F32)
    lo0 = jnp.full((tq, LANES), INT_MIN + 1, I32)
    si_ref[0] = lo0
    si_ref[1] = jnp.where(n_real > topk, key_max + 1, lo0 + 1)
    sf_ref[0] = n_real
    sf_ref[1] = jnp.zeros((tq, LANES), F32)

    def midpoint(lo, hi):
        return lo + lax.shift_right_logical(hi - lo, 1)

    def search_pass(cand):
        lo, hi, clo, chi = si_ref[0], si_ref[1], sf_ref[0], sf_ref[1]
        cnt = count_ge(cand)
        ge = cnt >= topk
        lo, clo = jnp.where(ge, cand, lo), jnp.where(ge, cnt, clo)
        hi = jnp.where(ge, hi, jnp.maximum(cand, lo + 1))
        hi = jnp.where(clo == topk, lo + 1, hi)
        si_ref[0], si_ref[1] = lo, hi
        sf_ref[0], sf_ref[1] = clo, jnp.where(ge, chi, cnt)
        open_rows = jnp.where((hi - lo) != 1, 1.0, 0.0)
        return jnp.max(open_rows, axis=0, keepdims=True)[0, 0]

    lo, hi = si_ref[0], si_ref[1]
    search_pass(jnp.where(((hi - lo) != 1) & (hi > 1), 1, midpoint(lo, hi)))
    lo, hi = si_ref[0], si_ref[1]
    mid = midpoint(lo, hi)
    probe = hi - (1 << 25)
    cand = jnp.where((lo == 1) & (probe > mid), probe, mid)
    pending = search_pass(jnp.where(((hi - lo) != 1) & (hi == 1), 0, cand))

    def search_body(carry):
        it, _ = carry
        return it + 1, search_pass(midpoint(si_ref[0], si_ref[1]))

    lax.while_loop(lambda carry: (carry[0] < 40) & (carry[1] > 0.0), search_body, (jnp.int32(0), pending))
    tau = si_ref[0]
    clo, chi = sf_ref[0], sf_ref[1]

    r_need = topk - chi
    tied = clo > topk
    x_ref[...] = jnp.full((tq, LANES), s_len, I32)
    lane = lax.broadcasted_iota(I32, (tq, LANES), 1)

    @pl.when(jnp.max(jnp.where(tied, 1.0, 0.0)) > 0.0)
    def _():
        nbits = int(math.log2(s_len)) + 1

        def tie_pass(i, x):
            cand = x + lax.shift_left(jnp.int32(1), nbits - 1 - i)

            def body(c, acc):
                idx = c * LANES + lane
                return acc + jnp.where((keys_ref[c] == tau) & (idx < cand), 1, 0)

            cnt = lane_sum(slab_loop(body, zero))
            return jnp.where(cnt < r_need, cand, x)

        x = lax.fori_loop(0, nbits, tie_pass, zero)
        x_ref[...] = jnp.where(tied, x, s_len)

    x_cut = x_ref[...]

    def emit(c, carry):
        k = keys_ref[c]
        idx = c * LANES + lane
        sel = (k > tau) | ((k == tau) & (idx <= x_cut))
        m_ref[0, c] = jnp.where(sel, 0.0, NEG).astype(m_ref.dtype)
        return carry

    slab_loop(emit, 0)

    def fill(c, carry):
        m_ref[0, c] = jnp.full((tq, LANES), NEG, m_ref.dtype)
        return carry

    lax.fori_loop(n_slabs, nslab_all, fill, 0)


def _a_select(qi, wi, ki, *, tq=256, cw=1024):
    bsz, s_len, _ = qi.shape
    topk = min(TOPK_MAX, s_len // 4)
    nslab = s_len // LANES
    return pl.pallas_call(
        functools.partial(_asel_kernel, tq=tq, cw=cw, s_len=s_len, topk=float(topk)),
        out_shape=jax.ShapeDtypeStruct((bsz, nslab, s_len, LANES), BF16),
        grid=(bsz, s_len // tq),
        in_specs=[pl.BlockSpec((1, tq, IDX_HEADS * IDX_DIM), lambda b, i: (b, i, 0)),
                  pl.BlockSpec((1, tq, IDX_HEADS), lambda b, i: (b, i, 0)),
                  pl.BlockSpec((1, s_len, IDX_DIM), lambda b, i: (b, 0, 0))],
        out_specs=pl.BlockSpec((1, nslab, tq, LANES), lambda b, i: (b, 0, i, 0)),
        scratch_shapes=[pltpu.VMEM((nslab, tq, LANES), I32),
                        pltpu.VMEM((IDX_HEADS, tq, LANES), F32),
                        pltpu.VMEM((tq, LANES), I32),
                        pltpu.VMEM((2, tq, LANES), I32),
                        pltpu.VMEM((2, tq, LANES), F32)],
        compiler_params=_cp("parallel", "arbitrary"),
        name="a_select",
    )(qi, wi, ki)


def _attn_kernel(qt_ref, kt_ref, tab_ref, q_ref, k_ref, v_ref, msk_ref, bias_ref, o_ref,
                 m_sc, l_sc, acc_sc, mk_sc, *slots, t, nd_far, hpg):
    step = pl.program_id(1)
    qi = qt_ref[step]
    ki = kt_ref[step]

    @pl.when(ki == 0)
    def _():
        m_sc[...] = jnp.full(m_sc.shape, NEG, F32)
        l_sc[...] = jnp.zeros_like(l_sc)
        acc_sc[...] = jnp.zeros_like(acc_sc)

    mk_sc[...] = jnp.concatenate([msk_ref[0, j] for j in range(t // LANES)], axis=1).astype(F32)
    ns = 2 * hpg
    s_sc, p_sc, al_sc = slots[:ns], slots[ns:2 * ns], slots[2 * ns:]

    def slot(h):
        return (h // hpg % 2) * hpg + h % hpg

    def scores(h, far):
        s = lax.dot_general(q_ref[h, 0], k_ref[h, 0], _NT, preferred_element_type=F32) + mk_sc[...]
        if not far:
            s = s + bias_ref[0, h].astype(F32)
        s_sc[slot(h)][...] = s

    def softmax(h, far):
        s = s_sc[slot(h)][...]
        c = tab_ref[REL_BUCKETS - 1, h] * LOG2E if far else 0.0
        m_prev = m_sc[h]
        m_cur = jnp.broadcast_to(jnp.max(s, axis=-1, keepdims=True), (t, LANES))
        m_new = jnp.maximum(m_prev, m_cur + c)
        alpha = jnp.exp2(m_prev - m_new)
        p = jnp.exp2(s - jnp.concatenate([m_new - c] * (t // LANES), axis=1))
        p_part = p[:, :LANES]
        for j in range(1, t // LANES):
            p_part = p_part + p[:, j * LANES:(j + 1) * LANES]
        l_sc[h] = alpha * l_sc[h] + p_part
        m_sc[h] = m_new
        al_sc[slot(h)][...] = alpha
        p_sc[slot(h)][...] = p.astype(BF16)

    def values(h):
        acc_sc[h] = al_sc[slot(h)][...] * acc_sc[h] + jnp.dot(p_sc[slot(h)][...], v_ref[h, 0],
                                                            preferred_element_type=F32)

    one = qt_ref[0] + 1

    def heads(far):
        n_grp = A_HEADS // hpg
        for st in range(n_grp + 2):
            def stage(i, carry, st=st):
                for j in range(hpg):
                    if st < n_grp:
                        scores(st * hpg + j, far)
                    if 1 <= st <= n_grp:
                        softmax((st - 1) * hpg + j, far)
                    if st >= 2:
                        values((st - 2) * hpg + j)
                return carry

            lax.fori_loop(0, one, stage, 0)

    @pl.when(qi - ki >= nd_far)
    def _():
        heads(True)

    @pl.when(qi - ki < nd_far)
    def _():
        heads(False)

    @pl.when(ki == qi)
    def _():
        for h in range(A_HEADS):
            l = jnp.sum(l_sc[h], axis=-1, keepdims=True)
            o_ref[h, 0] = (acc_sc[h] * (1.0 / l)).astype(o_ref.dtype)


def _a_attention(qkv, mask4, bias, rel_table, *, t, nd_far, hpg=4):
    _, bsz, s_len, e = qkv.shape
    nq = s_len // t
    nd = bias.shape[0]
    qt = np.concatenate([np.full(i + 1, i, np.int32) for i in range(nq)])
    kt = np.concatenate([np.arange(i + 1, dtype=np.int32) for i in range(nq)])
    spt = t // LANES
    grid_spec = pltpu.PrefetchScalarGridSpec(
        num_scalar_prefetch=2,
        grid=(bsz, len(qt)),
        in_specs=[
            pl.BlockSpec(memory_space=pltpu.SMEM),
            pl.BlockSpec((A_HEADS, 1, t, e), lambda b, s, qt, kt: (0, b, qt[s], 0)),
            pl.BlockSpec((A_HEADS, 1, t, e), lambda b, s, qt, kt: (1, b, kt[s], 0)),
            pl.BlockSpec((A_HEADS, 1, t, e), lambda b, s, qt, kt: (2, b, kt[s], 0)),
            pl.BlockSpec((1, spt, t, LANES), lambda b, s, qt, kt: (b, kt[s], qt[s], 0)),
            pl.BlockSpec((1, A_HEADS, t, t),
                         lambda b, s, qt, kt: (jnp.minimum(qt[s] - kt[s], nd - 1), 0, 0, 0)),
        ],
        out_specs=pl.BlockSpec((A_HEADS, 1, t, e), lambda b, s, qt, kt: (0, b, qt[s], 0)),
        scratch_shapes=[pltpu.VMEM((A_HEADS, t, LANES), F32),
                        pltpu.VMEM((A_HEADS, t, LANES), F32),
                        pltpu.VMEM((A_HEADS, t, e), F32),
                        pltpu.VMEM((t, t), F32)]
        + [pltpu.VMEM((t, t), F32)] * (2 * hpg)
        + [pltpu.VMEM((t, t), BF16)] * (2 * hpg)
        + [pltpu.VMEM((t, LANES), F32)] * (2 * hpg),
    )
    return pl.pallas_call(
        functools.partial(_attn_kernel, t=t, nd_far=nd_far, hpg=hpg),
        out_shape=jax.ShapeDtypeStruct((A_HEADS, bsz, s_len, e), BF16),
        grid_spec=grid_spec,
        compiler_params=_cp("parallel", "arbitrary"),
        name="a_attention",
    )(jnp.asarray(qt), jnp.asarray(kt), rel_table, qkv, qkv, qkv, mask4, bias)


def _mixer_a(h, g, w_in, w_out, rel_table, bsz, s_len):
    n = h.shape[0]
    d = D_MODEL
    n_idx = IDX_HEADS * IDX_DIM + IDX_DIM + IDX_HEADS
    w_qkv = jnp.concatenate([w_in[:, :d] * (A_HEAD_DIM ** -0.5 * LOG2E), w_in[:, d:3 * d]], axis=1).astype(BF16)
    w_idx = jnp.pad(w_in[:, 3 * d:], ((0, 0), (0, 640 - n_idx))).astype(BF16)
    qkv = _norm_matmul(h, g, w_qkv, out_dtype=BF16, tn=1024, head_major=True, name="a_qkv")
    idx = _norm_matmul(h, g, w_idx, out_dtype=F32, tn=640, name="a_idx")
    qkv = qkv.reshape(3 * A_HEADS, bsz, s_len, A_HEAD_DIM)
    idx = idx.reshape(bsz, s_len, 640)
    nqi = IDX_HEADS * IDX_DIM
    qi = idx[..., :nqi].astype(BF16)
    ki = idx[..., nqi:nqi + IDX_DIM].astype(BF16)
    wi = idx[..., nqi + IDX_DIM:n_idx]
    mask4 = _a_select(qi, wi, ki)

    t = min(512, s_len)
    nd_far = 0
    while _rel_bucket_np(np.array(nd_far * t - (t - 1))) < REL_BUCKETS - 1:
        nd_far += 1
    nd = min(nd_far, s_len // t)
    ii = np.arange(t)[:, None] - np.arange(t)[None, :]
    dist = (np.arange(nd)[:, None, None] * t + ii[None]).reshape(nd * t, t)
    bias = _bias_tiles(rel_table, _rel_bucket_np(dist), BF16, tile=t, tr=t // 2, scale=LOG2E)
    o = _a_attention(qkv, mask4, bias, rel_table, t=t, nd_far=nd_far)
    return _out_proj(o.reshape(A_HEADS, n, A_HEAD_DIM), w_out.astype(BF16), h, name="a_out")


def _bsp_kernel(u_ref, v_ref, lg_ref, lb_ref, wsp_ref, bsp_ref, o_ref, wt_ref, *, rows):
    @pl.when(pl.program_id(0) == 0)
    def _():
        r = lax.broadcasted_iota(I32, (B_CHUNK, B_CHUNK), 0)
        c = lax.broadcasted_iota(I32, (B_CHUNK, B_CHUNK), 1)
        for g in range(B_GROUPS):
            wt_ref[g] = jnp.where(r >= c, wsp_ref[g], 0.0).astype(BF16)

    vn = _layernorm(v_ref[...].astype(F32), lg_ref[...], lb_ref[...]).astype(BF16)
    gw = B_HALF // B_GROUPS
    for c in range(rows // B_CHUNK):
        rs = slice(c * B_CHUNK, (c + 1) * B_CHUNK)
        for g in range(B_GROUPS):
            cs = slice(g * gw, (g + 1) * gw)
            sv = jnp.dot(wt_ref[g], vn[rs, cs], preferred_element_type=F32) + bsp_ref[g]
            o_ref[rs, cs] = (u_ref[rs, cs].astype(F32) * sv).astype(o_ref.dtype)


def _b_spatial(z, ln_g, ln_b, w_sp, b_sp, *, rows=512):
    n = z.shape[0]
    return pl.pallas_call(
        functools.partial(_bsp_kernel, rows=rows),
        out_shape=jax.ShapeDtypeStruct((n, B_HALF), BF16),
        grid=(n // rows,),
        in_specs=[pl.BlockSpec((rows, B_HALF), lambda i: (i, 0)),
                  pl.BlockSpec((rows, B_HALF), lambda i: (i, 1)),
                  pl.BlockSpec((1, B_HALF), lambda i: (0, 0)),
                  pl.BlockSpec((1, B_HALF), lambda i: (0, 0)),
                  pl.BlockSpec((B_GROUPS, B_CHUNK, B_CHUNK), lambda i: (0, 0, 0)),
                  pl.BlockSpec((B_GROUPS, B_CHUNK, 1), lambda i: (0, 0, 0))],
        out_specs=pl.BlockSpec((rows, B_HALF), lambda i: (i, 0)),
        scratch_shapes=[pltpu.VMEM((B_GROUPS, B_CHUNK, B_CHUNK), BF16)],
        compiler_params=_cp("arbitrary"),
        name="b_spatial",
    )(z, z, ln_g.reshape(1, B_HALF), ln_b.reshape(1, B_HALF), w_sp, b_sp.reshape(B_GROUPS, B_CHUNK, 1))


def _mixer_b(h, g, w_in, b_in, ln_g, ln_b, w_sp, b_sp, w_out):
    z = _norm_matmul(h, g, w_in.astype(BF16), b_in, mode="gelu", out_dtype=BF16, tn=1024, name="b_in")
    y = _b_spatial(z, ln_g, ln_b, w_sp, b_sp)
    return _out_proj(y, w_out.astype(BF16), h, name="b_out")


C_HALO = 32


def _conv_kernel(cur_ref, halo_ref, wdw_ref, bdw_ref, lg_ref, lb_ref, w2_ref, b2_ref, r_ref, o_ref,
                 ext_ref, yc_ref, *, ts, rt, ct):
    i = pl.program_id(1)
    ext_ref[C_HALO:, :] = cur_ref[0]

    @pl.when(i == 0)
    def _():
        ext_ref[:C_HALO, :] = jnp.zeros((C_HALO, D_MODEL), F32)

    @pl.when(i > 0)
    def _():
        ext_ref[:C_HALO, :] = halo_ref[0]

    off = C_HALO - (C_KERNEL - 1)

    def tile(it, carry):
        r0 = pl.multiple_of((it // (D_MODEL // ct)) * rt, rt)
        c0 = pl.multiple_of((it % (D_MODEL // ct)) * ct, ct)
        nwin = rt + C_HALO
        win = ext_ref[pl.ds(r0, nwin), pl.ds(c0, ct)]
        acc = jnp.zeros((rt, ct), F32) + bdw_ref[:, pl.ds(c0, ct)]
        for b in range(8):
            rolled = pltpu.roll(win, nwin - (off + b), axis=0)
            for a in range((C_KERNEL - b + 7) // 8):
                j = 8 * a + b
                acc = acc + rolled[8 * a:8 * a + rt] * wdw_ref[j:j + 1, pl.ds(c0, ct)]
        yc_ref[pl.ds(r0, rt), pl.ds(c0, ct)] = acc
        return carry

    lax.fori_loop(0, (ts // rt) * (D_MODEL // ct), tile, 0)
    y = _layernorm(yc_ref[...], lg_ref[...], lb_ref[...])
    y = (y * jax.nn.sigmoid(y)).astype(BF16)
    o_ref[0] = r_ref[0] + jnp.dot(y, w2_ref[...], preferred_element_type=F32) + b2_ref[...]


def _c_conv(y1, w_dw, b_dw, ln_g, ln_b, w2, b2, resid, *, ts=512, rt=32, ct=512):
    bsz, s_len, d = y1.shape
    hb = ts // C_HALO
    vec = lambda a: a.reshape(1, d)
    return pl.pallas_call(
        functools.partial(_conv_kernel, ts=ts, rt=rt, ct=ct),
        out_shape=jax.ShapeDtypeStruct((bsz, s_len, d), F32),
        grid=(bsz, s_len // ts),
        in_specs=[pl.BlockSpec((1, ts, d), lambda b, i: (b, i, 0)),
                  pl.BlockSpec((1, C_HALO, d), lambda b, i: (b, jnp.maximum(i * hb - 1, 0), 0)),
                  pl.BlockSpec((C_KERNEL, d), lambda b, i: (0, 0)),
                  pl.BlockSpec((1, d), lambda b, i: (0, 0)),
                  pl.BlockSpec((1, d), lambda b, i: (0, 0)),
                  pl.BlockSpec((1, d), lambda b, i: (0, 0)),
                  pl.BlockSpec((d, d), lambda b, i: (0, 0)),
                  pl.BlockSpec((1, d), lambda b, i: (0, 0)),
                  pl.BlockSpec((1, ts, d), lambda b, i: (b, i, 0))],
        out_specs=pl.BlockSpec((1, ts, d), lambda b, i: (b, i, 0)),
        scratch_shapes=[pltpu.VMEM((ts + C_HALO, d), F32), pltpu.VMEM((ts, d), F32)],
        compiler_params=_cp("parallel", "parallel"),
        name="c_conv",
    )(y1, y1, w_dw, vec(b_dw), vec(ln_g), vec(ln_b), w2, vec(b2), resid)


def _mixer_c(h, g, w_pw1, b_pw1, w_dw, b_dw, ln_g, ln_b, w_pw2, b_pw2, bsz, s_len):
    n, d = h.shape
    y1 = _norm_matmul(h, g, w_pw1.astype(BF16), b_pw1, mode="glu", out_dtype=F32, tn=512, name="c_pw1")
    out = _c_conv(y1.reshape(bsz, s_len, d), w_dw, b_dw, ln_g, ln_b, w_pw2.astype(BF16), b_pw2,
                  h.reshape(bsz, s_len, d))
    return out.reshape(n, d)


def _dattn_kernel(q_ref, kc_ref, kp_ref, vc_ref, vp_ref, bias_ref, o_ref, lse_ref):
    nb = pl.program_id(2)
    first = jnp.where(nb == 0, NEG, 0.0)
    e = D_HEAD_DIM
    for h in range(D_HEADS):
        hs = slice(h * e, (h + 1) * e)
        q = q_ref[:, hs]
        kk = jnp.concatenate([kp_ref[:, hs], kc_ref[:, hs]], axis=0)
        vv = jnp.concatenate([vp_ref[:, hs], vc_ref[:, hs]], axis=0)
        s = lax.dot_general(q, kk, _NT, preferred_element_type=F32) * (e ** -0.5) + bias_ref[h]
        col = lax.broadcasted_iota(I32, s.shape, 1)
        s = s + jnp.where(col < D_BLOCK, first, 0.0)
        m = jnp.max(s, axis=-1, keepdims=True)
        p = jnp.exp(s - m)
        l = jnp.sum(p, axis=-1, keepdims=True)
        o = jnp.dot(p.astype(BF16), vv, preferred_element_type=F32) / l
        o_ref[0, :, hs] = o
        lse_ref[0, :, hs] = jnp.broadcast_to(m + jnp.log(l), (D_BLOCK, e))


def _dproj_kernel(x_ref, g_ref, w_ref, o_ref, xs_ref, *, dil):
    span, d = x_ref.shape
    nslab = d // LANES
    for j in range(nslab):
        xs_ref[j] = x_ref[:, j * LANES:(j + 1) * LANES]
    rows = 4 * D_BLOCK
    for c in range(span // rows):
        xr = jnp.concatenate(
            [jnp.concatenate([xs_ref[j, pl.ds(r, D_BLOCK, stride=dil), :] for j in range(nslab)], axis=1)
             for r in range(4 * c, 4 * c + 4)], axis=0)
        y = jnp.dot(_rms_bf16(xr, g_ref[...]), w_ref[...], preferred_element_type=F32)
        o_ref[c * rows:(c + 1) * rows, :] = y.astype(o_ref.dtype)


def _d_proj(h, g, w, dil):
    n, d = h.shape
    m = w.shape[1]
    span = D_BLOCK * dil
    return pl.pallas_call(
        functools.partial(_dproj_kernel, dil=dil),
        out_shape=jax.ShapeDtypeStruct((n, m), BF16),
        grid=(n // span,),
        in_specs=[pl.BlockSpec((span, d), lambda i: (i, 0)),
                  pl.BlockSpec((1, d), lambda i: (0, 0)),
                  pl.BlockSpec((d, m), lambda i: (0, 0))],
        out_specs=pl.BlockSpec((span, m), lambda i: (i, 0)),
        scratch_shapes=[pltpu.VMEM((d // LANES, span, LANES), F32)],
        compiler_params=_cp("parallel"),
        name=f"d_in_dil{dil}",
    )(h, g.reshape(1, d), w)


def _d_group(proj, bias_g, gidx, dil, bsz, s_len):
    n_sub = s_len // dil
    nb = n_sub // D_BLOCK
    w = D_HEADS * D_HEAD_DIM
    blk = (D_BLOCK, w)
    rb = lambda b, r, n: b * (s_len // D_BLOCK) + n * dil + r
    cur = lambda col: (lambda b, r, n: (rb(b, r, n), col))
    prev = lambda col: (lambda b, r, n: (rb(b, r, jnp.maximum(n - 1, 0)), col))
    out_blk = (1, D_BLOCK, w)
    out_sds = jax.ShapeDtypeStruct((bsz, n_sub, dil * w), F32)
    o, lse = pl.pallas_call(
        _dattn_kernel,
        out_shape=(out_sds, out_sds),
        grid=(bsz, dil, nb),
        in_specs=[pl.BlockSpec(blk, cur(0)),
                  pl.BlockSpec(blk, cur(1)),
                  pl.BlockSpec(blk, prev(1)),
                  pl.BlockSpec(blk, cur(2)),
                  pl.BlockSpec(blk, prev(2)),
                  pl.BlockSpec((D_HEADS, D_BLOCK, 2 * D_BLOCK), lambda b, r, n: (0, 0, 0))],
        out_specs=(pl.BlockSpec(out_blk, lambda b, r, n: (b, n, r)),
                   pl.BlockSpec(out_blk, lambda b, r, n: (b, n, r))),
        compiler_params=_cp("parallel", "parallel", "arbitrary"),
        name=f"d_attn_g{gidx}",
    )(proj, proj, proj, proj, proj, bias_g)
    return o.reshape(bsz * s_len, w), lse.reshape(bsz * s_len, w)


def _dmerge_kernel(o0, o1, o2, l0, l1, l2, w_ref, r_ref, out_ref):
    a0, a1, a2 = l0[...], l1[...], l2[...]
    mx = jnp.maximum(jnp.maximum(a0, a1), a2)
    e0, e1, e2 = jnp.exp(a0 - mx), jnp.exp(a1 - mx), jnp.exp(a2 - mx)
    y = (e0 * o0[...] + e1 * o1[...] + e2 * o2[...]) / (e0 + e1 + e2)
    out_ref[...] = r_ref[...] + jnp.dot(y.astype(BF16), w_ref[...], preferred_element_type=F32)


def _d_merge(outs, lses, w_out, resid, *, tm=512):
    n, w = outs[0].shape
    d = resid.shape[1]
    row = pl.BlockSpec((tm, w), lambda i: (i, 0))
    return pl.pallas_call(
        _dmerge_kernel,
        out_shape=jax.ShapeDtypeStruct((n, d), F32),
        grid=(n // tm,),
        in_specs=[row] * 6 + [pl.BlockSpec((w, d), lambda i: (0, 0)),
                              pl.BlockSpec((tm, d), lambda i: (i, 0))],
        out_specs=pl.BlockSpec((tm, d), lambda i: (i, 0)),
        compiler_params=_cp("parallel"),
        name="d_merge",
    )(*outs, *lses, w_out, resid)


def _mixer_d(h, g, w_in, w_out, rel_table, bsz, s_len):
    w = D_HEADS * D_HEAD_DIM
    ng = len(D_PAIRS)
    p_loc = np.arange(D_BLOCK)[:, None]
    j_loc = np.arange(2 * D_BLOCK)[None, :]
    m = p_loc + D_BLOCK - j_loc
    idx = []
    for window, dil in D_PAIRS:
        steps = window // dil
        idx.append(np.where((m >= 0) & (m <= steps), _rel_bucket_np(m * dil), -1))
    bias = _bias_tiles(rel_table, np.concatenate(idx, 0).astype(np.int32), F32, tile=D_BLOCK, tr=D_BLOCK)
    outs, lses = [], []
    for gi, (_, dil) in enumerate(D_PAIRS):
        cols = [w_in[:, (part * ng + gi) * w:(part * ng + gi + 1) * w] for part in range(3)]
        w_g = jnp.concatenate(cols, axis=1).astype(BF16)
        if dil == 1:
            proj = _norm_matmul(h, g, w_g, out_dtype=BF16, tn=3 * w, name="d_in_dil1")
        else:
            proj = _d_proj(h, g, w_g, dil)
        o, lse = _d_group(proj, bias[gi], gi, dil, bsz, s_len)
        outs.append(o)
        lses.append(lse)
    return _d_merge(outs, lses, w_out.astype(BF16), h)


def _fnorm_kernel(x_ref, g_ref, o_ref):
    x = x_ref[...]
    ms = jnp.mean(x * x, axis=-1, keepdims=True)
    o_ref[...] = x * lax.rsqrt(ms + EPS) * g_ref[...]


def _final_norm(h, g, *, tm=1024):
    n, d = h.shape
    return pl.pallas_call(
        _fnorm_kernel,
        out_shape=jax.ShapeDtypeStruct((n, d), F32),
        grid=(n // tm,),
        in_specs=[pl.BlockSpec((tm, d), lambda i: (i, 0)), pl.BlockSpec((1, d), lambda i: (0, 0))],
        out_specs=pl.BlockSpec((tm, d), lambda i: (i, 0)),
        compiler_params=_cp("parallel"),
        name="final_norm",
    )(h, g.reshape(1, d))


def kernel(x, norm_g, final_g, ffn_w_in, ffn_w_out, rel_table, a_w_in, a_w_out, b_w_in, b_b_in, b_ln_g, b_ln_b, b_w_sp, b_b_sp, b_w_out, c_w_pw1, c_b_pw1, c_w_dw, c_b_dw, c_ln_g, c_ln_b, c_w_pw2, c_b_pw2, d_w_in, d_w_out):
    bsz, s_len, d = x.shape
    depth = norm_g.shape[0]
    h = x.reshape(bsz * s_len, d)
    for i in range(depth):
        kind, j = i % 4, i // 4
        h = _ffn(h, norm_g[i, 0], ffn_w_in[i, 0].astype(BF16), ffn_w_out[i, 0].astype(BF16))
        g = norm_g[i, 1]
        if kind == 0:
            h = _mixer_a(h, g, a_w_in[j], a_w_out[j], rel_table, bsz, s_len)
        elif kind == 1:
            h = _mixer_b(h, g, b_w_in[j], b_b_in[j], b_ln_g[j], b_ln_b[j], b_w_sp[j], b_b_sp[j], b_w_out[j])
        elif kind == 2:
            h = _mixer_c(h, g, c_w_pw1[j], c_b_pw1[j], c_w_dw[j], c_b_dw[j], c_ln_g[j], c_ln_b[j],
                         c_w_pw2[j], c_b_pw2[j], bsz, s_len)
        else:
            h = _mixer_d(h, g, d_w_in[j], d_w_out[j], rel_table, bsz, s_len)
        h = _ffn(h, norm_g[i, 2], ffn_w_in[i, 1].astype(BF16), ffn_w_out[i, 1].astype(BF16))
    return _final_norm(h, final_g).reshape(bsz, s_len, d)
```

```python
import functools
import math

import numpy as np
import jax
import jax.numpy as jnp
from jax import lax
from jax.experimental import pallas as pl
from jax.experimental.pallas import tpu as pltpu

F32, BF16, I32 = jnp.float32, jnp.bfloat16, jnp.int32

D_MODEL = 1024
EPS = 1e-6
D_FF = 2816
REL_BUCKETS = 32
REL_MAX_DIST = 2048
A_HEADS = 8
A_HEAD_DIM = 128
IDX_HEADS = 8
IDX_DIM = 64
TOPK_MAX = 256
B_CHUNK = 128
B_HALF = 3 * D_MODEL
B_GROUPS = 8
C_KERNEL = 31
D_PAIRS = ((128, 1), (512, 4), (2048, 16))
D_HEADS = 8
D_HEAD_DIM = 64
D_BLOCK = 128

LANES = 128
MXU_TILE = 256
VMEM_LIMIT = 56 * 1024 * 1024
NEG = -1e30
INT_MIN = -2 ** 31
LOG2E = math.log2(math.e)

_NT = (((1,), (1,)), ((), ()))


def _cp(*sem, flags=None):
    return pltpu.CompilerParams(dimension_semantics=sem, vmem_limit_bytes=VMEM_LIMIT, flags=flags)


def _rms_bf16(x, g):
    ms = jnp.mean(x * x, axis=-1, keepdims=True)
    return (x * lax.rsqrt(ms + EPS) * g).astype(BF16)


def _layernorm(x, g, b):
    mu = jnp.mean(x, axis=-1, keepdims=True)
    xc = x - mu
    var = jnp.mean(xc * xc, axis=-1, keepdims=True)
    return xc * lax.rsqrt(var + EPS) * g + b


def _ffn_kernel(x_ref, xnext_ref, g_ref, wi_ref, wo_ref, o_ref, xn_ref, *, chunks):
    i = pl.program_id(0)
    slot = i % 2

    @pl.when(i == 0)
    def _():
        xn_ref[0] = _rms_bf16(x_ref[...], g_ref[...])

    xn = xn_ref[slot]
    acc = None
    c0 = 0
    for tf in chunks:
        gate = jnp.dot(xn, wi_ref[:, c0:c0 + tf], preferred_element_type=F32)
        up = jnp.dot(xn, wi_ref[:, D_FF + c0:D_FF + c0 + tf], preferred_element_type=F32)
        a = (gate * jax.nn.sigmoid(gate) * up).astype(BF16)
        part = jnp.dot(a, wo_ref[c0:c0 + tf, :], preferred_element_type=F32)
        acc = part if acc is None else acc + part
        c0 += tf
    o_ref[...] = x_ref[...] + 0.5 * acc
    xn_ref[1 - slot] = _rms_bf16(xnext_ref[...], g_ref[...])


def _ffn(h, g, w_in, w_out, *, tm=512):
    n, d = h.shape
    last = n // tm - 1
    n_tiles = D_FF // MXU_TILE
    chunks = ((n_tiles + 1) // 2 * MXU_TILE, n_tiles // 2 * MXU_TILE)
    resident = dict(pipeline_mode=pl.Buffered(1))
    return pl.pallas_call(
        functools.partial(_ffn_kernel, chunks=chunks),
        out_shape=jax.ShapeDtypeStruct((n, d), F32),
        grid=(n // tm,),
        in_specs=[
            pl.BlockSpec((tm, d), lambda i: (i, 0)),
            pl.BlockSpec((tm, d), lambda i: (jnp.minimum(i + 1, last), 0)),
            pl.BlockSpec((1, d), lambda i: (0, 0)),
            pl.BlockSpec((d, 2 * D_FF), lambda i: (0, 0), **resident),
            pl.BlockSpec((D_FF, d), lambda i: (0, 0), **resident),
        ],
        out_specs=pl.BlockSpec((tm, d), lambda i: (i, 0)),
        scratch_shapes=[pltpu.VMEM((2, tm, d), BF16)],
        compiler_params=_cp("arbitrary"),
        name="ffn",
    )(h, h, g.reshape(1, d), w_in, w_out)


def _nm_kernel(mode, has_bias, tn, x_ref, g_ref, w_ref, *refs):
    b_ref = refs[0] if has_bias else None
    o_ref, xn_ref = refs[-2], refs[-1]
    s = pl.program_id(0)
    head_major = len(o_ref.shape) == 3
    m_out = o_ref.shape[0] * LANES if head_major else o_ref.shape[1]

    @pl.when(s == 0)
    def _():
        xn_ref[0] = _rms_bf16(x_ref[...], g_ref[...])

    @pl.when(s > 0)
    def _():
        xn = xn_ref[(s - 1) % 2]

        def column_block(c0):
            y = jnp.dot(xn, w_ref[:, c0:c0 + tn], preferred_element_type=F32)
            return y + b_ref[:, c0:c0 + tn] if has_bias else y

        for j in range(m_out // tn):
            y = column_block(j * tn)
            if mode == "gelu":
                y = jax.nn.gelu(y)
            elif mode == "glu":
                y = y * jax.nn.sigmoid(column_block(m_out + j * tn))
            if head_major:
                for hh in range(tn // LANES):
                    o_ref[j * (tn // LANES) + hh] = y[:, hh * LANES:(hh + 1) * LANES].astype(o_ref.dtype)
            else:
                o_ref[:, j * tn:(j + 1) * tn] = y.astype(o_ref.dtype)
        xn_ref[s % 2] = _rms_bf16(x_ref[...], g_ref[...])


def _norm_matmul(h, g, w, b=None, *, mode="plain", out_dtype=BF16, tm=512, tn=512, head_major=False, name="proj"):
    n, d = h.shape
    m_total = w.shape[1]
    m_out = m_total // 2 if mode == "glu" else m_total
    nt = n // tm
    has_bias = b is not None
    prev = lambda s: jnp.maximum(s - 1, 0)
    if head_major:
        out_shape = jax.ShapeDtypeStruct((m_out // LANES, n, LANES), out_dtype)
        out_spec = pl.BlockSpec((m_out // LANES, tm, LANES), lambda s: (0, prev(s), 0))
    else:
        out_shape = jax.ShapeDtypeStruct((n, m_out), out_dtype)
        out_spec = pl.BlockSpec((tm, m_out), lambda s: (prev(s), 0))
    resident = dict(pipeline_mode=pl.Buffered(1))
    in_specs = [pl.BlockSpec((tm, d), lambda s: (jnp.minimum(s, nt - 1), 0)),
                pl.BlockSpec((1, d), lambda s: (0, 0)),
                pl.BlockSpec((d, m_total), lambda s: (0, 0), **resident)]
    args = [h, g.reshape(1, d), w]
    if has_bias:
        in_specs.append(pl.BlockSpec((1, m_total), lambda s: (0, 0)))
        args.append(b.reshape(1, m_total))
    return pl.pallas_call(
        functools.partial(_nm_kernel, mode, has_bias, tn),
        out_shape=out_shape,
        grid=(nt + 1,),
        in_specs=in_specs,
        out_specs=out_spec,
        scratch_shapes=[pltpu.VMEM((2, tm, d), BF16)],
        compiler_params=_cp("arbitrary"),
        name=name,
    )(*args)


def _op_kernel(has_bias, y_ref, w_ref, *refs):
    if has_bias:
        b_ref, r_ref, o_ref = refs
    else:
        r_ref, o_ref = refs
    if len(y_ref.shape) == 3:
        y = jnp.concatenate([y_ref[hh] for hh in range(y_ref.shape[0])], axis=1)
    else:
        y = y_ref[...]
    acc = jnp.dot(y, w_ref[...], preferred_element_type=F32)
    if has_bias:
        acc = acc + b_ref[...]
    o_ref[...] = r_ref[...] + acc


def _out_proj(y, w, resid, b=None, *, tm=512, name="out_proj"):
    k, d = w.shape
    n = resid.shape[0]
    has_bias = b is not None
    if y.ndim == 3:
        y_spec = pl.BlockSpec((k // LANES, tm, LANES), lambda i: (0, i, 0))
    else:
        y_spec = pl.BlockSpec((tm, k), lambda i: (i, 0))
    in_specs = [y_spec, pl.BlockSpec((k, d), lambda i: (0, 0))]
    args = [y, w]
    if has_bias:
        in_specs.append(pl.BlockSpec((1, d), lambda i: (0, 0)))
        args.append(b.reshape(1, d))
    in_specs.append(pl.BlockSpec((tm, d), lambda i: (i, 0)))
    args.append(resid)
    return pl.pallas_call(
        functools.partial(_op_kernel, has_bias),
        out_shape=jax.ShapeDtypeStruct((n, d), F32),
        grid=(n // tm,),
        in_specs=in_specs,
        out_specs=pl.BlockSpec((tm, d), lambda i: (i, 0)),
        compiler_params=_cp("parallel"),
        name=name,
    )(*args)


def _rel_bucket_np(dist):
    max_exact = REL_BUCKETS // 2
    d = np.maximum(dist, 0)
    df = np.maximum(d, 1).astype(np.float32)
    large = max_exact + (np.log(df / np.float32(max_exact)) / np.float32(math.log(REL_MAX_DIST / max_exact))
                         * np.float32(REL_BUCKETS - max_exact)).astype(np.int32)
    large = np.minimum(large, REL_BUCKETS - 1)
    return np.where(d < max_exact, d, large).astype(np.int32)


def _bias_kernel(scale, tab_ref, idx_ref, o_ref):
    idx = idx_ref[...]
    for h in range(o_ref.shape[1]):
        acc = jnp.zeros(idx.shape, F32)
        for b in range(REL_BUCKETS):
            acc = jnp.where(idx == b, tab_ref[b, h] * scale, acc)
        o_ref[0, h] = jnp.where(idx < 0, NEG, acc).astype(o_ref.dtype)


def _bias_tiles(rel_table, bucket_idx, out_dtype, *, tile, tr, scale=1.0):
    r, c = bucket_idx.shape
    nh = rel_table.shape[1]
    per = tile // tr
    return pl.pallas_call(
        functools.partial(_bias_kernel, scale),
        out_shape=jax.ShapeDtypeStruct((r // tile, nh, tile, c), out_dtype),
        grid=(r // tr,),
        in_specs=[pl.BlockSpec(memory_space=pltpu.SMEM),
                  pl.BlockSpec((tr, c), lambda i: (i, 0))],
        out_specs=pl.BlockSpec((1, nh, tr, c), lambda i: (i // per, 0, i % per, 0)),
        compiler_params=_cp("parallel"),
        name="bias_tiles",
    )(rel_table, jnp.asarray(bucket_idx))


def _asel_kernel(qi_ref, wi_ref, ki_ref, m_ref, keys_ref, wb_ref, x_ref, si_ref, sf_ref, *, tq, cw, s_len, topk):
    qb = pl.program_id(1)
    spc = cw // LANES
    n_chunks = (qb * tq + tq + cw - 1) // cw
    n_slabs = n_chunks * spc
    nslab_all = s_len // LANES

    wi = wi_ref[0] * (IDX_HEADS ** -0.5 * IDX_DIM ** -0.5)
    for h in range(IDX_HEADS):
        wb_ref[h] = jnp.broadcast_to(wi[:, h:h + 1], (tq, LANES))
    q_pos = qb * tq + lax.broadcasted_iota(I32, (tq, cw), 0)
    lane_pos = lax.broadcasted_iota(I32, (tq, cw), 1)

    def to_key(x):
        bits = pltpu.bitcast(x, I32)
        return bits ^ ((bits >> 31) & 0x7FFFFFFF)

    def build(c, mx):
        k0 = pl.multiple_of(c * cw, cw)
        kc = ki_ref[0, pl.ds(k0, cw), :]
        sc = jnp.zeros((tq, cw), F32)
        for h in range(IDX_HEADS):
            r = lax.dot_general(qi_ref[0, :, h * IDX_DIM:(h + 1) * IDX_DIM], kc, _NT,
                                preferred_element_type=F32)
            sc = sc + jnp.maximum(r, 0.0) * jnp.concatenate([wb_ref[h]] * spc, axis=1)
        sc = sc + 0.0
        causal = k0 + lane_pos <= q_pos
        key = jnp.where(causal, to_key(sc), INT_MIN)
        scm = jnp.where(causal, sc, -jnp.inf)
        for j in range(spc):
            keys_ref[c * spc + j] = key[:, j * LANES:(j + 1) * LANES]
            mx = jnp.maximum(mx, scm[:, j * LANES:(j + 1) * LANES])
        return mx

    mx = lax.fori_loop(0, n_chunks, build, jnp.full((tq, LANES), -jnp.inf, F32))
    key_max = to_key(jnp.broadcast_to(jnp.max(mx, axis=1, keepdims=True), (tq, LANES)))

    def lane_sum(acc):
        return jnp.broadcast_to(jnp.sum(acc.astype(F32), axis=1, keepdims=True), (tq, LANES))

    def slab_loop(body, init):
        def chunk(c, carry):
            for j in range(spc):
                carry = body(c * spc + j, carry)
            return carry
        return lax.fori_loop(0, n_chunks, chunk, init)

    zero = jnp.zeros((tq, LANES), I32)

    def count_ge(cand):
        parts = []
        for r0 in range(0, tq, 128):
            rows = slice(r0, r0 + 128)
            cand_r = cand[rows]
            parts.append(slab_loop(lambda c, acc: acc + jnp.where(keys_ref[c, rows, :] >= cand_r, 1, 0), zero[rows]))
        return lane_sum(parts[0] if len(parts) == 1 else jnp.concatenate(parts, axis=0))

    n_real = (q_pos[:, :LANES] + 1).astype(F32)
    lo0 = jnp.full((tq, LANES), INT_MIN + 1, I32)
    si_ref[0] = lo0
    si_ref[1] = jnp.where(n_real > topk, key_max + 1, lo0 + 1)
    sf_ref[0] = n_real
    sf_ref[1] = jnp.zeros((tq, LANES), F32)

    def midpoint(lo, hi):
        return lo + lax.shift_right_logical(hi - lo, 1)

    def search_pass(cand):
        lo, hi, clo, chi = si_ref[0], si_ref[1], sf_ref[0], sf_ref[1]
        cnt = count_ge(cand)
        ge = cnt >= topk
        lo, clo = jnp.where(ge, cand, lo), jnp.where(ge, cnt, clo)
        hi = jnp.where(ge, hi, jnp.maximum(cand, lo + 1))
        hi = jnp.where(clo == topk, lo + 1, hi)
        si_ref[0], si_ref[1] = lo, hi
        sf_ref[0], sf_ref[1] = clo, jnp.where(ge, chi, cnt)
        open_rows = jnp.where((hi - lo) != 1, 1.0, 0.0)
        return jnp.max(open_rows, axis=0, keepdims=True)[0, 0]

    lo, hi = si_ref[0], si_ref[1]
    search_pass(jnp.where(((hi - lo) != 1) & (hi > 1), 1, midpoint(lo, hi)))
    lo, hi = si_ref[0], si_ref[1]
    mid = midpoint(lo, hi)
    probe = hi - (1 << 25)
    cand = jnp.where((lo == 1) & (probe > mid), probe, mid)
    pending = search_pass(jnp.where(((hi - lo) != 1) & (hi == 1), 0, cand))

    def search_body(carry):
        it, _ = carry
        return it + 1, search_pass(midpoint(si_ref[0], si_ref[1]))

    lax.while_loop(lambda carry: (carry[0] < 40) & (carry[1] > 0.0), search_body, (jnp.int32(0), pending))
    tau = si_ref[0]
    clo, chi = sf_ref[0], sf_ref[1]

    r_need = topk - chi
    tied = clo > topk
    x_ref[...] = jnp.full((tq, LANES), s_len, I32)
    lane = lax.broadcasted_iota(I32, (tq, LANES), 1)

    @pl.when(jnp.max(jnp.where(tied, 1.0, 0.0)) > 0.0)
    def _():
        nbits = int(math.log2(s_len)) + 1

        def tie_pass(i, x):
            cand = x + lax.shift_left(jnp.int32(1), nbits - 1 - i)

            def body(c, acc):
                idx = c * LANES + lane
                return acc + jnp.where((keys_ref[c] == tau) & (idx < cand), 1, 0)

            cnt = lane_sum(slab_loop(body, zero))
            return jnp.where(cnt < r_need, cand, x)

        x = lax.fori_loop(0, nbits, tie_pass, zero)
        x_ref[...] = jnp.where(tied, x, s_len)

    x_cut = x_ref[...]

    def emit(c, carry):
        k = keys_ref[c]
        idx = c * LANES + lane
        sel = (k > tau) | ((k == tau) & (idx <= x_cut))
        m_ref[0, c] = jnp.where(sel, 0.0, NEG).astype(m_ref.dtype)
        return carry

    slab_loop(emit, 0)

    def fill(c, carry):
        m_ref[0, c] = jnp.full((tq, LANES), NEG, m_ref.dtype)
        return carry

    lax.fori_loop(n_slabs, nslab_all, fill, 0)


def _a_select(qi, wi, ki, *, tq=128, cw=1024):
    bsz, s_len, _ = qi.shape
    topk = min(TOPK_MAX, s_len // 4)
    nslab = s_len // LANES
    return pl.pallas_call(
        functools.partial(_asel_kernel, tq=tq, cw=cw, s_len=s_len, topk=float(topk)),
        out_shape=jax.ShapeDtypeStruct((bsz, nslab, s_len, LANES), BF16),
        grid=(bsz, s_len // tq),
        in_specs=[pl.BlockSpec((1, tq, IDX_HEADS * IDX_DIM), lambda b, i: (b, i, 0)),
                  pl.BlockSpec((1, tq, IDX_HEADS), lambda b, i: (b, i, 0)),
                  pl.BlockSpec((1, s_len, IDX_DIM), lambda b, i: (b, 0, 0))],
        out_specs=pl.BlockSpec((1, nslab, tq, LANES), lambda b, i: (b, 0, i, 0)),
        scratch_shapes=[pltpu.VMEM((nslab, tq, LANES), I32),
                        pltpu.VMEM((IDX_HEADS, tq, LANES), F32),
                        pltpu.VMEM((tq, LANES), I32),
                        pltpu.VMEM((2, tq, LANES), I32),
                        pltpu.VMEM((2, tq, LANES), F32)],
        compiler_params=_cp("parallel", "arbitrary"),
        name="a_select",
    )(qi, wi, ki)


def _attn_kernel(qt_ref, kt_ref, tab_ref, q_ref, k_ref, v_ref, msk_ref, bias_ref, o_ref,
                 m_sc, l_sc, acc_sc, mk_sc, *slots, t, nd_far, hpg):
    step = pl.program_id(1)
    qi = qt_ref[step]
    ki = kt_ref[step]

    @pl.when(ki == 0)
    def _():
        m_sc[...] = jnp.full(m_sc.shape, NEG, F32)
        l_sc[...] = jnp.zeros_like(l_sc)
        acc_sc[...] = jnp.zeros_like(acc_sc)

    mk_sc[...] = jnp.concatenate([msk_ref[0, j] for j in range(t // LANES)], axis=1).astype(F32)
    ns = 2 * hpg
    s_sc, p_sc, al_sc = slots[:ns], slots[ns:2 * ns], slots[2 * ns:]

    def slot(h):
        return (h // hpg % 2) * hpg + h % hpg

    def scores(h, far):
        s = lax.dot_general(q_ref[h, 0], k_ref[h, 0], _NT, preferred_element_type=F32) + mk_sc[...]
        if not far:
            s = s + bias_ref[0, h].astype(F32)
        s_sc[slot(h)][...] = s

    def softmax(h, far):
        s = s_sc[slot(h)][...]
        c = tab_ref[REL_BUCKETS - 1, h] * LOG2E if far else 0.0
        m_prev = m_sc[h]
        m_cur = jnp.broadcast_to(jnp.max(s, axis=-1, keepdims=True), (t, LANES))
        m_new = jnp.maximum(m_prev, m_cur + c)
        alpha = jnp.exp2(m_prev - m_new)
        p = jnp.exp2(s - jnp.concatenate([m_new - c] * (t // LANES), axis=1))
        p_part = p[:, :LANES]
        for j in range(1, t // LANES):
            p_part = p_part + p[:, j * LANES:(j + 1) * LANES]
        l_sc[h] = alpha * l_sc[h] + p_part
        m_sc[h] = m_new
        al_sc[slot(h)][...] = alpha
        p_sc[slot(h)][...] = p.astype(BF16)

    def values(h):
        acc_sc[h] = al_sc[slot(h)][...] * acc_sc[h] + jnp.dot(p_sc[slot(h)][...], v_ref[h, 0],
                                                            preferred_element_type=F32)

    one = qt_ref[0] + 1

    def heads(far):
        n_grp = A_HEADS // hpg
        for st in range(n_grp + 2):
            def stage(i, carry, st=st):
                for j in range(hpg):
                    if st < n_grp:
                        scores(st * hpg + j, far)
                    if 1 <= st <= n_grp:
                        softmax((st - 1) * hpg + j, far)
                    if st >= 2:
                        values((st - 2) * hpg + j)
                return carry

            lax.fori_loop(0, one, stage, 0)

    @pl.when(qi - ki >= nd_far)
    def _():
        heads(True)

    @pl.when(qi - ki < nd_far)
    def _():
        heads(False)

    @pl.when(ki == qi)
    def _():
        for h in range(A_HEADS):
            l = jnp.sum(l_sc[h], axis=-1, keepdims=True)
            o_ref[h, 0] = (acc_sc[h] * (1.0 / l)).astype(o_ref.dtype)


def _a_attention(qkv, mask4, bias, rel_table, *, t, nd_far, hpg=4):
    _, bsz, s_len, e = qkv.shape
    nq = s_len // t
    nd = bias.shape[0]
    qt = np.concatenate([np.full(i + 1, i, np.int32) for i in range(nq)])
    kt = np.concatenate([np.arange(i + 1, dtype=np.int32) for i in range(nq)])
    spt = t // LANES
    grid_spec = pltpu.PrefetchScalarGridSpec(
        num_scalar_prefetch=2,
        grid=(bsz, len(qt)),
        in_specs=[
            pl.BlockSpec(memory_space=pltpu.SMEM),
            pl.BlockSpec((A_HEADS, 1, t, e), lambda b, s, qt, kt: (0, b, qt[s], 0)),
            pl.BlockSpec((A_HEADS, 1, t, e), lambda b, s, qt, kt: (1, b, kt[s], 0)),
            pl.BlockSpec((A_HEADS, 1, t, e), lambda b, s, qt, kt: (2, b, kt[s], 0)),
            pl.BlockSpec((1, spt, t, LANES), lambda b, s, qt, kt: (b, kt[s], qt[s], 0)),
            pl.BlockSpec((1, A_HEADS, t, t),
                         lambda b, s, qt, kt: (jnp.minimum(qt[s] - kt[s], nd - 1), 0, 0, 0)),
        ],
        out_specs=pl.BlockSpec((A_HEADS, 1, t, e), lambda b, s, qt, kt: (0, b, qt[s], 0)),
        scratch_shapes=[pltpu.VMEM((A_HEADS, t, LANES), F32),
                        pltpu.VMEM((A_HEADS, t, LANES), F32),
                        pltpu.VMEM((A_HEADS, t, e), F32),
                        pltpu.VMEM((t, t), F32)]
        + [pltpu.VMEM((t, t), F32)] * (2 * hpg)
        + [pltpu.VMEM((t, t), BF16)] * (2 * hpg)
        + [pltpu.VMEM((t, LANES), F32)] * (2 * hpg),
    )
    return pl.pallas_call(
        functools.partial(_attn_kernel, t=t, nd_far=nd_far, hpg=hpg),
        out_shape=jax.ShapeDtypeStruct((A_HEADS, bsz, s_len, e), BF16),
        grid_spec=grid_spec,
        compiler_params=_cp("parallel", "arbitrary"),
        name="a_attention",
    )(jnp.asarray(qt), jnp.asarray(kt), rel_table, qkv, qkv, qkv, mask4, bias)


def _mixer_a(h, g, w_in, w_out, rel_table, bsz, s_len):
    n = h.shape[0]
    d = D_MODEL
    n_idx = IDX_HEADS * IDX_DIM + IDX_DIM + IDX_HEADS
    w_qkv = jnp.concatenate([w_in[:, :d] * (A_HEAD_DIM ** -0.5 * LOG2E), w_in[:, d:3 * d]], axis=1).astype(BF16)
    w_idx = jnp.pad(w_in[:, 3 * d:], ((0, 0), (0, 640 - n_idx))).astype(BF16)
    qkv = _norm_matmul(h, g, w_qkv, out_dtype=BF16, tn=1024, head_major=True, name="a_qkv")
    idx = _norm_matmul(h, g, w_idx, out_dtype=F32, tn=640, name="a_idx")
    qkv = qkv.reshape(3 * A_HEADS, bsz, s_len, A_HEAD_DIM)
    idx = idx.reshape(bsz, s_len, 640)
    nqi = IDX_HEADS * IDX_DIM
    qi = idx[..., :nqi].astype(BF16)
    ki = idx[..., nqi:nqi + IDX_DIM].astype(BF16)
    wi = idx[..., nqi + IDX_DIM:n_idx]
    mask4 = _a_select(qi, wi, ki)

    t = min(512, s_len)
    nd_far = 0
    while _rel_bucket_np(np.array(nd_far * t - (t - 1))) < REL_BUCKETS - 1:
        nd_far += 1
    nd = min(nd_far, s_len // t)
    ii = np.arange(t)[:, None] - np.arange(t)[None, :]
    dist = (np.arange(nd)[:, None, None] * t + ii[None]).reshape(nd * t, t)
    bias = _bias_tiles(rel_table, _rel_bucket_np(dist), BF16, tile=t, tr=t // 2, scale=LOG2E)
    o = _a_attention(qkv, mask4, bias, rel_table, t=t, nd_far=nd_far)
    return _out_proj(o.reshape(A_HEADS, n, A_HEAD_DIM), w_out.astype(BF16), h, name="a_out")


def _bsp_kernel(u_ref, v_ref, lg_ref, lb_ref, wsp_ref, bsp_ref, o_ref, wt_ref, *, rows):
    @pl.when(pl.program_id(0) == 0)
    def _():
        r = lax.broadcasted_iota(I32, (B_CHUNK, B_CHUNK), 0)
        c = lax.broadcasted_iota(I32, (B_CHUNK, B_CHUNK), 1)
        for g in range(B_GROUPS):
            wt_ref[g] = jnp.where(r >= c, wsp_ref[g], 0.0).astype(BF16)

    vn = _layernorm(v_ref[...].astype(F32), lg_ref[...], lb_ref[...]).astype(BF16)
    gw = B_HALF // B_GROUPS
    for c in range(rows // B_CHUNK):
        rs = slice(c * B_CHUNK, (c + 1) * B_CHUNK)
        for g in range(B_GROUPS):
            cs = slice(g * gw, (g + 1) * gw)
            sv = jnp.dot(wt_ref[g], vn[rs, cs], preferred_element_type=F32) + bsp_ref[g]
            o_ref[rs, cs] = (u_ref[rs, cs].astype(F32) * sv).astype(o_ref.dtype)


def _b_spatial(z, ln_g, ln_b, w_sp, b_sp, *, rows=512):
    n = z.shape[0]
    return pl.pallas_call(
        functools.partial(_bsp_kernel, rows=rows),
        out_shape=jax.ShapeDtypeStruct((n, B_HALF), BF16),
        grid=(n // rows,),
        in_specs=[pl.BlockSpec((rows, B_HALF), lambda i: (i, 0)),
                  pl.BlockSpec((rows, B_HALF), lambda i: (i, 1)),
                  pl.BlockSpec((1, B_HALF), lambda i: (0, 0)),
                  pl.BlockSpec((1, B_HALF), lambda i: (0, 0)),
                  pl.BlockSpec((B_GROUPS, B_CHUNK, B_CHUNK), lambda i: (0, 0, 0)),
                  pl.BlockSpec((B_GROUPS, B_CHUNK, 1), lambda i: (0, 0, 0))],
        out_specs=pl.BlockSpec((rows, B_HALF), lambda i: (i, 0)),
        scratch_shapes=[pltpu.VMEM((B_GROUPS, B_CHUNK, B_CHUNK), BF16)],
        compiler_params=_cp("arbitrary"),
        name="b_spatial",
    )(z, z, ln_g.reshape(1, B_HALF), ln_b.reshape(1, B_HALF), w_sp, b_sp.reshape(B_GROUPS, B_CHUNK, 1))


def _mixer_b(h, g, w_in, b_in, ln_g, ln_b, w_sp, b_sp, w_out):
    z = _norm_matmul(h, g, w_in.astype(BF16), b_in, mode="gelu", out_dtype=BF16, tn=1024, name="b_in")
    y = _b_spatial(z, ln_g, ln_b, w_sp, b_sp)
    return _out_proj(y, w_out.astype(BF16), h, name="b_out")


C_HALO = 32


def _conv_kernel(cur_ref, halo_ref, wdw_ref, bdw_ref, lg_ref, lb_ref, w2_ref, b2_ref, r_ref, o_ref,
                 ext_ref, yc_ref, *, ts, rt, ct):
    i = pl.program_id(1)
    ext_ref[C_HALO:, :] = cur_ref[0]

    @pl.when(i == 0)
    def _():
        ext_ref[:C_HALO, :] = jnp.zeros((C_HALO, D_MODEL), F32)

    @pl.when(i > 0)
    def _():
        ext_ref[:C_HALO, :] = halo_ref[0]

    off = C_HALO - (C_KERNEL - 1)

    def tile(it, carry):
        r0 = pl.multiple_of((it // (D_MODEL // ct)) * rt, rt)
        c0 = pl.multiple_of((it % (D_MODEL // ct)) * ct, ct)
        nwin = rt + C_HALO
        win = ext_ref[pl.ds(r0, nwin), pl.ds(c0, ct)]
        acc = jnp.zeros((rt, ct), F32) + bdw_ref[:, pl.ds(c0, ct)]
        for b in range(8):
            rolled = pltpu.roll(win, nwin - (off + b), axis=0)
            for a in range((C_KERNEL - b + 7) // 8):
                j = 8 * a + b
                acc = acc + rolled[8 * a:8 * a + rt] * wdw_ref[j:j + 1, pl.ds(c0, ct)]
        yc_ref[pl.ds(r0, rt), pl.ds(c0, ct)] = acc
        return carry

    lax.fori_loop(0, (ts // rt) * (D_MODEL // ct), tile, 0)
    y = _layernorm(yc_ref[...], lg_ref[...], lb_ref[...])
    y = (y * jax.nn.sigmoid(y)).astype(BF16)
    o_ref[0] = r_ref[0] + jnp.dot(y, w2_ref[...], preferred_element_type=F32) + b2_ref[...]


def _c_conv(y1, w_dw, b_dw, ln_g, ln_b, w2, b2, resid, *, ts=512, rt=32, ct=512):
    bsz, s_len, d = y1.shape
    hb = ts // C_HALO
    vec = lambda a: a.reshape(1, d)
    return pl.pallas_call(
        functools.partial(_conv_kernel, ts=ts, rt=rt, ct=ct),
        out_shape=jax.ShapeDtypeStruct((bsz, s_len, d), F32),
        grid=(bsz, s_len // ts),
        in_specs=[pl.BlockSpec((1, ts, d), lambda b, i: (b, i, 0)),
                  pl.BlockSpec((1, C_HALO, d), lambda b, i: (b, jnp.maximum(i * hb - 1, 0), 0)),
                  pl.BlockSpec((C_KERNEL, d), lambda b, i: (0, 0)),
                  pl.BlockSpec((1, d), lambda b, i: (0, 0)),
                  pl.BlockSpec((1, d), lambda b, i: (0, 0)),
                  pl.BlockSpec((1, d), lambda b, i: (0, 0)),
                  pl.BlockSpec((d, d), lambda b, i: (0, 0)),
                  pl.BlockSpec((1, d), lambda b, i: (0, 0)),
                  pl.BlockSpec((1, ts, d), lambda b, i: (b, i, 0))],
        out_specs=pl.BlockSpec((1, ts, d), lambda b, i: (b, i, 0)),
        scratch_shapes=[pltpu.VMEM((ts + C_HALO, d), F32), pltpu.VMEM((ts, d), F32)],
        compiler_params=_cp("parallel", "parallel"),
        name="c_conv",
    )(y1, y1, w_dw, vec(b_dw), vec(ln_g), vec(ln_b), w2, vec(b2), resid)


def _mixer_c(h, g, w_pw1, b_pw1, w_dw, b_dw, ln_g, ln_b, w_pw2, b_pw2, bsz, s_len):
    n, d = h.shape
    y1 = _norm_matmul(h, g, w_pw1.astype(BF16), b_pw1, mode="glu", out_dtype=F32, tn=512, name="c_pw1")
    out = _c_conv(y1.reshape(bsz, s_len, d), w_dw, b_dw, ln_g, ln_b, w_pw2.astype(BF16), b_pw2,
                  h.reshape(bsz, s_len, d))
    return out.reshape(n, d)


def _dattn_kernel(q_ref, kc_ref, kp_ref, vc_ref, vp_ref, bias_ref, o_ref, lse_ref):
    nb = pl.program_id(2)
    first = jnp.where(nb == 0, NEG, 0.0)
    e = D_HEAD_DIM
    col = lax.broadcasted_iota(I32, (D_BLOCK, 2 * D_BLOCK), 1)
    first_mask = jnp.where(col < D_BLOCK, first, 0.0)
    low = lax.broadcasted_iota(I32, (D_BLOCK, LANES), 1) < e
    for hp in range(D_HEADS // 2):
        ps = slice(hp * LANES, (hp + 1) * LANES)
        q2 = q_ref[:, ps]
        kk = jnp.concatenate([kp_ref[:, ps], kc_ref[:, ps]], axis=0)
        vv = jnp.concatenate([vp_ref[:, ps], vc_ref[:, ps]], axis=0)
        outs, lses = [], []
        for hh in range(2):
            qh = jnp.where(low if hh == 0 else ~low, q2, jnp.zeros_like(q2))
            s = lax.dot_general(qh, kk, _NT, preferred_element_type=F32) * (e ** -0.5) + bias_ref[2 * hp + hh]
            s = s + first_mask
            m = jnp.max(s, axis=-1, keepdims=True)
            p = jnp.exp(s - m)
            l = jnp.sum(p, axis=-1, keepdims=True)
            outs.append(jnp.dot(p.astype(BF16), vv, preferred_element_type=F32) / l)
            lses.append(jnp.broadcast_to(m + jnp.log(l), (D_BLOCK, LANES)))
        o_ref[0, :, ps] = jnp.where(low, outs[0], outs[1])
        lse_ref[0, :, ps] = jnp.where(low, lses[0], lses[1])


def _dproj_kernel(x_ref, g_ref, w_ref, o_ref, xs_ref, *, dil):
    span, d = x_ref.shape
    nslab = d // LANES
    for j in range(nslab):
        xs_ref[j] = x_ref[:, j * LANES:(j + 1) * LANES]
    rows = 4 * D_BLOCK
    for c in range(span // rows):
        xr = jnp.concatenate(
            [jnp.concatenate([xs_ref[j, pl.ds(r, D_BLOCK, stride=dil), :] for j in range(nslab)], axis=1)
             for r in range(4 * c, 4 * c + 4)], axis=0)
        y = jnp.dot(_rms_bf16(xr, g_ref[...]), w_ref[...], preferred_element_type=F32)
        o_ref[c * rows:(c + 1) * rows, :] = y.astype(o_ref.dtype)


def _d_proj(h, g, w, dil):
    n, d = h.shape
    m = w.shape[1]
    span = D_BLOCK * dil
    return pl.pallas_call(
        functools.partial(_dproj_kernel, dil=dil),
        out_shape=jax.ShapeDtypeStruct((n, m), BF16),
        grid=(n // span,),
        in_specs=[pl.BlockSpec((span, d), lambda i: (i, 0)),
                  pl.BlockSpec((1, d), lambda i: (0, 0)),
                  pl.BlockSpec((d, m), lambda i: (0, 0))],
        out_specs=pl.BlockSpec((span, m), lambda i: (i, 0)),
        scratch_shapes=[pltpu.VMEM((d // LANES, span, LANES), F32)],
        compiler_params=_cp("parallel"),
        name=f"d_in_dil{dil}",
    )(h, g.reshape(1, d), w)


def _d_group(proj, bias_g, gidx, dil, bsz, s_len):
    n_sub = s_len // dil
    nb = n_sub // D_BLOCK
    w = D_HEADS * D_HEAD_DIM
    blk = (D_BLOCK, w)
    rb = lambda b, r, n: b * (s_len // D_BLOCK) + n * dil + r
    cur = lambda col: (lambda b, r, n: (rb(b, r, n), col))
    prev = lambda col: (lambda b, r, n: (rb(b, r, jnp.maximum(n - 1, 0)), col))
    out_blk = (1, D_BLOCK, w)
    out_sds = jax.ShapeDtypeStruct((bsz, n_sub, dil * w), F32)
    o, lse = pl.pallas_call(
        _dattn_kernel,
        out_shape=(out_sds, out_sds),
        grid=(bsz, dil, nb),
        in_specs=[pl.BlockSpec(blk, cur(0)),
                  pl.BlockSpec(blk, cur(1)),
                  pl.BlockSpec(blk, prev(1)),
                  pl.BlockSpec(blk, cur(2)),
                  pl.BlockSpec(blk, prev(2)),
                  pl.BlockSpec((D_HEADS, D_BLOCK, 2 * D_BLOCK), lambda b, r, n: (0, 0, 0))],
        out_specs=(pl.BlockSpec(out_blk, lambda b, r, n: (b, n, r)),
                   pl.BlockSpec(out_blk, lambda b, r, n: (b, n, r))),
        compiler_params=_cp("parallel", "parallel", "arbitrary"),
        name=f"d_attn_g{gidx}",
    )(proj, proj, proj, proj, proj, bias_g)
    return o.reshape(bsz * s_len, w), lse.reshape(bsz * s_len, w)


def _dmerge_kernel(o0, o1, o2, l0, l1, l2, w_ref, r_ref, out_ref):
    a0, a1, a2 = l0[...], l1[...], l2[...]
    mx = jnp.maximum(jnp.maximum(a0, a1), a2)
    e0, e1, e2 = jnp.exp(a0 - mx), jnp.exp(a1 - mx), jnp.exp(a2 - mx)
    y = (e0 * o0[...] + e1 * o1[...] + e2 * o2[...]) / (e0 + e1 + e2)
    out_ref[...] = r_ref[...] + jnp.dot(y.astype(BF16), w_ref[...], preferred_element_type=F32)


def _d_merge(outs, lses, w_out, resid, *, tm=512):
    n, w = outs[0].shape
    d = resid.shape[1]
    row = pl.BlockSpec((tm, w), lambda i: (i, 0))
    return pl.pallas_call(
        _dmerge_kernel,
        out_shape=jax.ShapeDtypeStruct((n, d), F32),
        grid=(n // tm,),
        in_specs=[row] * 6 + [pl.BlockSpec((w, d), lambda i: (0, 0)),
                              pl.BlockSpec((tm, d), lambda i: (i, 0))],
        out_specs=pl.BlockSpec((tm, d), lambda i: (i, 0)),
        compiler_params=_cp("parallel"),
        name="d_merge",
    )(*outs, *lses, w_out, resid)


def _mixer_d(h, g, w_in, w_out, rel_table, bsz, s_len):
    w = D_HEADS * D_HEAD_DIM
    ng = len(D_PAIRS)
    p_loc = np.arange(D_BLOCK)[:, None]
    j_loc = np.arange(2 * D_BLOCK)[None, :]
    m = p_loc + D_BLOCK - j_loc
    idx = []
    for window, dil in D_PAIRS:
        steps = window // dil
        idx.append(np.where((m >= 0) & (m <= steps), _rel_bucket_np(m * dil), -1))
    bias = _bias_tiles(rel_table, np.concatenate(idx, 0).astype(np.int32), F32, tile=D_BLOCK, tr=D_BLOCK)
    outs, lses = [], []
    for gi, (_, dil) in enumerate(D_PAIRS):
        cols = [w_in[:, (part * ng + gi) * w:(part * ng + gi + 1) * w] for part in range(3)]
        w_g = jnp.concatenate(cols, axis=1).astype(BF16)
        if dil == 1:
            proj = _norm_matmul(h, g, w_g, out_dtype=BF16, tn=3 * w, name="d_in_dil1")
        else:
            proj = _d_proj(h, g, w_g, dil)
        o, lse = _d_group(proj, bias[gi], gi, dil, bsz, s_len)
        outs.append(o)
        lses.append(lse)
    return _d_merge(outs, lses, w_out.astype(BF16), h)


def _fnorm_kernel(x_ref, g_ref, o_ref):
    x = x_ref[...]
    ms = jnp.mean(x * x, axis=-1, keepdims=True)
    o_ref[...] = x * lax.rsqrt(ms + EPS) * g_ref[...]


def _final_norm(h, g, *, tm=1024):
    n, d = h.shape
    return pl.pallas_call(
        _fnorm_kernel,
        out_shape=jax.ShapeDtypeStruct((n, d), F32),
        grid=(n // tm,),
        in_specs=[pl.BlockSpec((tm, d), lambda i: (i, 0)), pl.BlockSpec((1, d), lambda i: (0, 0))],
        out_specs=pl.BlockSpec((tm, d), lambda i: (i, 0)),
        compiler_params=_cp("parallel"),
        name="final_norm",
    )(h, g.reshape(1, d))


def kernel(x, norm_g, final_g, ffn_w_in, ffn_w_out, rel_table, a_w_in, a_w_out, b_w_in, b_b_in, b_ln_g, b_ln_b, b_w_sp, b_b_sp, b_w_out, c_w_pw1, c_b_pw1, c_w_dw, c_b_dw, c_ln_g, c_ln_b, c_w_pw2, c_b_pw2, d_w_in, d_w_out):
    bsz, s_len, d = x.shape
    depth = norm_g.shape[0]
    h = x.reshape(bsz * s_len, d)
    for i in range(depth):
        kind, j = i % 4, i // 4
        h = _ffn(h, norm_g[i, 0], ffn_w_in[i, 0].astype(BF16), ffn_w_out[i, 0].astype(BF16))
        g = norm_g[i, 1]
        if kind == 0:
            h = _mixer_a(h, g, a_w_in[j], a_w_out[j], rel_table, bsz, s_len)
        elif kind == 1:
            h = _mixer_b(h, g, b_w_in[j], b_b_in[j], b_ln_g[j], b_ln_b[j], b_w_sp[j], b_b_sp[j], b_w_out[j])
        elif kind == 2:
            h = _mixer_c(h, g, c_w_pw1[j], c_b_pw1[j], c_w_dw[j], c_b_dw[j], c_ln_g[j], c_ln_b[j],
                         c_w_pw2[j], c_b_pw2[j], bsz, s_len)
        else:
            h = _mixer_d(h, g, d_w_in[j], d_w_out[j], rel_table, bsz, s_len)
        h = _ffn(h, norm_g[i, 2], ffn_w_in[i, 1].astype(BF16), ffn_w_out[i, 1].astype(BF16))
    return _final_norm(h, final_g).reshape(bsz, s_len, d)
```

```python
import functools
import math

import numpy as np
import jax
import jax.numpy as jnp
from jax import lax
from jax.experimental import pallas as pl
from jax.experimental.pallas import tpu as pltpu

F32, BF16, I32 = jnp.float32, jnp.bfloat16, jnp.int32

D_MODEL = 1024
EPS = 1e-6
D_FF = 2816
REL_BUCKETS = 32
REL_MAX_DIST = 2048
A_HEADS = 8
A_HEAD_DIM = 128
IDX_HEADS = 8
IDX_DIM = 64
TOPK_MAX = 256
B_CHUNK = 128
B_HALF = 3 * D_MODEL
B_GROUPS = 8
C_KERNEL = 31
D_PAIRS = ((128, 1), (512, 4), (2048, 16))
D_HEADS = 8
D_HEAD_DIM = 64
D_BLOCK = 128

LANES = 128
MXU_TILE = 256
VMEM_LIMIT = 56 * 1024 * 1024
NEG = -1e30
INT_MIN = -2 ** 31
LOG2E = math.log2(math.e)

_NT = (((1,), (1,)), ((), ()))


def _cp(*sem, flags=None):
    return pltpu.CompilerParams(dimension_semantics=sem, vmem_limit_bytes=VMEM_LIMIT, flags=flags)


def _rms_bf16(x, g):
    ms = jnp.mean(x * x, axis=-1, keepdims=True)
    return (x * lax.rsqrt(ms + EPS) * g).astype(BF16)


def _layernorm(x, g, b):
    mu = jnp.mean(x, axis=-1, keepdims=True)
    xc = x - mu
    var = jnp.mean(xc * xc, axis=-1, keepdims=True)
    return xc * lax.rsqrt(var + EPS) * g + b


def _ffn_kernel(x_ref, xnext_ref, g_ref, wi_ref, wo_ref, *refs, chunks):
    o_ref, xn_ref = refs[-2], refs[-1]
    i = pl.program_id(0)
    slot = i % 2

    @pl.when(i == 0)
    def _():
        xn_ref[0] = _rms_bf16(x_ref[...], g_ref[...])

    xn = xn_ref[slot]
    acc = None
    c0 = 0
    for tf in chunks:
        gate = jnp.dot(xn, wi_ref[:, c0:c0 + tf], preferred_element_type=F32)
        up = jnp.dot(xn, wi_ref[:, D_FF + c0:D_FF + c0 + tf], preferred_element_type=F32)
        a = (gate * jax.nn.sigmoid(gate) * up).astype(BF16)
        part = jnp.dot(a, wo_ref[c0:c0 + tf, :], preferred_element_type=F32)
        acc = part if acc is None else acc + part
        c0 += tf
    out = x_ref[...] + 0.5 * acc
    if len(refs) == 3:
        ms = jnp.mean(out * out, axis=-1, keepdims=True)
        out = out * lax.rsqrt(ms + EPS) * refs[0][...]
    o_ref[...] = out
    xn_ref[1 - slot] = _rms_bf16(xnext_ref[...], g_ref[...])


def _ffn(h, g, w_in, w_out, final_g=None, *, tm=512):
    n, d = h.shape
    last = n // tm - 1
    n_tiles = D_FF // MXU_TILE
    chunks = ((n_tiles + 1) // 2 * MXU_TILE, n_tiles // 2 * MXU_TILE)
    resident = dict(pipeline_mode=pl.Buffered(1))
    vec = pl.BlockSpec((1, d), lambda i: (0, 0))
    in_specs = [pl.BlockSpec((tm, d), lambda i: (i, 0)),
                pl.BlockSpec((tm, d), lambda i: (jnp.minimum(i + 1, last), 0)),
                vec,
                pl.BlockSpec((d, 2 * D_FF), lambda i: (0, 0), **resident),
                pl.BlockSpec((D_FF, d), lambda i: (0, 0), **resident)]
    args = [h, h, g.reshape(1, d), w_in, w_out]
    if final_g is not None:
        in_specs.append(vec)
        args.append(final_g.reshape(1, d))
    return pl.pallas_call(
        functools.partial(_ffn_kernel, chunks=chunks),
        out_shape=jax.ShapeDtypeStruct((n, d), F32),
        grid=(n // tm,),
        in_specs=in_specs,
        out_specs=pl.BlockSpec((tm, d), lambda i: (i, 0)),
        scratch_shapes=[pltpu.VMEM((2, tm, d), BF16)],
        compiler_params=_cp("arbitrary"),
        name="ffn",
    )(*args)


def _nm_kernel(mode, has_bias, tn, x_ref, g_ref, w_ref, *refs):
    b_ref = refs[0] if has_bias else None
    o_ref, xn_ref = refs[-2], refs[-1]
    s = pl.program_id(0)
    head_major = len(o_ref.shape) == 3
    m_out = o_ref.shape[0] * LANES if head_major else o_ref.shape[1]

    @pl.when(s == 0)
    def _():
        xn_ref[0] = _rms_bf16(x_ref[...], g_ref[...])

    @pl.when(s > 0)
    def _():
        xn = xn_ref[(s - 1) % 2]

        def column_block(c0):
            y = jnp.dot(xn, w_ref[:, c0:c0 + tn], preferred_element_type=F32)
            return y + b_ref[:, c0:c0 + tn] if has_bias else y

        for j in range(m_out // tn):
            y = column_block(j * tn)
            if mode == "gelu":
                y = jax.nn.gelu(y)
            elif mode == "glu":
                y = y * jax.nn.sigmoid(column_block(m_out + j * tn))
            if head_major:
                for hh in range(tn // LANES):
                    o_ref[j * (tn // LANES) + hh] = y[:, hh * LANES:(hh + 1) * LANES].astype(o_ref.dtype)
            else:
                o_ref[:, j * tn:(j + 1) * tn] = y.astype(o_ref.dtype)
        xn_ref[s % 2] = _rms_bf16(x_ref[...], g_ref[...])


def _norm_matmul(h, g, w, b=None, *, mode="plain", out_dtype=BF16, tm=512, tn=512, head_major=False, name="proj"):
    n, d = h.shape
    m_total = w.shape[1]
    m_out = m_total // 2 if mode == "glu" else m_total
    nt = n // tm
    has_bias = b is not None
    prev = lambda s: jnp.maximum(s - 1, 0)
    if head_major:
        out_shape = jax.ShapeDtypeStruct((m_out // LANES, n, LANES), out_dtype)
        out_spec = pl.BlockSpec((m_out // LANES, tm, LANES), lambda s: (0, prev(s), 0))
    else:
        out_shape = jax.ShapeDtypeStruct((n, m_out), out_dtype)
        out_spec = pl.BlockSpec((tm, m_out), lambda s: (prev(s), 0))
    resident = dict(pipeline_mode=pl.Buffered(1))
    in_specs = [pl.BlockSpec((tm, d), lambda s: (jnp.minimum(s, nt - 1), 0)),
                pl.BlockSpec((1, d), lambda s: (0, 0)),
                pl.BlockSpec((d, m_total), lambda s: (0, 0), **resident)]
    args = [h, g.reshape(1, d), w]
    if has_bias:
        in_specs.append(pl.BlockSpec((1, m_total), lambda s: (0, 0)))
        args.append(b.reshape(1, m_total))
    return pl.pallas_call(
        functools.partial(_nm_kernel, mode, has_bias, tn),
        out_shape=out_shape,
        grid=(nt + 1,),
        in_specs=in_specs,
        out_specs=out_spec,
        scratch_shapes=[pltpu.VMEM((2, tm, d), BF16)],
        compiler_params=_cp("arbitrary"),
        name=name,
    )(*args)


def _op_kernel(has_bias, y_ref, w_ref, *refs):
    if has_bias:
        b_ref, r_ref, o_ref = refs
    else:
        r_ref, o_ref = refs
    if len(y_ref.shape) == 3:
        y = jnp.concatenate([y_ref[hh] for hh in range(y_ref.shape[0])], axis=1)
    else:
        y = y_ref[...]
    acc = jnp.dot(y, w_ref[...], preferred_element_type=F32)
    if has_bias:
        acc = acc + b_ref[...]
    o_ref[...] = r_ref[...] + acc


def _out_proj(y, w, resid, b=None, *, tm=512, name="out_proj"):
    k, d = w.shape
    n = resid.shape[0]
    has_bias = b is not None
    if y.ndim == 3:
        y_spec = pl.BlockSpec((k // LANES, tm, LANES), lambda i: (0, i, 0))
    else:
        y_spec = pl.BlockSpec((tm, k), lambda i: (i, 0))
    in_specs = [y_spec, pl.BlockSpec((k, d), lambda i: (0, 0))]
    args = [y, w]
    if has_bias:
        in_specs.append(pl.BlockSpec((1, d), lambda i: (0, 0)))
        args.append(b.reshape(1, d))
    in_specs.append(pl.BlockSpec((tm, d), lambda i: (i, 0)))
    args.append(resid)
    return pl.pallas_call(
        functools.partial(_op_kernel, has_bias),
        out_shape=jax.ShapeDtypeStruct((n, d), F32),
        grid=(n // tm,),
        in_specs=in_specs,
        out_specs=pl.BlockSpec((tm, d), lambda i: (i, 0)),
        compiler_params=_cp("parallel"),
        name=name,
    )(*args)


def _rel_bucket_np(dist):
    max_exact = REL_BUCKETS // 2
    d = np.maximum(dist, 0)
    df = np.maximum(d, 1).astype(np.float32)
    large = max_exact + (np.log(df / np.float32(max_exact)) / np.float32(math.log(REL_MAX_DIST / max_exact))
                         * np.float32(REL_BUCKETS - max_exact)).astype(np.int32)
    large = np.minimum(large, REL_BUCKETS - 1)
    return np.where(d < max_exact, d, large).astype(np.int32)


def _bias_kernel(scale, tab_ref, idx_ref, o_ref):
    idx = idx_ref[...]
    for h in range(o_ref.shape[1]):
        acc = jnp.zeros(idx.shape, F32)
        for b in range(REL_BUCKETS):
            acc = jnp.where(idx == b, tab_ref[b, h] * scale, acc)
        o_ref[0, h] = jnp.where(idx < 0, NEG, acc).astype(o_ref.dtype)


def _bias_tiles(rel_table, bucket_idx, out_dtype, *, tile, tr, scale=1.0):
    r, c = bucket_idx.shape
    nh = rel_table.shape[1]
    per = tile // tr
    return pl.pallas_call(
        functools.partial(_bias_kernel, scale),
        out_shape=jax.ShapeDtypeStruct((r // tile, nh, tile, c), out_dtype),
        grid=(r // tr,),
        in_specs=[pl.BlockSpec(memory_space=pltpu.SMEM),
                  pl.BlockSpec((tr, c), lambda i: (i, 0))],
        out_specs=pl.BlockSpec((1, nh, tr, c), lambda i: (i // per, 0, i % per, 0)),
        compiler_params=_cp("parallel"),
        name="bias_tiles",
    )(rel_table, jnp.asarray(bucket_idx))


def _asel_kernel(qi_ref, side_ref, kall_ref, m_ref, keys_ref, wb_ref, x_ref, si_ref, sf_ref, qb_ref,
                 *, tq, cw, s_len, topk):
    qb = pl.program_id(1)
    spc = cw // LANES
    n_chunks = (qb * tq + tq + cw - 1) // cw
    n_slabs = n_chunks * spc
    nslab_all = s_len // LANES

    qb_ref[...] = qi_ref[0].astype(BF16)
    wi = side_ref[0, :, IDX_DIM:IDX_DIM + IDX_HEADS] * (IDX_HEADS ** -0.5 * IDX_DIM ** -0.5)
    for h in range(IDX_HEADS):
        wb_ref[h] = jnp.broadcast_to(wi[:, h:h + 1], (tq, LANES))
    q_pos = qb * tq + lax.broadcasted_iota(I32, (tq, cw), 0)
    lane_pos = lax.broadcasted_iota(I32, (tq, cw), 1)

    def to_key(x):
        bits = pltpu.bitcast(x, I32)
        return bits ^ ((bits >> 31) & 0x7FFFFFFF)

    def build(c, mx):
        k0 = pl.multiple_of(c * cw, cw)
        kc = kall_ref[0, pl.ds(k0, cw), :][:, :IDX_DIM].astype(BF16)
        sc = jnp.zeros((tq, cw), F32)
        for h in range(IDX_HEADS):
            r = lax.dot_general(qb_ref[:, h * IDX_DIM:(h + 1) * IDX_DIM], kc, _NT,
                                preferred_element_type=F32)
            sc = sc + jnp.maximum(r, 0.0) * jnp.concatenate([wb_ref[h]] * spc, axis=1)
        sc = sc + 0.0
        causal = k0 + lane_pos <= q_pos
        key = jnp.where(causal, to_key(sc), INT_MIN)
        scm = jnp.where(causal, sc, -jnp.inf)
        for j in range(spc):
            keys_ref[c * spc + j] = key[:, j * LANES:(j + 1) * LANES]
            mx = jnp.maximum(mx, scm[:, j * LANES:(j + 1) * LANES])
        return mx

    mx = lax.fori_loop(0, n_chunks, build, jnp.full((tq, LANES), -jnp.inf, F32))
    key_max = to_key(jnp.broadcast_to(jnp.max(mx, axis=1, keepdims=True), (tq, LANES)))

    def lane_sum(acc):
        return jnp.broadcast_to(jnp.sum(acc.astype(F32), axis=1, keepdims=True), (tq, LANES))

    def slab_loop(body, init):
        def chunk(c, carry):
            for j in range(spc):
                carry = body(c * spc + j, carry)
            return carry
        return lax.fori_loop(0, n_chunks, chunk, init)

    zero = jnp.zeros((tq, LANES), I32)

    def count_ge(cand):
        parts = []
        for r0 in range(0, tq, 128):
            rows = slice(r0, r0 + 128)
            cand_r = cand[rows]
            parts.append(slab_loop(lambda c, acc: acc + jnp.where(keys_ref[c, rows, :] >= cand_r, 1, 0), zero[rows]))
        return lane_sum(parts[0] if len(parts) == 1 else jnp.concatenate(parts, axis=0))

    n_real = (q_pos[:, :LANES] + 1).astype(F32)
    lo0 = jnp.full((tq, LANES), INT_MIN + 1, I32)
    si_ref[0] = lo0
    si_ref[1] = jnp.where(n_real > topk, key_max + 1, lo0 + 1)
    sf_ref[0] = n_real
    sf_ref[1] = jnp.zeros((tq, LANES), F32)

    def midpoint(lo, hi):
        return lo + lax.shift_right_logical(hi - lo, 1)

    def search_pass(cand):
        lo, hi, clo, chi = si_ref[0], si_ref[1], sf_ref[0], sf_ref[1]
        cnt = count_ge(cand)
        ge = cnt >= topk
        lo, clo = jnp.where(ge, cand, lo), jnp.where(ge, cnt, clo)
        hi = jnp.where(ge, hi, jnp.maximum(cand, lo + 1))
        hi = jnp.where(clo == topk, lo + 1, hi)
        si_ref[0], si_ref[1] = lo, hi
        sf_ref[0], sf_ref[1] = clo, jnp.where(ge, chi, cnt)
        open_rows = jnp.where((hi - lo) != 1, 1.0, 0.0)
        return jnp.max(open_rows, axis=0, keepdims=True)[0, 0]

    lo, hi = si_ref[0], si_ref[1]
    search_pass(jnp.where(((hi - lo) != 1) & (hi > 1), 1, midpoint(lo, hi)))
    lo, hi = si_ref[0], si_ref[1]
    mid = midpoint(lo, hi)
    probe = hi - (1 << 25)
    cand = jnp.where((lo == 1) & (probe > mid), probe, mid)
    pending = search_pass(jnp.where(((hi - lo) != 1) & (hi == 1), 0, cand))

    def search_body(carry):
        it, _ = carry
        return it + 1, search_pass(midpoint(si_ref[0], si_ref[1]))

    lax.while_loop(lambda carry: (carry[0] < 40) & (carry[1] > 0.0), search_body, (jnp.int32(0), pending))
    tau = si_ref[0]
    clo, chi = sf_ref[0], sf_ref[1]

    r_need = topk - chi
    tied = clo > topk
    x_ref[...] = jnp.full((tq, LANES), s_len, I32)
    lane = lax.broadcasted_iota(I32, (tq, LANES), 1)

    @pl.when(jnp.max(jnp.where(tied, 1.0, 0.0)) > 0.0)
    def _():
        nbits = int(math.log2(s_len)) + 1

        def tie_pass(i, x):
            cand = x + lax.shift_left(jnp.int32(1), nbits - 1 - i)

            def body(c, acc):
                idx = c * LANES + lane
                return acc + jnp.where((keys_ref[c] == tau) & (idx < cand), 1, 0)

            cnt = lane_sum(slab_loop(body, zero))
            return jnp.where(cnt < r_need, cand, x)

        x = lax.fori_loop(0, nbits, tie_pass, zero)
        x_ref[...] = jnp.where(tied, x, s_len)

    x_cut = x_ref[...]

    def emit(c, carry):
        k = keys_ref[c]
        idx = c * LANES + lane
        sel = (k > tau) | ((k == tau) & (idx <= x_cut))
        m_ref[0, c] = jnp.where(sel, 0.0, NEG).astype(m_ref.dtype)
        return carry

    slab_loop(emit, 0)

    def fill(c, carry):
        m_ref[0, c] = jnp.full((tq, LANES), NEG, m_ref.dtype)
        return carry

    lax.fori_loop(n_slabs, nslab_all, fill, 0)


def _a_select(idx, *, tq=128, cw=1024):
    bsz, s_len, width = idx.shape
    nq = IDX_HEADS * IDX_DIM
    side = nq // LANES
    assert width == nq + LANES
    topk = min(TOPK_MAX, s_len // 4)
    nslab = s_len // LANES
    return pl.pallas_call(
        functools.partial(_asel_kernel, tq=tq, cw=cw, s_len=s_len, topk=float(topk)),
        out_shape=jax.ShapeDtypeStruct((bsz, nslab, s_len, LANES), BF16),
        grid=(bsz, s_len // tq),
        in_specs=[pl.BlockSpec((1, tq, nq), lambda b, i: (b, i, 0)),
                  pl.BlockSpec((1, tq, LANES), lambda b, i: (b, i, side)),
                  pl.BlockSpec((1, s_len, LANES), lambda b, i: (b, 0, side), pipeline_mode=pl.Buffered(1))],
        out_specs=pl.BlockSpec((1, nslab, tq, LANES), lambda b, i: (b, 0, i, 0)),
        scratch_shapes=[pltpu.VMEM((nslab, tq, LANES), I32),
                        pltpu.VMEM((IDX_HEADS, tq, LANES), F32),
                        pltpu.VMEM((tq, LANES), I32),
                        pltpu.VMEM((2, tq, LANES), I32),
                        pltpu.VMEM((2, tq, LANES), F32),
                        pltpu.VMEM((tq, nq), BF16)],
        compiler_params=_cp("arbitrary", "arbitrary"),
        name="a_select",
    )(idx, idx, idx)


def _attn_kernel(qt_ref, kt_ref, tab_ref, q_ref, k_ref, v_ref, msk_ref, bias_ref, o_ref,
                 m_sc, l_sc, acc_sc, mk_sc, *slots, t, kpt, nd_far, hpg):
    step = pl.program_id(1)
    qi = qt_ref[step]
    kp = kt_ref[step]
    d_last = qi - (kp * kpt + kpt - 1)
    n_unit = kpt * A_HEADS
    sub = t // LANES

    @pl.when(kp == 0)
    def _():
        m_sc[...] = jnp.full(m_sc.shape, NEG, F32)
        l_sc[...] = jnp.zeros_like(l_sc)
        acc_sc[...] = jnp.zeros_like(acc_sc)

    mk_sc[...] = jnp.concatenate([msk_ref[0, j] for j in range(kpt * sub)], axis=1).astype(F32)
    ns = 2 * hpg
    s_sc, p_sc, al_sc = slots[:ns], slots[ns:2 * ns], slots[2 * ns:]

    def slot(u):
        return (u // hpg % 2) * hpg + u % hpg

    def scores(u, far):
        h, kt = u % A_HEADS, u // A_HEADS
        keys = slice(kt * t, (kt + 1) * t)
        s = lax.dot_general(q_ref[h, 0], k_ref[h, 0, keys, :], _NT, preferred_element_type=F32) + mk_sc[:, keys]
        if not far:
            off0 = sub * jnp.clip(qi - (kp * kpt + kt), 0, nd_far) + (sub - 1)
            s = s + jnp.concatenate(
                [jnp.concatenate([bias_ref[off0 + a - b, h] for b in range(sub)], axis=1) for a in range(sub)],
                axis=0).astype(F32)
        s_sc[slot(u)][...] = s

    def softmax(u, far):
        h = u % A_HEADS
        s = s_sc[slot(u)][...]
        c = tab_ref[REL_BUCKETS - 1, h] * LOG2E if far else 0.0
        m_prev = m_sc[h]
        m_cur = jnp.broadcast_to(jnp.max(s, axis=-1, keepdims=True), (t, LANES))
        m_new = jnp.maximum(m_prev, m_cur + c)
        alpha = jnp.exp2(m_prev - m_new)
        p = jnp.exp2(s - jnp.concatenate([m_new - c] * (t // LANES), axis=1))
        p_part = p[:, :LANES]
        for j in range(1, t // LANES):
            p_part = p_part + p[:, j * LANES:(j + 1) * LANES]
        l_sc[h] = alpha * l_sc[h] + p_part
        m_sc[h] = m_new
        al_sc[slot(u)][...] = alpha
        p_sc[slot(u)][...] = p.astype(BF16)

    def values(u):
        h, kt = u % A_HEADS, u // A_HEADS
        acc_sc[h] = al_sc[slot(u)][...] * acc_sc[h] + jnp.dot(p_sc[slot(u)][...], v_ref[h, 0, kt * t:(kt + 1) * t, :],
                                                            preferred_element_type=F32)

    one = qt_ref[0] + 1

    def units(far):
        n_grp = n_unit // hpg
        for st in range(n_grp + 2):
            def stage(i, carry, st=st):
                for j in range(hpg):
                    if st < n_grp:
                        scores(st * hpg + j, far)
                    if 1 <= st <= n_grp:
                        softmax((st - 1) * hpg + j, far)
                    if st >= 2:
                        values((st - 2) * hpg + j)
                return carry

            lax.fori_loop(0, one, stage, 0)

    @pl.when(d_last >= nd_far)
    def _():
        units(True)

    @pl.when(d_last < nd_far)
    def _():
        units(False)

    @pl.when(kp == qi // kpt)
    def _():
        for h in range(A_HEADS):
            l = jnp.sum(l_sc[h], axis=-1, keepdims=True)
            o_ref[h, 0] = (acc_sc[h] * (1.0 / l)).astype(o_ref.dtype)


def _a_attention(qkv, mask4, bias, rel_table, *, t, kpt, nd_far, hpg=4):
    _, bsz, s_len, e = qkv.shape
    nq = s_len // t
    qt = np.concatenate([np.full(i // kpt + 1, i, np.int32) for i in range(nq)])
    kt = np.concatenate([np.arange(i // kpt + 1, dtype=np.int32) for i in range(nq)])
    spt = t // LANES
    grid_spec = pltpu.PrefetchScalarGridSpec(
        num_scalar_prefetch=2,
        grid=(bsz, len(qt)),
        in_specs=[
            pl.BlockSpec(memory_space=pltpu.SMEM),
            pl.BlockSpec((A_HEADS, 1, t, e), lambda b, s, qt, kt: (0, b, qt[s], 0)),
            pl.BlockSpec((A_HEADS, 1, kpt * t, e), lambda b, s, qt, kt: (1, b, kt[s], 0)),
            pl.BlockSpec((A_HEADS, 1, kpt * t, e), lambda b, s, qt, kt: (2, b, kt[s], 0)),
            pl.BlockSpec((1, kpt * spt, t, LANES), lambda b, s, qt, kt: (b, kt[s], qt[s], 0)),
            pl.BlockSpec(bias.shape, lambda b, s, qt, kt: (0, 0, 0, 0), pipeline_mode=pl.Buffered(1)),
        ],
        out_specs=pl.BlockSpec((A_HEADS, 1, t, e), lambda b, s, qt, kt: (0, b, qt[s], 0)),
        scratch_shapes=[pltpu.VMEM((A_HEADS, t, LANES), F32),
                        pltpu.VMEM((A_HEADS, t, LANES), F32),
                        pltpu.VMEM((A_HEADS, t, e), F32),
                        pltpu.VMEM((t, kpt * t), F32)]
        + [pltpu.VMEM((t, t), F32)] * (2 * hpg)
        + [pltpu.VMEM((t, t), BF16)] * (2 * hpg)
        + [pltpu.VMEM((t, LANES), F32)] * (2 * hpg),
    )
    return pl.pallas_call(
        functools.partial(_attn_kernel, t=t, kpt=kpt, nd_far=nd_far, hpg=hpg),
        out_shape=jax.ShapeDtypeStruct((A_HEADS, bsz, s_len, e), BF16),
        grid_spec=grid_spec,
        compiler_params=_cp("parallel", "arbitrary"),
        name="a_attention",
    )(jnp.asarray(qt), jnp.asarray(kt), rel_table, qkv, qkv, qkv, mask4, bias)


def _mixer_a(h, g, w_in, w_out, rel_table, bsz, s_len):
    n = h.shape[0]
    d = D_MODEL
    n_idx = IDX_HEADS * IDX_DIM + IDX_DIM + IDX_HEADS
    w_qkv = jnp.concatenate([w_in[:, :d] * (A_HEAD_DIM ** -0.5 * LOG2E), w_in[:, d:3 * d]], axis=1).astype(BF16)
    w_idx = jnp.pad(w_in[:, 3 * d:], ((0, 0), (0, 640 - n_idx))).astype(BF16)
    qkv = _norm_matmul(h, g, w_qkv, out_dtype=BF16, tn=1024, head_major=True, name="a_qkv")
    idx = _norm_matmul(h, g, w_idx, out_dtype=F32, tn=640, name="a_idx")
    qkv = qkv.reshape(3 * A_HEADS, bsz, s_len, A_HEAD_DIM)
    mask4 = _a_select(idx.reshape(bsz, s_len, 640))

    t = min(512, s_len)
    nd_far = 0
    while _rel_bucket_np(np.array(nd_far * t - (t - 1))) < REL_BUCKETS - 1:
        nd_far += 1
    sub = t // LANES
    offs = np.arange(-(sub - 1), sub * nd_far + sub)
    ii = np.arange(LANES)[:, None] - np.arange(LANES)[None, :]
    dist = (offs[:, None, None] * LANES + ii[None]).reshape(len(offs) * LANES, LANES)
    bias = _bias_tiles(rel_table, _rel_bucket_np(dist), BF16, tile=LANES, tr=LANES, scale=LOG2E)
    kpt = 2 if (s_len // t) % 2 == 0 else 1
    o = _a_attention(qkv, mask4, bias, rel_table, t=t, kpt=kpt, nd_far=nd_far)
    return _out_proj(o.reshape(A_HEADS, n, A_HEAD_DIM), w_out.astype(BF16), h, name="a_out")


def _bsp_kernel(u_ref, v_ref, lg_ref, lb_ref, wsp_ref, bsp_ref, wo_ref, r_ref, o_ref, wt_ref, y_ref, *, rows):
    @pl.when(pl.program_id(0) == 0)
    def _():
        r = lax.broadcasted_iota(I32, (B_CHUNK, B_CHUNK), 0)
        c = lax.broadcasted_iota(I32, (B_CHUNK, B_CHUNK), 1)
        for g in range(B_GROUPS):
            wt_ref[g] = jnp.where(r >= c, wsp_ref[g], 0.0).astype(BF16)

    vn = _layernorm(v_ref[...].astype(F32), lg_ref[...], lb_ref[...]).astype(BF16)
    gw = B_HALF // B_GROUPS
    for c in range(rows // B_CHUNK):
        rs = slice(c * B_CHUNK, (c + 1) * B_CHUNK)
        for g in range(B_GROUPS):
            cs = slice(g * gw, (g + 1) * gw)
            sv = jnp.dot(wt_ref[g], vn[rs, cs], preferred_element_type=F32) + bsp_ref[g]
            y_ref[rs, cs] = (u_ref[rs, cs].astype(F32) * sv).astype(BF16)
    o_ref[...] = r_ref[...] + jnp.dot(y_ref[...], wo_ref[...], preferred_element_type=F32)


def _b_spatial_out(z, ln_g, ln_b, w_sp, b_sp, w_out, resid, *, rows=512):
    n, d = resid.shape
    return pl.pallas_call(
        functools.partial(_bsp_kernel, rows=rows),
        out_shape=jax.ShapeDtypeStruct((n, d), F32),
        grid=(n // rows,),
        in_specs=[pl.BlockSpec((rows, B_HALF), lambda i: (i, 0)),
                  pl.BlockSpec((rows, B_HALF), lambda i: (i, 1)),
                  pl.BlockSpec((1, B_HALF), lambda i: (0, 0)),
                  pl.BlockSpec((1, B_HALF), lambda i: (0, 0)),
                  pl.BlockSpec((B_GROUPS, B_CHUNK, B_CHUNK), lambda i: (0, 0, 0)),
                  pl.BlockSpec((B_GROUPS, B_CHUNK, 1), lambda i: (0, 0, 0)),
                  pl.BlockSpec((B_HALF, d), lambda i: (0, 0), pipeline_mode=pl.Buffered(1)),
                  pl.BlockSpec((rows, d), lambda i: (i, 0))],
        out_specs=pl.BlockSpec((rows, d), lambda i: (i, 0)),
        scratch_shapes=[pltpu.VMEM((B_GROUPS, B_CHUNK, B_CHUNK), BF16), pltpu.VMEM((rows, B_HALF), BF16)],
        compiler_params=_cp("arbitrary"),
        name="b_spatial_out",
    )(z, z, ln_g.reshape(1, B_HALF), ln_b.reshape(1, B_HALF), w_sp, b_sp.reshape(B_GROUPS, B_CHUNK, 1),
      w_out, resid)


def _mixer_b(h, g, w_in, b_in, ln_g, ln_b, w_sp, b_sp, w_out):
    z = _norm_matmul(h, g, w_in.astype(BF16), b_in, mode="gelu", out_dtype=BF16, tn=1024, name="b_in")
    return _b_spatial_out(z, ln_g, ln_b, w_sp, b_sp, w_out.astype(BF16), h)


C_HALO = 32


def _conv_kernel(cur_ref, halo_ref, wdw_ref, bdw_ref, lg_ref, lb_ref, w2_ref, b2_ref, r_ref, o_ref,
                 ext_ref, yc_ref, *, ts, rt, ct):
    i = pl.program_id(1)
    ext_ref[C_HALO:, :] = cur_ref[0]

    @pl.when(i == 0)
    def _():
        ext_ref[:C_HALO, :] = jnp.zeros((C_HALO, D_MODEL), F32)

    @pl.when(i > 0)
    def _():
        ext_ref[:C_HALO, :] = halo_ref[0]

    off = C_HALO - (C_KERNEL - 1)

    def tile(it, carry):
        r0 = pl.multiple_of((it // (D_MODEL // ct)) * rt, rt)
        c0 = pl.multiple_of((it % (D_MODEL // ct)) * ct, ct)
        nwin = rt + C_HALO
        win = ext_ref[pl.ds(r0, nwin), pl.ds(c0, ct)]
        acc = jnp.zeros((rt, ct), F32) + bdw_ref[:, pl.ds(c0, ct)]
        for b in range(8):
            rolled = pltpu.roll(win, nwin - (off + b), axis=0)
            for a in range((C_KERNEL - b + 7) // 8):
                j = 8 * a + b
                acc = acc + rolled[8 * a:8 * a + rt] * wdw_ref[j:j + 1, pl.ds(c0, ct)]
        yc_ref[pl.ds(r0, rt), pl.ds(c0, ct)] = acc
        return carry

    lax.fori_loop(0, (ts // rt) * (D_MODEL // ct), tile, 0)
    y = _layernorm(yc_ref[...], lg_ref[...], lb_ref[...])
    y = (y * jax.nn.sigmoid(y)).astype(BF16)
    o_ref[0] = r_ref[0] + jnp.dot(y, w2_ref[...], preferred_element_type=F32) + b2_ref[...]


def _c_conv(y1, w_dw, b_dw, ln_g, ln_b, w2, b2, resid, *, ts=512, rt=32, ct=512):
    bsz, s_len, d = y1.shape
    hb = ts // C_HALO
    vec = lambda a: a.reshape(1, d)
    return pl.pallas_call(
        functools.partial(_conv_kernel, ts=ts, rt=rt, ct=ct),
        out_shape=jax.ShapeDtypeStruct((bsz, s_len, d), F32),
        grid=(bsz, s_len // ts),
        in_specs=[pl.BlockSpec((1, ts, d), lambda b, i: (b, i, 0)),
                  pl.BlockSpec((1, C_HALO, d), lambda b, i: (b, jnp.maximum(i * hb - 1, 0), 0)),
                  pl.BlockSpec((C_KERNEL, d), lambda b, i: (0, 0)),
                  pl.BlockSpec((1, d), lambda b, i: (0, 0)),
                  pl.BlockSpec((1, d), lambda b, i: (0, 0)),
                  pl.BlockSpec((1, d), lambda b, i: (0, 0)),
                  pl.BlockSpec((d, d), lambda b, i: (0, 0)),
                  pl.BlockSpec((1, d), lambda b, i: (0, 0)),
                  pl.BlockSpec((1, ts, d), lambda b, i: (b, i, 0))],
        out_specs=pl.BlockSpec((1, ts, d), lambda b, i: (b, i, 0)),
        scratch_shapes=[pltpu.VMEM((ts + C_HALO, d), F32), pltpu.VMEM((ts, d), F32)],
        compiler_params=_cp("parallel", "parallel"),
        name="c_conv",
    )(y1, y1, w_dw, vec(b_dw), vec(ln_g), vec(ln_b), w2, vec(b2), resid)


def _mixer_c(h, g, w_pw1, b_pw1, w_dw, b_dw, ln_g, ln_b, w_pw2, b_pw2, bsz, s_len):
    n, d = h.shape
    y1 = _norm_matmul(h, g, w_pw1.astype(BF16), b_pw1, mode="glu", out_dtype=F32, tn=512, name="c_pw1")
    out = _c_conv(y1.reshape(bsz, s_len, d), w_dw, b_dw, ln_g, ln_b, w_pw2.astype(BF16), b_pw2,
                  h.reshape(bsz, s_len, d))
    return out.reshape(n, d)


def _dattn_kernel(q_ref, kc_ref, kp_ref, vc_ref, vp_ref, bias_ref, o_ref, lse_ref):
    nb = pl.program_id(2)
    first = jnp.where(nb == 0, NEG, 0.0)
    e = D_HEAD_DIM
    col = lax.broadcasted_iota(I32, (D_BLOCK, 2 * D_BLOCK), 1)
    first_mask = jnp.where(col < D_BLOCK, first, 0.0)
    low = lax.broadcasted_iota(I32, (D_BLOCK, LANES), 1) < e
    for hp in range(D_HEADS // 2):
        ps = slice(hp * LANES, (hp + 1) * LANES)
        q2 = q_ref[:, ps]
        kk = jnp.concatenate([kp_ref[:, ps], kc_ref[:, ps]], axis=0)
        vv = jnp.concatenate([vp_ref[:, ps], vc_ref[:, ps]], axis=0)
        outs, lses = [], []
        for hh in range(2):
            qh = jnp.where(low if hh == 0 else ~low, q2, jnp.zeros_like(q2))
            s = lax.dot_general(qh, kk, _NT, preferred_element_type=F32) * (e ** -0.5) + bias_ref[2 * hp + hh]
            s = s + first_mask
            m = jnp.max(s, axis=-1, keepdims=True)
            p = jnp.exp(s - m)
            l = jnp.sum(p, axis=-1, keepdims=True)
            outs.append(jnp.dot(p.astype(BF16), vv, preferred_element_type=F32) / l)
            lses.append(jnp.broadcast_to(m + jnp.log(l), (D_BLOCK, LANES)))
        o_ref[0, :, ps] = jnp.where(low, outs[0], outs[1])
        lse_ref[0, :, ps] = jnp.where(low, lses[0], lses[1])


def _dproj_kernel(x_ref, g_ref, w_ref, o_ref, xs_ref, *, dil):
    span, d = x_ref.shape
    nslab = d // LANES
    for j in range(nslab):
        xs_ref[j] = x_ref[:, j * LANES:(j + 1) * LANES]
    rows = 4 * D_BLOCK
    for c in range(span // rows):
        xr = jnp.concatenate(
            [jnp.concatenate([xs_ref[j, pl.ds(r, D_BLOCK, stride=dil), :] for j in range(nslab)], axis=1)
             for r in range(4 * c, 4 * c + 4)], axis=0)
        y = jnp.dot(_rms_bf16(xr, g_ref[...]), w_ref[...], preferred_element_type=F32)
        o_ref[c * rows:(c + 1) * rows, :] = y.astype(o_ref.dtype)


def _d_proj(h, g, w, dil):
    n, d = h.shape
    m = w.shape[1]
    span = D_BLOCK * dil
    return pl.pallas_call(
        functools.partial(_dproj_kernel, dil=dil),
        out_shape=jax.ShapeDtypeStruct((n, m), BF16),
        grid=(n // span,),
        in_specs=[pl.BlockSpec((span, d), lambda i: (i, 0)),
                  pl.BlockSpec((1, d), lambda i: (0, 0)),
                  pl.BlockSpec((d, m), lambda i: (0, 0))],
        out_specs=pl.BlockSpec((span, m), lambda i: (i, 0)),
        scratch_shapes=[pltpu.VMEM((d // LANES, span, LANES), F32)],
        compiler_params=_cp("parallel"),
        name=f"d_in_dil{dil}",
    )(h, g.reshape(1, d), w)


def _d_group(proj, bias_g, gidx, dil, bsz, s_len):
    n_sub = s_len // dil
    nb = n_sub // D_BLOCK
    w = D_HEADS * D_HEAD_DIM
    blk = (D_BLOCK, w)
    rb = lambda b, r, n: b * (s_len // D_BLOCK) + n * dil + r
    cur = lambda col: (lambda b, r, n: (rb(b, r, n), col))
    prev = lambda col: (lambda b, r, n: (rb(b, r, jnp.maximum(n - 1, 0)), col))
    out_blk = (1, D_BLOCK, w)
    out_sds = jax.ShapeDtypeStruct((bsz, n_sub, dil * w), F32)
    o, lse = pl.pallas_call(
        _dattn_kernel,
        out_shape=(out_sds, out_sds),
        grid=(bsz, dil, nb),
        in_specs=[pl.BlockSpec(blk, cur(0)),
                  pl.BlockSpec(blk, cur(1)),
                  pl.BlockSpec(blk, prev(1)),
                  pl.BlockSpec(blk, cur(2)),
                  pl.BlockSpec(blk, prev(2)),
                  pl.BlockSpec((D_HEADS, D_BLOCK, 2 * D_BLOCK), lambda b, r, n: (0, 0, 0))],
        out_specs=(pl.BlockSpec(out_blk, lambda b, r, n: (b, n, r)),
                   pl.BlockSpec(out_blk, lambda b, r, n: (b, n, r))),
        compiler_params=_cp("parallel", "parallel", "arbitrary"),
        name=f"d_attn_g{gidx}",
    )(proj, proj, proj, proj, proj, bias_g)
    return o.reshape(bsz * s_len, w), lse.reshape(bsz * s_len, w)


def _dmerge_kernel(o0, o1, o2, l0, l1, l2, w_ref, r_ref, out_ref):
    a0, a1, a2 = l0[...], l1[...], l2[...]
    mx = jnp.maximum(jnp.maximum(a0, a1), a2)
    e0, e1, e2 = jnp.exp(a0 - mx), jnp.exp(a1 - mx), jnp.exp(a2 - mx)
    y = (e0 * o0[...] + e1 * o1[...] + e2 * o2[...]) / (e0 + e1 + e2)
    out_ref[...] = r_ref[...] + jnp.dot(y.astype(BF16), w_ref[...], preferred_element_type=F32)


def _d_merge(outs, lses, w_out, resid, *, tm=512):
    n, w = outs[0].shape
    d = resid.shape[1]
    row = pl.BlockSpec((tm, w), lambda i: (i, 0))
    return pl.pallas_call(
        _dmerge_kernel,
        out_shape=jax.ShapeDtypeStruct((n, d), F32),
        grid=(n // tm,),
        in_specs=[row] * 6 + [pl.BlockSpec((w, d), lambda i: (0, 0)),
                              pl.BlockSpec((tm, d), lambda i: (i, 0))],
        out_specs=pl.BlockSpec((tm, d), lambda i: (i, 0)),
        compiler_params=_cp("parallel"),
        name="d_merge",
    )(*outs, *lses, w_out, resid)


def _mixer_d(h, g, w_in, w_out, rel_table, bsz, s_len):
    w = D_HEADS * D_HEAD_DIM
    ng = len(D_PAIRS)
    p_loc = np.arange(D_BLOCK)[:, None]
    j_loc = np.arange(2 * D_BLOCK)[None, :]
    m = p_loc + D_BLOCK - j_loc
    idx = []
    for window, dil in D_PAIRS:
        steps = window // dil
        idx.append(np.where((m >= 0) & (m <= steps), _rel_bucket_np(m * dil), -1))
    bias = _bias_tiles(rel_table, np.concatenate(idx, 0).astype(np.int32), F32, tile=D_BLOCK, tr=D_BLOCK)
    outs, lses = [], []
    for gi, (_, dil) in enumerate(D_PAIRS):
        cols = [w_in[:, (part * ng + gi) * w:(part * ng + gi + 1) * w] for part in range(3)]
        w_g = jnp.concatenate(cols, axis=1).astype(BF16)
        if dil == 1:
            proj = _norm_matmul(h, g, w_g, out_dtype=BF16, tn=3 * w, name="d_in_dil1")
        else:
            proj = _d_proj(h, g, w_g, dil)
        o, lse = _d_group(proj, bias[gi], gi, dil, bsz, s_len)
        outs.append(o)
        lses.append(lse)
    return _d_merge(outs, lses, w_out.astype(BF16), h)


def kernel(x, norm_g, final_g, ffn_w_in, ffn_w_out, rel_table, a_w_in, a_w_out, b_w_in, b_b_in, b_ln_g, b_ln_b, b_w_sp, b_b_sp, b_w_out, c_w_pw1, c_b_pw1, c_w_dw, c_b_dw, c_ln_g, c_ln_b, c_w_pw2, c_b_pw2, d_w_in, d_w_out):
    bsz, s_len, d = x.shape
    depth = norm_g.shape[0]
    h = x.reshape(bsz * s_len, d)
    for i in range(depth):
        kind, j = i % 4, i // 4
        h = _ffn(h, norm_g[i, 0], ffn_w_in[i, 0].astype(BF16), ffn_w_out[i, 0].astype(BF16))
        g = norm_g[i, 1]
        if kind == 0:
            h = _mixer_a(h, g, a_w_in[j], a_w_out[j], rel_table, bsz, s_len)
        elif kind == 1:
            h = _mixer_b(h, g, b_w_in[j], b_b_in[j], b_ln_g[j], b_ln_b[j], b_w_sp[j], b_b_sp[j], b_w_out[j])
        elif kind == 2:
            h = _mixer_c(h, g, c_w_pw1[j], c_b_pw1[j], c_w_dw[j], c_b_dw[j], c_ln_g[j], c_ln_b[j],
                         c_w_pw2[j], c_b_pw2[j], bsz, s_len)
        else:
            h = _mixer_d(h, g, d_w_in[j], d_w_out[j], rel_table, bsz, s_len)
        h = _ffn(h, norm_g[i, 2], ffn_w_in[i, 1].astype(BF16), ffn_w_out[i, 1].astype(BF16),
                 final_g if i == depth - 1 else None)
    return h.reshape(bsz, s_len, d)
```

```python
import functools
import math

import numpy as np
import jax
import jax.numpy as jnp
from jax import lax
from jax.experimental import pallas as pl
from jax.experimental.pallas import tpu as pltpu

F32, BF16, I32 = jnp.float32, jnp.bfloat16, jnp.int32

D_MODEL = 1024
EPS = 1e-6
D_FF = 2816
REL_BUCKETS = 32
REL_MAX_DIST = 2048
A_HEADS = 8
A_HEAD_DIM = 128
IDX_HEADS = 8
IDX_DIM = 64
TOPK_MAX = 256
B_CHUNK = 128
B_HALF = 3 * D_MODEL
B_GROUPS = 8
C_KERNEL = 31
D_PAIRS = ((128, 1), (512, 4), (2048, 16))
D_HEADS = 8
D_HEAD_DIM = 64
D_BLOCK = 128

LANES = 128
MXU_TILE = 256
VMEM_LIMIT = 56 * 1024 * 1024
NEG = -1e30
INT_MIN = -2 ** 31
LOG2E = math.log2(math.e)

_NT = (((1,), (1,)), ((), ()))


def _cp(*sem, flags=None):
    return pltpu.CompilerParams(dimension_semantics=sem, vmem_limit_bytes=VMEM_LIMIT, flags=flags)


def _rms_bf16(x, g):
    ms = jnp.mean(x * x, axis=-1, keepdims=True)
    return (x * lax.rsqrt(ms + EPS) * g).astype(BF16)


def _layernorm(x, g, b):
    mu = jnp.mean(x, axis=-1, keepdims=True)
    xc = x - mu
    var = jnp.mean(xc * xc, axis=-1, keepdims=True)
    return xc * lax.rsqrt(var + EPS) * g + b


def _ffn_kernel(x_ref, xnext_ref, g_ref, wi_ref, wo_ref, *refs, chunks):
    o_ref, xn_ref = refs[-2], refs[-1]
    i = pl.program_id(0)
    slot = i % 2

    @pl.when(i == 0)
    def _():
        xn_ref[0] = _rms_bf16(x_ref[...], g_ref[...])

    xn = xn_ref[slot]
    acc = None
    c0 = 0
    for tf in chunks:
        gate = jnp.dot(xn, wi_ref[:, c0:c0 + tf], preferred_element_type=F32)
        up = jnp.dot(xn, wi_ref[:, D_FF + c0:D_FF + c0 + tf], preferred_element_type=F32)
        a = (gate * jax.nn.sigmoid(gate) * up).astype(BF16)
        part = jnp.dot(a, wo_ref[c0:c0 + tf, :], preferred_element_type=F32)
        acc = part if acc is None else acc + part
        c0 += tf
    out = x_ref[...] + 0.5 * acc
    if len(refs) == 3:
        ms = jnp.mean(out * out, axis=-1, keepdims=True)
        out = out * lax.rsqrt(ms + EPS) * refs[0][...]
    o_ref[...] = out
    xn_ref[1 - slot] = _rms_bf16(xnext_ref[...], g_ref[...])


def _ffn(h, g, w_in, w_out, final_g=None, *, tm=512):
    n, d = h.shape
    last = n // tm - 1
    n_tiles = D_FF // MXU_TILE
    chunks = ((n_tiles + 1) // 2 * MXU_TILE, n_tiles // 2 * MXU_TILE)
    resident = dict(pipeline_mode=pl.Buffered(1))
    vec = pl.BlockSpec((1, d), lambda i: (0, 0))
    in_specs = [pl.BlockSpec((tm, d), lambda i: (i, 0)),
                pl.BlockSpec((tm, d), lambda i: (jnp.minimum(i + 1, last), 0)),
                vec,
                pl.BlockSpec((d, 2 * D_FF), lambda i: (0, 0), **resident),
                pl.BlockSpec((D_FF, d), lambda i: (0, 0), **resident)]
    args = [h, h, g.reshape(1, d), w_in, w_out]
    if final_g is not None:
        in_specs.append(vec)
        args.append(final_g.reshape(1, d))
    return pl.pallas_call(
        functools.partial(_ffn_kernel, chunks=chunks),
        out_shape=jax.ShapeDtypeStruct((n, d), F32),
        grid=(n // tm,),
        in_specs=in_specs,
        out_specs=pl.BlockSpec((tm, d), lambda i: (i, 0)),
        scratch_shapes=[pltpu.VMEM((2, tm, d), BF16)],
        compiler_params=_cp("arbitrary"),
        name="ffn",
    )(*args)


def _nm_kernel(mode, has_bias, tn, x_ref, g_ref, w_ref, *refs):
    b_ref = refs[0] if has_bias else None
    o_ref, xn_ref = refs[-2], refs[-1]
    s = pl.program_id(0)
    head_major = len(o_ref.shape) == 3
    m_out = o_ref.shape[0] * LANES if head_major else o_ref.shape[1]

    @pl.when(s == 0)
    def _():
        xn_ref[0] = _rms_bf16(x_ref[...], g_ref[...])

    @pl.when(s > 0)
    def _():
        xn = xn_ref[(s - 1) % 2]

        def column_block(c0):
            y = jnp.dot(xn, w_ref[:, c0:c0 + tn], preferred_element_type=F32)
            return y + b_ref[:, c0:c0 + tn] if has_bias else y

        for j in range(m_out // tn):
            y = column_block(j * tn)
            if mode == "gelu":
                y = jax.nn.gelu(y)
            elif mode == "glu":
                y = y * jax.nn.sigmoid(column_block(m_out + j * tn))
            if head_major:
                for hh in range(tn // LANES):
                    o_ref[j * (tn // LANES) + hh] = y[:, hh * LANES:(hh + 1) * LANES].astype(o_ref.dtype)
            else:
                o_ref[:, j * tn:(j + 1) * tn] = y.astype(o_ref.dtype)
        xn_ref[s % 2] = _rms_bf16(x_ref[...], g_ref[...])


def _norm_matmul(h, g, w, b=None, *, mode="plain", out_dtype=BF16, tm=512, tn=512, head_major=False, name="proj"):
    n, d = h.shape
    m_total = w.shape[1]
    m_out = m_total // 2 if mode == "glu" else m_total
    nt = n // tm
    has_bias = b is not None
    prev = lambda s: jnp.maximum(s - 1, 0)
    if head_major:
        out_shape = jax.ShapeDtypeStruct((m_out // LANES, n, LANES), out_dtype)
        out_spec = pl.BlockSpec((m_out // LANES, tm, LANES), lambda s: (0, prev(s), 0))
    else:
        out_shape = jax.ShapeDtypeStruct((n, m_out), out_dtype)
        out_spec = pl.BlockSpec((tm, m_out), lambda s: (prev(s), 0))
    resident = dict(pipeline_mode=pl.Buffered(1))
    in_specs = [pl.BlockSpec((tm, d), lambda s: (jnp.minimum(s, nt - 1), 0)),
                pl.BlockSpec((1, d), lambda s: (0, 0)),
                pl.BlockSpec((d, m_total), lambda s: (0, 0), **resident)]
    args = [h, g.reshape(1, d), w]
    if has_bias:
        in_specs.append(pl.BlockSpec((1, m_total), lambda s: (0, 0)))
        args.append(b.reshape(1, m_total))
    return pl.pallas_call(
        functools.partial(_nm_kernel, mode, has_bias, tn),
        out_shape=out_shape,
        grid=(nt + 1,),
        in_specs=in_specs,
        out_specs=out_spec,
        scratch_shapes=[pltpu.VMEM((2, tm, d), BF16)],
        compiler_params=_cp("arbitrary"),
        name=name,
    )(*args)


def _op_kernel(has_bias, y_ref, w_ref, *refs):
    if has_bias:
        b_ref, r_ref, o_ref = refs
    else:
        r_ref, o_ref = refs
    if len(y_ref.shape) == 3:
        y = jnp.concatenate([y_ref[hh] for hh in range(y_ref.shape[0])], axis=1)
    else:
        y = y_ref[...]
    acc = jnp.dot(y, w_ref[...], preferred_element_type=F32)
    if has_bias:
        acc = acc + b_ref[...]
    o_ref[...] = r_ref[...] + acc


def _out_proj(y, w, resid, b=None, *, tm=512, name="out_proj"):
    k, d = w.shape
    n = resid.shape[0]
    has_bias = b is not None
    if y.ndim == 3:
        y_spec = pl.BlockSpec((k // LANES, tm, LANES), lambda i: (0, i, 0))
    else:
        y_spec = pl.BlockSpec((tm, k), lambda i: (i, 0))
    in_specs = [y_spec, pl.BlockSpec((k, d), lambda i: (0, 0))]
    args = [y, w]
    if has_bias:
        in_specs.append(pl.BlockSpec((1, d), lambda i: (0, 0)))
        args.append(b.reshape(1, d))
    in_specs.append(pl.BlockSpec((tm, d), lambda i: (i, 0)))
    args.append(resid)
    return pl.pallas_call(
        functools.partial(_op_kernel, has_bias),
        out_shape=jax.ShapeDtypeStruct((n, d), F32),
        grid=(n // tm,),
        in_specs=in_specs,
        out_specs=pl.BlockSpec((tm, d), lambda i: (i, 0)),
        compiler_params=_cp("parallel"),
        name=name,
    )(*args)


def _rel_bucket_np(dist):
    max_exact = REL_BUCKETS // 2
    d = np.maximum(dist, 0)
    df = np.maximum(d, 1).astype(np.float32)
    large = max_exact + (np.log(df / np.float32(max_exact)) / np.float32(math.log(REL_MAX_DIST / max_exact))
                         * np.float32(REL_BUCKETS - max_exact)).astype(np.int32)
    large = np.minimum(large, REL_BUCKETS - 1)
    return np.where(d < max_exact, d, large).astype(np.int32)


def _bias_kernel(scale, tab_ref, idx_ref, o_ref):
    idx = idx_ref[...]
    for h in range(o_ref.shape[1]):
        acc = jnp.zeros(idx.shape, F32)
        for b in range(REL_BUCKETS):
            acc = jnp.where(idx == b, tab_ref[b, h] * scale, acc)
        o_ref[0, h] = jnp.where(idx < 0, NEG, acc).astype(o_ref.dtype)


def _bias_tiles(rel_table, bucket_idx, out_dtype, *, tile, tr, scale=1.0):
    r, c = bucket_idx.shape
    nh = rel_table.shape[1]
    per = tile // tr
    return pl.pallas_call(
        functools.partial(_bias_kernel, scale),
        out_shape=jax.ShapeDtypeStruct((r // tile, nh, tile, c), out_dtype),
        grid=(r // tr,),
        in_specs=[pl.BlockSpec(memory_space=pltpu.SMEM),
                  pl.BlockSpec((tr, c), lambda i: (i, 0))],
        out_specs=pl.BlockSpec((1, nh, tr, c), lambda i: (i // per, 0, i % per, 0)),
        compiler_params=_cp("parallel"),
        name="bias_tiles",
    )(rel_table, jnp.asarray(bucket_idx))


def _asel_kernel(qi_ref, side_ref, kall_ref, m_ref, keys_ref, wb_ref, x_ref, si_ref, sf_ref, qb_ref,
                 *, tq, cw, s_len, topk):
    qb = pl.program_id(1)
    spc = cw // LANES
    n_chunks = (qb * tq + tq + cw - 1) // cw
    n_slabs = n_chunks * spc
    nslab_all = s_len // LANES

    qb_ref[...] = qi_ref[0].astype(BF16)
    wi = side_ref[0, :, IDX_DIM:IDX_DIM + IDX_HEADS] * (IDX_HEADS ** -0.5 * IDX_DIM ** -0.5)
    for h in range(IDX_HEADS):
        wb_ref[h] = jnp.broadcast_to(wi[:, h:h + 1], (tq, LANES))
    q_pos = qb * tq + lax.broadcasted_iota(I32, (tq, cw), 0)
    lane_pos = lax.broadcasted_iota(I32, (tq, cw), 1)

    def to_key(x):
        bits = pltpu.bitcast(x, I32)
        return bits ^ ((bits >> 31) & 0x7FFFFFFF)

    def build(c, mx):
        k0 = pl.multiple_of(c * cw, cw)
        kc = kall_ref[0, pl.ds(k0, cw), :][:, :IDX_DIM].astype(BF16)
        sc = jnp.zeros((tq, cw), F32)
        for h in range(IDX_HEADS):
            r = lax.dot_general(qb_ref[:, h * IDX_DIM:(h + 1) * IDX_DIM], kc, _NT,
                                preferred_element_type=F32)
            sc = sc + jnp.maximum(r, 0.0) * jnp.concatenate([wb_ref[h]] * spc, axis=1)
        sc = sc + 0.0
        causal = k0 + lane_pos <= q_pos
        key = jnp.where(causal, to_key(sc), INT_MIN)
        scm = jnp.where(causal, sc, -jnp.inf)
        for j in range(spc):
            keys_ref[c * spc + j] = key[:, j * LANES:(j + 1) * LANES]
            mx = jnp.maximum(mx, scm[:, j * LANES:(j + 1) * LANES])
        return mx

    mx = lax.fori_loop(0, n_chunks, build, jnp.full((tq, LANES), -jnp.inf, F32))
    key_max = to_key(jnp.broadcast_to(jnp.max(mx, axis=1, keepdims=True), (tq, LANES)))

    def lane_sum(acc):
        return jnp.broadcast_to(jnp.sum(acc.astype(F32), axis=1, keepdims=True), (tq, LANES))

    def slab_loop(body, init):
        def chunk(c, carry):
            for j in range(spc):
                carry = body(c * spc + j, carry)
            return carry
        return lax.fori_loop(0, n_chunks, chunk, init)

    zero = jnp.zeros((tq, LANES), I32)

    def count_ge(cand):
        parts = []
        for r0 in range(0, tq, 128):
            rows = slice(r0, r0 + 128)
            cand_r = cand[rows]
            parts.append(slab_loop(lambda c, acc: acc + jnp.where(keys_ref[c, rows, :] >= cand_r, 1, 0), zero[rows]))
        return lane_sum(parts[0] if len(parts) == 1 else jnp.concatenate(parts, axis=0))

    n_real = (q_pos[:, :LANES] + 1).astype(F32)
    lo0 = jnp.full((tq, LANES), INT_MIN + 1, I32)
    si_ref[0] = lo0
    si_ref[1] = jnp.where(n_real > topk, key_max + 1, lo0 + 1)
    sf_ref[0] = n_real
    sf_ref[1] = jnp.zeros((tq, LANES), F32)

    def midpoint(lo, hi):
        return lo + lax.shift_right_logical(hi - lo, 1)

    def search_pass(cand):
        lo, hi, clo, chi = si_ref[0], si_ref[1], sf_ref[0], sf_ref[1]
        cnt = count_ge(cand)
        ge = cnt >= topk
        lo, clo = jnp.where(ge, cand, lo), jnp.where(ge, cnt, clo)
        hi = jnp.where(ge, hi, jnp.maximum(cand, lo + 1))
        hi = jnp.where(clo == topk, lo + 1, hi)
        si_ref[0], si_ref[1] = lo, hi
        sf_ref[0], sf_ref[1] = clo, jnp.where(ge, chi, cnt)
        open_rows = jnp.where((hi - lo) != 1, 1.0, 0.0)
        return jnp.max(open_rows, axis=0, keepdims=True)[0, 0]

    def guided(first):
        lo, hi = si_ref[0], si_ref[1]
        is_open = (hi - lo) != 1
        cand = midpoint(lo, hi)
        cand = jnp.where(is_open & (lo < 0) & (hi == 1), 0, cand)
        cand = jnp.where(is_open & (lo < 0) & (hi > 1), 1, cand)
        if first:
            probe = hi - (1 << 25)
            cand = jnp.where(is_open & (probe > 1), probe, cand)
        return search_pass(cand)

    guided(True)
    guided(False)
    pending = guided(False)

    def search_body(carry):
        it, _ = carry
        return it + 1, search_pass(midpoint(si_ref[0], si_ref[1]))

    lax.while_loop(lambda carry: (carry[0] < 40) & (carry[1] > 0.0), search_body, (jnp.int32(0), pending))
    tau = si_ref[0]
    clo, chi = sf_ref[0], sf_ref[1]

    r_need = topk - chi
    tied = clo > topk
    x_ref[...] = jnp.full((tq, LANES), s_len, I32)
    lane = lax.broadcasted_iota(I32, (tq, LANES), 1)

    @pl.when(jnp.max(jnp.where(tied, 1.0, 0.0)) > 0.0)
    def _():
        nbits = int(math.log2(s_len)) + 1

        def tie_pass(i, x):
            cand = x + lax.shift_left(jnp.int32(1), nbits - 1 - i)

            def body(c, acc):
                idx = c * LANES + lane
                return acc + jnp.where((keys_ref[c] == tau) & (idx < cand), 1, 0)

            cnt = lane_sum(slab_loop(body, zero))
            return jnp.where(cnt < r_need, cand, x)

        x = lax.fori_loop(0, nbits, tie_pass, zero)
        x_ref[...] = jnp.where(tied, x, s_len)

    x_cut = x_ref[...]

    def emit(c, carry):
        k = keys_ref[c]
        idx = c * LANES + lane
        sel = (k > tau) | ((k == tau) & (idx <= x_cut))
        m_ref[0, c] = jnp.where(sel, 0.0, NEG).astype(m_ref.dtype)
        return carry

    slab_loop(emit, 0)

    def fill(c, carry):
        m_ref[0, c] = jnp.full((tq, LANES), NEG, m_ref.dtype)
        return carry

    lax.fori_loop(n_slabs, nslab_all, fill, 0)


def _a_select(idx, *, tq=128, cw=1024):
    bsz, s_len, width = idx.shape
    nq = IDX_HEADS * IDX_DIM
    side = nq // LANES
    assert width == nq + LANES
    topk = min(TOPK_MAX, s_len // 4)
    nslab = s_len // LANES
    return pl.pallas_call(
        functools.partial(_asel_kernel, tq=tq, cw=cw, s_len=s_len, topk=float(topk)),
        out_shape=jax.ShapeDtypeStruct((bsz, nslab, s_len, LANES), BF16),
        grid=(bsz, s_len // tq),
        in_specs=[pl.BlockSpec((1, tq, nq), lambda b, i: (b, i, 0)),
                  pl.BlockSpec((1, tq, LANES), lambda b, i: (b, i, side)),
                  pl.BlockSpec((1, s_len, LANES), lambda b, i: (b, 0, side), pipeline_mode=pl.Buffered(1))],
        out_specs=pl.BlockSpec((1, nslab, tq, LANES), lambda b, i: (b, 0, i, 0)),
        scratch_shapes=[pltpu.VMEM((nslab, tq, LANES), I32),
                        pltpu.VMEM((IDX_HEADS, tq, LANES), F32),
                        pltpu.VMEM((tq, LANES), I32),
                        pltpu.VMEM((2, tq, LANES), I32),
                        pltpu.VMEM((2, tq, LANES), F32),
                        pltpu.VMEM((tq, nq), BF16)],
        compiler_params=_cp("arbitrary", "arbitrary"),
        name="a_select",
    )(idx, idx, idx)


def _attn_kernel(qt_ref, kt_ref, tab_ref, q_ref, k_ref, v_ref, msk_ref, bias_ref, o_ref,
                 m_sc, l_sc, acc_sc, mk_sc, *slots, t, kpt, nd_far, hpg):
    step = pl.program_id(1)
    qi = qt_ref[step]
    kp = kt_ref[step]
    d_last = qi - (kp * kpt + kpt - 1)
    n_unit = kpt * A_HEADS
    sub = t // LANES

    @pl.when(kp == 0)
    def _():
        m_sc[...] = jnp.full(m_sc.shape, NEG, F32)
        l_sc[...] = jnp.zeros_like(l_sc)
        acc_sc[...] = jnp.zeros_like(acc_sc)

    mk_sc[...] = jnp.concatenate([msk_ref[0, j] for j in range(kpt * sub)], axis=1).astype(F32)
    ns = 2 * hpg
    s_sc, p_sc, al_sc = slots[:ns], slots[ns:2 * ns], slots[2 * ns:]

    def slot(u):
        return (u // hpg % 2) * hpg + u % hpg

    def scores(u, far):
        h, kt = u % A_HEADS, u // A_HEADS
        keys = slice(kt * t, (kt + 1) * t)
        s = lax.dot_general(q_ref[h, 0], k_ref[h, 0, keys, :], _NT, preferred_element_type=F32) + mk_sc[:, keys]
        if not far:
            off0 = sub * jnp.clip(qi - (kp * kpt + kt), 0, nd_far) + (sub - 1)
            s = s + jnp.concatenate(
                [jnp.concatenate([bias_ref[off0 + a - b, h] for b in range(sub)], axis=1) for a in range(sub)],
                axis=0).astype(F32)
        s_sc[slot(u)][...] = s

    def softmax(u, far):
        h = u % A_HEADS
        s = s_sc[slot(u)][...]
        c = tab_ref[REL_BUCKETS - 1, h] * LOG2E if far else 0.0
        m_prev = m_sc[h]
        m_cur = jnp.broadcast_to(jnp.max(s, axis=-1, keepdims=True), (t, LANES))
        m_new = jnp.maximum(m_prev, m_cur + c)
        alpha = jnp.exp2(m_prev - m_new)
        p = jnp.exp2(s - jnp.concatenate([m_new - c] * (t // LANES), axis=1))
        p_part = p[:, :LANES]
        for j in range(1, t // LANES):
            p_part = p_part + p[:, j * LANES:(j + 1) * LANES]
        l_sc[h] = alpha * l_sc[h] + p_part
        m_sc[h] = m_new
        al_sc[slot(u)][...] = alpha
        p_sc[slot(u)][...] = p.astype(BF16)

    def values(u):
        h, kt = u % A_HEADS, u // A_HEADS
        acc_sc[h] = al_sc[slot(u)][...] * acc_sc[h] + jnp.dot(p_sc[slot(u)][...], v_ref[h, 0, kt * t:(kt + 1) * t, :],
                                                            preferred_element_type=F32)

    one = qt_ref[0] + 1

    def units(far):
        n_grp = n_unit // hpg
        for st in range(n_grp + 2):
            def stage(i, carry, st=st):
                for j in range(hpg):
                    if st < n_grp:
                        scores(st * hpg + j, far)
                    if 1 <= st <= n_grp:
                        softmax((st - 1) * hpg + j, far)
                    if st >= 2:
                        values((st - 2) * hpg + j)
                return carry

            lax.fori_loop(0, one, stage, 0)

    @pl.when(d_last >= nd_far)
    def _():
        units(True)

    @pl.when(d_last < nd_far)
    def _():
        units(False)

    @pl.when(kp == qi // kpt)
    def _():
        for h in range(A_HEADS):
            l = jnp.sum(l_sc[h], axis=-1, keepdims=True)
            o_ref[h, 0] = (acc_sc[h] * (1.0 / l)).astype(o_ref.dtype)


def _a_attention(qkv, mask4, bias, rel_table, *, t, kpt, nd_far, hpg=4):
    _, bsz, s_len, e = qkv.shape
    nq = s_len // t
    qt = np.concatenate([np.full(i // kpt + 1, i, np.int32) for i in range(nq)])
    kt = np.concatenate([np.arange(i // kpt + 1, dtype=np.int32) for i in range(nq)])
    spt = t // LANES
    grid_spec = pltpu.PrefetchScalarGridSpec(
        num_scalar_prefetch=2,
        grid=(bsz, len(qt)),
        in_specs=[
            pl.BlockSpec(memory_space=pltpu.SMEM),
            pl.BlockSpec((A_HEADS, 1, t, e), lambda b, s, qt, kt: (0, b, qt[s], 0)),
            pl.BlockSpec((A_HEADS, 1, kpt * t, e), lambda b, s, qt, kt: (1, b, kt[s], 0)),
            pl.BlockSpec((A_HEADS, 1, kpt * t, e), lambda b, s, qt, kt: (2, b, kt[s], 0)),
            pl.BlockSpec((1, kpt * spt, t, LANES), lambda b, s, qt, kt: (b, kt[s], qt[s], 0)),
            pl.BlockSpec(bias.shape, lambda b, s, qt, kt: (0, 0, 0, 0), pipeline_mode=pl.Buffered(1)),
        ],
        out_specs=pl.BlockSpec((A_HEADS, 1, t, e), lambda b, s, qt, kt: (0, b, qt[s], 0)),
        scratch_shapes=[pltpu.VMEM((A_HEADS, t, LANES), F32),
                        pltpu.VMEM((A_HEADS, t, LANES), F32),
                        pltpu.VMEM((A_HEADS, t, e), F32),
                        pltpu.VMEM((t, kpt * t), F32)]
        + [pltpu.VMEM((t, t), F32)] * (2 * hpg)
        + [pltpu.VMEM((t, t), BF16)] * (2 * hpg)
        + [pltpu.VMEM((t, LANES), F32)] * (2 * hpg),
    )
    return pl.pallas_call(
        functools.partial(_attn_kernel, t=t, kpt=kpt, nd_far=nd_far, hpg=hpg),
        out_shape=jax.ShapeDtypeStruct((A_HEADS, bsz, s_len, e), BF16),
        grid_spec=grid_spec,
        compiler_params=_cp("parallel", "arbitrary"),
        name="a_attention",
    )(jnp.asarray(qt), jnp.asarray(kt), rel_table, qkv, qkv, qkv, mask4, bias)


def _mixer_a(h, g, w_in, w_out, rel_table, bsz, s_len):
    n = h.shape[0]
    d = D_MODEL
    n_idx = IDX_HEADS * IDX_DIM + IDX_DIM + IDX_HEADS
    w_qkv = jnp.concatenate([w_in[:, :d] * (A_HEAD_DIM ** -0.5 * LOG2E), w_in[:, d:3 * d]], axis=1).astype(BF16)
    w_idx = jnp.pad(w_in[:, 3 * d:], ((0, 0), (0, 640 - n_idx))).astype(BF16)
    qkv = _norm_matmul(h, g, w_qkv, out_dtype=BF16, tn=1024, head_major=True, name="a_qkv")
    idx = _norm_matmul(h, g, w_idx, out_dtype=F32, tn=640, name="a_idx")
    qkv = qkv.reshape(3 * A_HEADS, bsz, s_len, A_HEAD_DIM)
    mask4 = _a_select(idx.reshape(bsz, s_len, 640))

    t = min(512, s_len)
    nd_far = 0
    while _rel_bucket_np(np.array(nd_far * t - (t - 1))) < REL_BUCKETS - 1:
        nd_far += 1
    sub = t // LANES
    offs = np.arange(-(sub - 1), sub * nd_far + sub)
    ii = np.arange(LANES)[:, None] - np.arange(LANES)[None, :]
    dist = (offs[:, None, None] * LANES + ii[None]).reshape(len(offs) * LANES, LANES)
    bias = _bias_tiles(rel_table, _rel_bucket_np(dist), BF16, tile=LANES, tr=LANES, scale=LOG2E)
    kpt = 2 if (s_len // t) % 2 == 0 else 1
    o = _a_attention(qkv, mask4, bias, rel_table, t=t, kpt=kpt, nd_far=nd_far)
    return _out_proj(o.reshape(A_HEADS, n, A_HEAD_DIM), w_out.astype(BF16), h, name="a_out")


def _bsp_kernel(u_ref, v_ref, lg_ref, lb_ref, wsp_ref, bsp_ref, wo_ref, r_ref, o_ref, wt_ref, y_ref, *, rows):
    @pl.when(pl.program_id(0) == 0)
    def _():
        r = lax.broadcasted_iota(I32, (B_CHUNK, B_CHUNK), 0)
        c = lax.broadcasted_iota(I32, (B_CHUNK, B_CHUNK), 1)
        for g in range(B_GROUPS):
            wt_ref[g] = jnp.where(r >= c, wsp_ref[g], 0.0).astype(BF16)

    vn = _layernorm(v_ref[...].astype(F32), lg_ref[...], lb_ref[...]).astype(BF16)
    gw = B_HALF // B_GROUPS
    for c in range(rows // B_CHUNK):
        rs = slice(c * B_CHUNK, (c + 1) * B_CHUNK)
        for g in range(B_GROUPS):
            cs = slice(g * gw, (g + 1) * gw)
            sv = jnp.dot(wt_ref[g], vn[rs, cs], preferred_element_type=F32) + bsp_ref[g]
            y_ref[rs, cs] = (u_ref[rs, cs].astype(F32) * sv).astype(BF16)
    o_ref[...] = r_ref[...] + jnp.dot(y_ref[...], wo_ref[...], preferred_element_type=F32)


def _b_spatial_out(z, ln_g, ln_b, w_sp, b_sp, w_out, resid, *, rows=512):
    n, d = resid.shape
    return pl.pallas_call(
        functools.partial(_bsp_kernel, rows=rows),
        out_shape=jax.ShapeDtypeStruct((n, d), F32),
        grid=(n // rows,),
        in_specs=[pl.BlockSpec((rows, B_HALF), lambda i: (i, 0)),
                  pl.BlockSpec((rows, B_HALF), lambda i: (i, 1)),
                  pl.BlockSpec((1, B_HALF), lambda i: (0, 0)),
                  pl.BlockSpec((1, B_HALF), lambda i: (0, 0)),
                  pl.BlockSpec((B_GROUPS, B_CHUNK, B_CHUNK), lambda i: (0, 0, 0)),
                  pl.BlockSpec((B_GROUPS, B_CHUNK, 1), lambda i: (0, 0, 0)),
                  pl.BlockSpec((B_HALF, d), lambda i: (0, 0), pipeline_mode=pl.Buffered(1)),
                  pl.BlockSpec((rows, d), lambda i: (i, 0))],
        out_specs=pl.BlockSpec((rows, d), lambda i: (i, 0)),
        scratch_shapes=[pltpu.VMEM((B_GROUPS, B_CHUNK, B_CHUNK), BF16), pltpu.VMEM((rows, B_HALF), BF16)],
        compiler_params=_cp("arbitrary"),
        name="b_spatial_out",
    )(z, z, ln_g.reshape(1, B_HALF), ln_b.reshape(1, B_HALF), w_sp, b_sp.reshape(B_GROUPS, B_CHUNK, 1),
      w_out, resid)


def _mixer_b(h, g, w_in, b_in, ln_g, ln_b, w_sp, b_sp, w_out):
    z = _norm_matmul(h, g, w_in.astype(BF16), b_in, mode="gelu", out_dtype=BF16, tn=1024, name="b_in")
    return _b_spatial_out(z, ln_g, ln_b, w_sp, b_sp, w_out.astype(BF16), h)


C_HALO = 32


def _conv_kernel(cur_ref, halo_ref, wdw_ref, bdw_ref, lg_ref, lb_ref, w2_ref, b2_ref, r_ref, o_ref,
                 ext_ref, yc_ref, *, ts, rt, ct):
    i = pl.program_id(1)
    ext_ref[C_HALO:, :] = cur_ref[0]

    @pl.when(i == 0)
    def _():
        ext_ref[:C_HALO, :] = jnp.zeros((C_HALO, D_MODEL), F32)

    @pl.when(i > 0)
    def _():
        ext_ref[:C_HALO, :] = halo_ref[0]

    off = C_HALO - (C_KERNEL - 1)

    def tile(it, carry):
        r0 = pl.multiple_of((it // (D_MODEL // ct)) * rt, rt)
        c0 = pl.multiple_of((it % (D_MODEL // ct)) * ct, ct)
        nwin = rt + C_HALO
        win = ext_ref[pl.ds(r0, nwin), pl.ds(c0, ct)]
        acc = jnp.zeros((rt, ct), F32) + bdw_ref[:, pl.ds(c0, ct)]
        for b in range(8):
            rolled = pltpu.roll(win, nwin - (off + b), axis=0)
            for a in range((C_KERNEL - b + 7) // 8):
                j = 8 * a + b
                acc = acc + rolled[8 * a:8 * a + rt] * wdw_ref[j:j + 1, pl.ds(c0, ct)]
        yc_ref[pl.ds(r0, rt), pl.ds(c0, ct)] = acc
        return carry

    lax.fori_loop(0, (ts // rt) * (D_MODEL // ct), tile, 0)
    y = _layernorm(yc_ref[...], lg_ref[...], lb_ref[...])
    y = (y * jax.nn.sigmoid(y)).astype(BF16)
    o_ref[0] = r_ref[0] + jnp.dot(y, w2_ref[...], preferred_element_type=F32) + b2_ref[...]


def _c_conv(y1, w_dw, b_dw, ln_g, ln_b, w2, b2, resid, *, ts=512, rt=128, ct=128):
    bsz, s_len, d = y1.shape
    hb = ts // C_HALO
    vec = lambda a: a.reshape(1, d)
    return pl.pallas_call(
        functools.partial(_conv_kernel, ts=ts, rt=rt, ct=ct),
        out_shape=jax.ShapeDtypeStruct((bsz, s_len, d), F32),
        grid=(bsz, s_len // ts),
        in_specs=[pl.BlockSpec((1, ts, d), lambda b, i: (b, i, 0)),
                  pl.BlockSpec((1, C_HALO, d), lambda b, i: (b, jnp.maximum(i * hb - 1, 0), 0)),
                  pl.BlockSpec((C_KERNEL, d), lambda b, i: (0, 0)),
                  pl.BlockSpec((1, d), lambda b, i: (0, 0)),
                  pl.BlockSpec((1, d), lambda b, i: (0, 0)),
                  pl.BlockSpec((1, d), lambda b, i: (0, 0)),
                  pl.BlockSpec((d, d), lambda b, i: (0, 0)),
                  pl.BlockSpec((1, d), lambda b, i: (0, 0)),
                  pl.BlockSpec((1, ts, d), lambda b, i: (b, i, 0))],
        out_specs=pl.BlockSpec((1, ts, d), lambda b, i: (b, i, 0)),
        scratch_shapes=[pltpu.VMEM((ts + C_HALO, d), F32), pltpu.VMEM((ts, d), F32)],
        compiler_params=_cp("parallel", "parallel"),
        name="c_conv",
    )(y1, y1, w_dw, vec(b_dw), vec(ln_g), vec(ln_b), w2, vec(b2), resid)


def _mixer_c(h, g, w_pw1, b_pw1, w_dw, b_dw, ln_g, ln_b, w_pw2, b_pw2, bsz, s_len):
    n, d = h.shape
    y1 = _norm_matmul(h, g, w_pw1.astype(BF16), b_pw1, mode="glu", out_dtype=F32, tn=512, name="c_pw1")
    out = _c_conv(y1.reshape(bsz, s_len, d), w_dw, b_dw, ln_g, ln_b, w_pw2.astype(BF16), b_pw2,
                  h.reshape(bsz, s_len, d))
    return out.reshape(n, d)


def _dattn_kernel(q_ref, kc_ref, kp_ref, vc_ref, vp_ref, bias_ref, o_ref, lse_ref):
    nb = pl.program_id(2)
    first = jnp.where(nb == 0, NEG, 0.0)
    e = D_HEAD_DIM
    col = lax.broadcasted_iota(I32, (D_BLOCK, 2 * D_BLOCK), 1)
    first_mask = jnp.where(col < D_BLOCK, first, 0.0)
    low = lax.broadcasted_iota(I32, (D_BLOCK, LANES), 1) < e
    for hp in range(D_HEADS // 2):
        ps = slice(hp * LANES, (hp + 1) * LANES)
        q2 = q_ref[:, ps]
        kk = jnp.concatenate([kp_ref[:, ps], kc_ref[:, ps]], axis=0)
        vv = jnp.concatenate([vp_ref[:, ps], vc_ref[:, ps]], axis=0)
        outs, lses = [], []
        for hh in range(2):
            qh = jnp.where(low if hh == 0 else ~low, q2, jnp.zeros_like(q2))
            s = lax.dot_general(qh, kk, _NT, preferred_element_type=F32) * (e ** -0.5) + bias_ref[2 * hp + hh]
            s = s + first_mask
            m = jnp.max(s, axis=-1, keepdims=True)
            p = jnp.exp(s - m)
            l = jnp.sum(p, axis=-1, keepdims=True)
            outs.append(jnp.dot(p.astype(BF16), vv, preferred_element_type=F32) / l)
            lses.append(jnp.broadcast_to(m + jnp.log(l), (D_BLOCK, LANES)))
        o_ref[0, :, ps] = jnp.where(low, outs[0], outs[1])
        lse_ref[0, :, ps] = jnp.where(low, lses[0], lses[1])


def _dproj_kernel(x_ref, g_ref, w_ref, o_ref, xs_ref, *, dil):
    span, d = x_ref.shape
    nslab = d // LANES
    for j in range(nslab):
        xs_ref[j] = x_ref[:, j * LANES:(j + 1) * LANES]
    rows = 4 * D_BLOCK
    for c in range(span // rows):
        xr = jnp.concatenate(
            [jnp.concatenate([xs_ref[j, pl.ds(r, D_BLOCK, stride=dil), :] for j in range(nslab)], axis=1)
             for r in range(4 * c, 4 * c + 4)], axis=0)
        y = jnp.dot(_rms_bf16(xr, g_ref[...]), w_ref[...], preferred_element_type=F32)
        o_ref[c * rows:(c + 1) * rows, :] = y.astype(o_ref.dtype)


def _d_proj(h, g, w, dil):
    n, d = h.shape
    m = w.shape[1]
    span = D_BLOCK * dil
    return pl.pallas_call(
        functools.partial(_dproj_kernel, dil=dil),
        out_shape=jax.ShapeDtypeStruct((n, m), BF16),
        grid=(n // span,),
        in_specs=[pl.BlockSpec((span, d), lambda i: (i, 0)),
                  pl.BlockSpec((1, d), lambda i: (0, 0)),
                  pl.BlockSpec((d, m), lambda i: (0, 0))],
        out_specs=pl.BlockSpec((span, m), lambda i: (i, 0)),
        scratch_shapes=[pltpu.VMEM((d // LANES, span, LANES), F32)],
        compiler_params=_cp("parallel"),
        name=f"d_in_dil{dil}",
    )(h, g.reshape(1, d), w)


def _d_group(proj, bias_g, gidx, dil, bsz, s_len):
    n_sub = s_len // dil
    nb = n_sub // D_BLOCK
    w = D_HEADS * D_HEAD_DIM
    blk = (D_BLOCK, w)
    rb = lambda b, r, n: b * (s_len // D_BLOCK) + n * dil + r
    cur = lambda col: (lambda b, r, n: (rb(b, r, n), col))
    prev = lambda col: (lambda b, r, n: (rb(b, r, jnp.maximum(n - 1, 0)), col))
    out_blk = (1, D_BLOCK, w)
    out_sds = jax.ShapeDtypeStruct((bsz, n_sub, dil * w), F32)
    o, lse = pl.pallas_call(
        _dattn_kernel,
        out_shape=(out_sds, out_sds),
        grid=(bsz, dil, nb),
        in_specs=[pl.BlockSpec(blk, cur(0)),
                  pl.BlockSpec(blk, cur(1)),
                  pl.BlockSpec(blk, prev(1)),
                  pl.BlockSpec(blk, cur(2)),
                  pl.BlockSpec(blk, prev(2)),
                  pl.BlockSpec((D_HEADS, D_BLOCK, 2 * D_BLOCK), lambda b, r, n: (0, 0, 0))],
        out_specs=(pl.BlockSpec(out_blk, lambda b, r, n: (b, n, r)),
                   pl.BlockSpec(out_blk, lambda b, r, n: (b, n, r))),
        compiler_params=_cp("parallel", "parallel", "arbitrary"),
        name=f"d_attn_g{gidx}",
    )(proj, proj, proj, proj, proj, bias_g)
    return o.reshape(bsz * s_len, w), lse.reshape(bsz * s_len, w)


def _dmerge_kernel(o0, o1, o2, l0, l1, l2, w_ref, r_ref, out_ref):
    a0, a1, a2 = l0[...], l1[...], l2[...]
    mx = jnp.maximum(jnp.maximum(a0, a1), a2)
    e0, e1, e2 = jnp.exp(a0 - mx), jnp.exp(a1 - mx), jnp.exp(a2 - mx)
    y = (e0 * o0[...] + e1 * o1[...] + e2 * o2[...]) / (e0 + e1 + e2)
    out_ref[...] = r_ref[...] + jnp.dot(y.astype(BF16), w_ref[...], preferred_element_type=F32)


def _d_merge(outs, lses, w_out, resid, *, tm=512):
    n, w = outs[0].shape
    d = resid.shape[1]
    row = pl.BlockSpec((tm, w), lambda i: (i, 0))
    return pl.pallas_call(
        _dmerge_kernel,
        out_shape=jax.ShapeDtypeStruct((n, d), F32),
        grid=(n // tm,),
        in_specs=[row] * 6 + [pl.BlockSpec((w, d), lambda i: (0, 0)),
                              pl.BlockSpec((tm, d), lambda i: (i, 0))],
        out_specs=pl.BlockSpec((tm, d), lambda i: (i, 0)),
        compiler_params=_cp("parallel"),
        name="d_merge",
    )(*outs, *lses, w_out, resid)


def _mixer_d(h, g, w_in, w_out, rel_table, bsz, s_len):
    w = D_HEADS * D_HEAD_DIM
    ng = len(D_PAIRS)
    p_loc = np.arange(D_BLOCK)[:, None]
    j_loc = np.arange(2 * D_BLOCK)[None, :]
    m = p_loc + D_BLOCK - j_loc
    idx = []
    for window, dil in D_PAIRS:
        steps = window // dil
        idx.append(np.where((m >= 0) & (m <= steps), _rel_bucket_np(m * dil), -1))
    bias = _bias_tiles(rel_table, np.concatenate(idx, 0).astype(np.int32), F32, tile=D_BLOCK, tr=D_BLOCK)
    outs, lses = [], []
    for gi, (_, dil) in enumerate(D_PAIRS):
        cols = [w_in[:, (part * ng + gi) * w:(part * ng + gi + 1) * w] for part in range(3)]
        w_g = jnp.concatenate(cols, axis=1).astype(BF16)
        if dil == 1:
            proj = _norm_matmul(h, g, w_g, out_dtype=BF16, tn=3 * w, name="d_in_dil1")
        else:
            proj = _d_proj(h, g, w_g, dil)
        o, lse = _d_group(proj, bias[gi], gi, dil, bsz, s_len)
        outs.append(o)
        lses.append(lse)
    return _d_merge(outs, lses, w_out.astype(BF16), h)


def kernel(x, norm_g, final_g, ffn_w_in, ffn_w_out, rel_table, a_w_in, a_w_out, b_w_in, b_b_in, b_ln_g, b_ln_b, b_w_sp, b_b_sp, b_w_out, c_w_pw1, c_b_pw1, c_w_dw, c_b_dw, c_ln_g, c_ln_b, c_w_pw2, c_b_pw2, d_w_in, d_w_out):
    bsz, s_len, d = x.shape
    depth = norm_g.shape[0]
    h = x.reshape(bsz * s_len, d)
    for i in range(depth):
        kind, j = i % 4, i // 4
        h = _ffn(h, norm_g[i, 0], ffn_w_in[i, 0].astype(BF16), ffn_w_out[i, 0].astype(BF16))
        g = norm_g[i, 1]
        if kind == 0:
            h = _mixer_a(h, g, a_w_in[j], a_w_out[j], rel_table, bsz, s_len)
        elif kind == 1:
            h = _mixer_b(h, g, b_w_in[j], b_b_in[j], b_ln_g[j], b_ln_b[j], b_w_sp[j], b_b_sp[j], b_w_out[j])
        elif kind == 2:
            h = _mixer_c(h, g, c_w_pw1[j], c_b_pw1[j], c_w_dw[j], c_b_dw[j], c_ln_g[j], c_ln_b[j],
                         c_w_pw2[j], c_b_pw2[j], bsz, s_len)
        else:
            h = _mixer_d(h, g, d_w_in[j], d_w_out[j], rel_table, bsz, s_len)
        h = _ffn(h, norm_g[i, 2], ffn_w_in[i, 1].astype(BF16), ffn_w_out[i, 1].astype(BF16),
                 final_g if i == depth - 1 else None)
    return h.reshape(bsz, s_len, d)
```

```python
import functools
import math

import numpy as np
import jax
import jax.numpy as jnp
from jax import lax
from jax.experimental import pallas as pl
from jax.experimental.pallas import tpu as pltpu

F32, BF16, I32 = jnp.float32, jnp.bfloat16, jnp.int32

D_MODEL = 1024
EPS = 1e-6
D_FF = 2816
REL_BUCKETS = 32
REL_MAX_DIST = 2048
A_HEADS = 8
A_HEAD_DIM = 128
IDX_HEADS = 8
IDX_DIM = 64
TOPK_MAX = 256
B_CHUNK = 128
B_HALF = 3 * D_MODEL
B_GROUPS = 8
C_KERNEL = 31
D_PAIRS = ((128, 1), (512, 4), (2048, 16))
D_HEADS = 8
D_HEAD_DIM = 64
D_BLOCK = 128

LANES = 128
MXU_TILE = 256
VMEM_LIMIT = 56 * 1024 * 1024
NEG = -1e30
INT_MIN = -2 ** 31
KEY_UNIT = 1 << 16
MIN_NORMAL_TOP = 0x0080
LOG2E = math.log2(math.e)

_NT = (((1,), (1,)), ((), ()))


def _cp(*sem, flags=None):
    return pltpu.CompilerParams(dimension_semantics=sem, vmem_limit_bytes=VMEM_LIMIT, flags=flags)


def _rms_bf16(x, g):
    ms = jnp.mean(x * x, axis=-1, keepdims=True)
    return (x * lax.rsqrt(ms + EPS) * g).astype(BF16)


def _layernorm(x, g, b):
    mu = jnp.mean(x, axis=-1, keepdims=True)
    xc = x - mu
    var = jnp.mean(xc * xc, axis=-1, keepdims=True)
    return xc * lax.rsqrt(var + EPS) * g + b


def _ffn_kernel(x_ref, xnext_ref, g_ref, wi_ref, wo_ref, *refs, chunks):
    o_ref, xn_ref = refs[-2], refs[-1]
    i = pl.program_id(0)
    slot = i % 2

    @pl.when(i == 0)
    def _():
        xn_ref[0] = _rms_bf16(x_ref[...], g_ref[...])

    xn = xn_ref[slot]
    acc = None
    c0 = 0
    for tf in chunks:
        gate = jnp.dot(xn, wi_ref[:, c0:c0 + tf], preferred_element_type=F32)
        up = jnp.dot(xn, wi_ref[:, D_FF + c0:D_FF + c0 + tf], preferred_element_type=F32)
        a = (gate * jax.nn.sigmoid(gate) * up).astype(BF16)
        part = jnp.dot(a, wo_ref[c0:c0 + tf, :], preferred_element_type=F32)
        acc = part if acc is None else acc + part
        c0 += tf
    out = x_ref[...] + 0.5 * acc
    if len(refs) == 3:
        ms = jnp.mean(out * out, axis=-1, keepdims=True)
        out = out * lax.rsqrt(ms + EPS) * refs[0][...]
    o_ref[...] = out
    xn_ref[1 - slot] = _rms_bf16(xnext_ref[...], g_ref[...])


def _ffn(h, g, w_in, w_out, final_g=None, *, tm=512):
    n, d = h.shape
    last = n // tm - 1
    n_tiles = D_FF // MXU_TILE
    chunks = ((n_tiles + 1) // 2 * MXU_TILE, n_tiles // 2 * MXU_TILE)
    resident = dict(pipeline_mode=pl.Buffered(1))
    vec = pl.BlockSpec((1, d), lambda i: (0, 0))
    in_specs = [pl.BlockSpec((tm, d), lambda i: (i, 0)),
                pl.BlockSpec((tm, d), lambda i: (jnp.minimum(i + 1, last), 0)),
                vec,
                pl.BlockSpec((d, 2 * D_FF), lambda i: (0, 0), **resident),
                pl.BlockSpec((D_FF, d), lambda i: (0, 0), **resident)]
    args = [h, h, g.reshape(1, d), w_in, w_out]
    if final_g is not None:
        in_specs.append(vec)
        args.append(final_g.reshape(1, d))
    return pl.pallas_call(
        functools.partial(_ffn_kernel, chunks=chunks),
        out_shape=jax.ShapeDtypeStruct((n, d), F32),
        grid=(n // tm,),
        in_specs=in_specs,
        out_specs=pl.BlockSpec((tm, d), lambda i: (i, 0)),
        scratch_shapes=[pltpu.VMEM((2, tm, d), BF16)],
        compiler_params=_cp("arbitrary"),
        name="ffn",
    )(*args)


def _nm_kernel(mode, has_bias, tn, x_ref, g_ref, w_ref, *refs):
    b_ref = refs[0] if has_bias else None
    o_ref, xn_ref = refs[-2], refs[-1]
    s = pl.program_id(0)
    head_major = len(o_ref.shape) == 3
    m_out = o_ref.shape[0] * LANES if head_major else o_ref.shape[1]

    @pl.when(s == 0)
    def _():
        xn_ref[0] = _rms_bf16(x_ref[...], g_ref[...])

    @pl.when(s > 0)
    def _():
        xn = xn_ref[(s - 1) % 2]

        def column_block(c0):
            y = jnp.dot(xn, w_ref[:, c0:c0 + tn], preferred_element_type=F32)
            return y + b_ref[:, c0:c0 + tn] if has_bias else y

        for j in range(m_out // tn):
            y = column_block(j * tn)
            if mode == "gelu":
                y = jax.nn.gelu(y)
            elif mode == "glu":
                y = y * jax.nn.sigmoid(column_block(m_out + j * tn))
            if head_major:
                for hh in range(tn // LANES):
                    o_ref[j * (tn // LANES) + hh] = y[:, hh * LANES:(hh + 1) * LANES].astype(o_ref.dtype)
            else:
                o_ref[:, j * tn:(j + 1) * tn] = y.astype(o_ref.dtype)
        xn_ref[s % 2] = _rms_bf16(x_ref[...], g_ref[...])


def _norm_matmul(h, g, w, b=None, *, mode="plain", out_dtype=BF16, tm=512, tn=512, head_major=False, name="proj"):
    n, d = h.shape
    m_total = w.shape[1]
    m_out = m_total // 2 if mode == "glu" else m_total
    nt = n // tm
    has_bias = b is not None
    prev = lambda s: jnp.maximum(s - 1, 0)
    if head_major:
        out_shape = jax.ShapeDtypeStruct((m_out // LANES, n, LANES), out_dtype)
        out_spec = pl.BlockSpec((m_out // LANES, tm, LANES), lambda s: (0, prev(s), 0))
    else:
        out_shape = jax.ShapeDtypeStruct((n, m_out), out_dtype)
        out_spec = pl.BlockSpec((tm, m_out), lambda s: (prev(s), 0))
    resident = dict(pipeline_mode=pl.Buffered(1))
    in_specs = [pl.BlockSpec((tm, d), lambda s: (jnp.minimum(s, nt - 1), 0)),
                pl.BlockSpec((1, d), lambda s: (0, 0)),
                pl.BlockSpec((d, m_total), lambda s: (0, 0), **resident)]
    args = [h, g.reshape(1, d), w]
    if has_bias:
        in_specs.append(pl.BlockSpec((1, m_total), lambda s: (0, 0)))
        args.append(b.reshape(1, m_total))
    return pl.pallas_call(
        functools.partial(_nm_kernel, mode, has_bias, tn),
        out_shape=out_shape,
        grid=(nt + 1,),
        in_specs=in_specs,
        out_specs=out_spec,
        scratch_shapes=[pltpu.VMEM((2, tm, d), BF16)],
        compiler_params=_cp("arbitrary"),
        name=name,
    )(*args)


def _op_kernel(has_bias, y_ref, w_ref, *refs):
    if has_bias:
        b_ref, r_ref, o_ref = refs
    else:
        r_ref, o_ref = refs
    if len(y_ref.shape) == 3:
        y = jnp.concatenate([y_ref[hh] for hh in range(y_ref.shape[0])], axis=1)
    else:
        y = y_ref[...]
    acc = jnp.dot(y, w_ref[...], preferred_element_type=F32)
    if has_bias:
        acc = acc + b_ref[...]
    o_ref[...] = r_ref[...] + acc


def _out_proj(y, w, resid, b=None, *, tm=512, name="out_proj"):
    k, d = w.shape
    n = resid.shape[0]
    has_bias = b is not None
    if y.ndim == 3:
        y_spec = pl.BlockSpec((k // LANES, tm, LANES), lambda i: (0, i, 0))
    else:
        y_spec = pl.BlockSpec((tm, k), lambda i: (i, 0))
    in_specs = [y_spec, pl.BlockSpec((k, d), lambda i: (0, 0))]
    args = [y, w]
    if has_bias:
        in_specs.append(pl.BlockSpec((1, d), lambda i: (0, 0)))
        args.append(b.reshape(1, d))
    in_specs.append(pl.BlockSpec((tm, d), lambda i: (i, 0)))
    args.append(resid)
    return pl.pallas_call(
        functools.partial(_op_kernel, has_bias),
        out_shape=jax.ShapeDtypeStruct((n, d), F32),
        grid=(n // tm,),
        in_specs=in_specs,
        out_specs=pl.BlockSpec((tm, d), lambda i: (i, 0)),
        compiler_params=_cp("parallel"),
        name=name,
    )(*args)


def _rel_bucket_np(dist):
    max_exact = REL_BUCKETS // 2
    d = np.maximum(dist, 0)
    df = np.maximum(d, 1).astype(np.float32)
    large = max_exact + (np.log(df / np.float32(max_exact)) / np.float32(math.log(REL_MAX_DIST / max_exact))
                         * np.float32(REL_BUCKETS - max_exact)).astype(np.int32)
    large = np.minimum(large, REL_BUCKETS - 1)
    return np.where(d < max_exact, d, large).astype(np.int32)


def _bias_kernel(scale, tab_ref, idx_ref, o_ref):
    idx = idx_ref[...]
    for h in range(o_ref.shape[1]):
        acc = jnp.zeros(idx.shape, F32)
        for b in range(REL_BUCKETS):
            acc = jnp.where(idx == b, tab_ref[b, h] * scale, acc)
        o_ref[0, h] = jnp.where(idx < 0, NEG, acc).astype(o_ref.dtype)


def _bias_tiles(rel_table, bucket_idx, out_dtype, *, tile, tr, scale=1.0):
    r, c = bucket_idx.shape
    nh = rel_table.shape[1]
    per = tile // tr
    return pl.pallas_call(
        functools.partial(_bias_kernel, scale),
        out_shape=jax.ShapeDtypeStruct((r // tile, nh, tile, c), out_dtype),
        grid=(r // tr,),
        in_specs=[pl.BlockSpec(memory_space=pltpu.SMEM),
                  pl.BlockSpec((tr, c), lambda i: (i, 0))],
        out_specs=pl.BlockSpec((1, nh, tr, c), lambda i: (i // per, 0, i % per, 0)),
        compiler_params=_cp("parallel"),
        name="bias_tiles",
    )(rel_table, jnp.asarray(bucket_idx))


def _asel_kernel(qi_ref, side_ref, kall_ref, m_ref, keys_ref, wb_ref, x_ref, si_ref, sf_ref, qb_ref, kb_ref,
                 *, tq, cw, s_len, topk):
    qb = pl.program_id(1)
    spc = cw // LANES
    n_chunks = (qb * tq + tq + cw - 1) // cw
    n_slabs = n_chunks * spc
    nslab_all = s_len // LANES

    qb_ref[...] = qi_ref[0].astype(BF16)
    wi = side_ref[0, :, IDX_DIM:IDX_DIM + IDX_HEADS] * (IDX_HEADS ** -0.5 * IDX_DIM ** -0.5)
    for h in range(IDX_HEADS):
        wb_ref[h] = jnp.broadcast_to(wi[:, h:h + 1], (tq, LANES))
    q_pos = qb * tq + lax.broadcasted_iota(I32, (tq, cw), 0)
    lane_pos = lax.broadcasted_iota(I32, (tq, cw), 1)

    def to_key(x):
        bits = pltpu.bitcast(x, I32)
        return bits ^ ((bits >> 31) & 0x7FFFFFFF)

    def build(c, mx):
        k0 = pl.multiple_of(c * cw, cw)
        kc = kall_ref[0, pl.ds(k0, cw), :][:, :IDX_DIM].astype(BF16)
        sc = jnp.zeros((tq, cw), F32)
        for h in range(IDX_HEADS):
            r = lax.dot_general(qb_ref[:, h * IDX_DIM:(h + 1) * IDX_DIM], kc, _NT,
                                preferred_element_type=F32)
            sc = sc + jnp.maximum(r, 0.0) * jnp.concatenate([wb_ref[h]] * spc, axis=1)
        sc = sc + 0.0
        causal = k0 + lane_pos <= q_pos
        key = jnp.where(causal, to_key(sc), INT_MIN)
        scm = jnp.where(causal, sc, -jnp.inf)
        top = pltpu.bitcast(pltpu.bitcast(scm, I32) & -KEY_UNIT, F32).astype(BF16)
        for j in range(spc):
            keys_ref[c * spc + j] = key[:, j * LANES:(j + 1) * LANES]
            kb_ref[c * spc + j] = top[:, j * LANES:(j + 1) * LANES]
            mx = jnp.maximum(mx, scm[:, j * LANES:(j + 1) * LANES])
        return mx

    mx = lax.fori_loop(0, n_chunks, build, jnp.full((tq, LANES), -jnp.inf, F32))
    key_max = to_key(jnp.broadcast_to(jnp.max(mx, axis=1, keepdims=True), (tq, LANES)))

    def lane_sum(acc):
        return jnp.broadcast_to(jnp.sum(acc.astype(F32), axis=1, keepdims=True), (tq, LANES))

    def slab_loop(body, init):
        def chunk(c, carry):
            for j in range(spc):
                carry = body(c * spc + j, carry)
            return carry
        return lax.fori_loop(0, n_chunks, chunk, init)

    zero = jnp.zeros((tq, LANES), I32)

    def count_ge(cand):
        parts = []
        for r0 in range(0, tq, 128):
            rows = slice(r0, r0 + 128)
            cand_r = cand[rows]
            parts.append(slab_loop(lambda c, acc: acc + jnp.where(keys_ref[c, rows, :] >= cand_r, 1, 0), zero[rows]))
        return lane_sum(parts[0] if len(parts) == 1 else jnp.concatenate(parts, axis=0))

    def count_ge_coarse(cand):
        c16 = cand >> 16
        pos = jnp.where((c16 > 0) & (c16 < MIN_NORMAL_TOP), MIN_NORMAL_TOP, c16)
        neg = jnp.minimum(-c16 - 1, 0x7F7F)
        neg = jnp.where(neg < MIN_NORMAL_TOP, 0, neg)
        bits = jnp.where(c16 >= 0, pos << 16, (neg << 16) + INT_MIN)
        cand_b = pltpu.bitcast(bits, F32).astype(BF16)
        one_b, zero_b = jnp.ones((tq, LANES), BF16), jnp.zeros((tq, LANES), BF16)
        acc = slab_loop(lambda c, a: a + jnp.where(kb_ref[c] >= cand_b, one_b, zero_b), zero_b)
        return lane_sum(acc)

    n_real = (q_pos[:, :LANES] + 1).astype(F32)
    lo0 = jnp.full((tq, LANES), INT_MIN, I32)
    si_ref[0] = lo0
    si_ref[1] = jnp.where(n_real > topk, ((key_max >> 16) + 1) << 16, lo0 + KEY_UNIT)
    sf_ref[0] = n_real
    sf_ref[1] = jnp.zeros((tq, LANES), F32)

    def midpoint(lo, hi, unit):
        half = lax.shift_right_logical(hi - lo, 1)
        return lo + (half & -unit)

    def search_pass(cand, unit, count_fn):
        lo, hi, clo, chi = si_ref[0], si_ref[1], sf_ref[0], sf_ref[1]
        cnt = count_fn(cand)
        ge = cnt >= topk
        lo, clo = jnp.where(ge, cand, lo), jnp.where(ge, cnt, clo)
        hi = jnp.where(ge, hi, jnp.maximum(cand, lo + unit))
        hi = jnp.where(clo == topk, lo + unit, hi)
        si_ref[0], si_ref[1] = lo, hi
        sf_ref[0], sf_ref[1] = clo, jnp.where(ge, chi, cnt)
        open_rows = jnp.where((hi - lo) != unit, 1.0, 0.0)
        return jnp.max(open_rows, axis=0, keepdims=True)[0, 0]

    def bisect(unit, count_fn, pending):
        def body(carry):
            return carry[0] + 1, search_pass(midpoint(si_ref[0], si_ref[1], unit), unit, count_fn)
        lax.while_loop(lambda carry: (carry[0] < 40) & (carry[1] > 0.0), body, (jnp.int32(0), pending))

    def coarse_guided(first):
        lo, hi = si_ref[0], si_ref[1]
        is_open = (hi - lo) != KEY_UNIT
        cand = midpoint(lo, hi, KEY_UNIT)
        cand = jnp.where(is_open & (lo < 0) & (hi > 0), 0, cand)
        if first:
            probe = hi - (1 << 25)
            cand = jnp.where(is_open & (probe > 0), probe, cand)
        return search_pass(cand, KEY_UNIT, count_ge_coarse)

    coarse_guided(True)
    bisect(KEY_UNIT, count_ge_coarse, coarse_guided(False))

    lo, hi, clo = si_ref[0], si_ref[1], sf_ref[0]
    lo = jnp.maximum(lo, INT_MIN + 1)
    done = (clo <= topk)
    si_ref[0], si_ref[1] = lo, jnp.where(done, lo + 1, hi)
    first_fine = jnp.where((lo == 0) & ~done, 1, midpoint(lo, si_ref[1], 1))
    bisect(1, count_ge, search_pass(first_fine, 1, count_ge))
    tau = si_ref[0]
    clo, chi = sf_ref[0], sf_ref[1]

    r_need = topk - chi
    tied = clo > topk
    x_ref[...] = jnp.full((tq, LANES), s_len, I32)
    lane = lax.broadcasted_iota(I32, (tq, LANES), 1)

    @pl.when(jnp.max(jnp.where(tied, 1.0, 0.0)) > 0.0)
    def _():
        nbits = int(math.log2(s_len)) + 1

        def tie_pass(i, x):
            cand = x + lax.shift_left(jnp.int32(1), nbits - 1 - i)

            def body(c, acc):
                idx = c * LANES + lane
                return acc + jnp.where((keys_ref[c] == tau) & (idx < cand), 1, 0)

            cnt = lane_sum(slab_loop(body, zero))
            return jnp.where(cnt < r_need, cand, x)

        x = lax.fori_loop(0, nbits, tie_pass, zero)
        x_ref[...] = jnp.where(tied, x, s_len)

    x_cut = x_ref[...]

    def emit(c, carry):
        k = keys_ref[c]
        idx = c * LANES + lane
        sel = (k > tau) | ((k == tau) & (idx <= x_cut))
        m_ref[0, c] = jnp.where(sel, 0.0, NEG).astype(m_ref.dtype)
        return carry

    slab_loop(emit, 0)

    def fill(c, carry):
        m_ref[0, c] = jnp.full((tq, LANES), NEG, m_ref.dtype)
        return carry

    lax.fori_loop(n_slabs, nslab_all, fill, 0)


def _a_select(idx, *, tq=128, cw=1024):
    bsz, s_len, width = idx.shape
    nq = IDX_HEADS * IDX_DIM
    side = nq // LANES
    assert width == nq + LANES
    topk = min(TOPK_MAX, s_len // 4)
    nslab = s_len // LANES
    return pl.pallas_call(
        functools.partial(_asel_kernel, tq=tq, cw=cw, s_len=s_len, topk=float(topk)),
        out_shape=jax.ShapeDtypeStruct((bsz, nslab, s_len, LANES), BF16),
        grid=(bsz, s_len // tq),
        in_specs=[pl.BlockSpec((1, tq, nq), lambda b, i: (b, i, 0)),
                  pl.BlockSpec((1, tq, LANES), lambda b, i: (b, i, side)),
                  pl.BlockSpec((1, s_len, LANES), lambda b, i: (b, 0, side), pipeline_mode=pl.Buffered(1))],
        out_specs=pl.BlockSpec((1, nslab, tq, LANES), lambda b, i: (b, 0, i, 0)),
        scratch_shapes=[pltpu.VMEM((nslab, tq, LANES), I32),
                        pltpu.VMEM((IDX_HEADS, tq, LANES), F32),
                        pltpu.VMEM((tq, LANES), I32),
                        pltpu.VMEM((2, tq, LANES), I32),
                        pltpu.VMEM((2, tq, LANES), F32),
                        pltpu.VMEM((tq, nq), BF16),
                        pltpu.VMEM((nslab, tq, LANES), BF16)],
        compiler_params=_cp("arbitrary", "arbitrary"),
        name="a_select",
    )(idx, idx, idx)


def _attn_kernel(qt_ref, kt_ref, tab_ref, q_ref, k_ref, v_ref, msk_ref, bias_ref, o_ref,
                 m_sc, l_sc, acc_sc, mk_sc, *slots, t, kpt, nd_far, hpg):
    step = pl.program_id(1)
    qi = qt_ref[step]
    kp = kt_ref[step]
    d_last = qi - (kp * kpt + kpt - 1)
    n_unit = kpt * A_HEADS
    sub = t // LANES

    @pl.when(kp == 0)
    def _():
        m_sc[...] = jnp.full(m_sc.shape, NEG, F32)
        l_sc[...] = jnp.zeros_like(l_sc)
        acc_sc[...] = jnp.zeros_like(acc_sc)

    mk_sc[...] = jnp.concatenate([msk_ref[0, j] for j in range(kpt * sub)], axis=1).astype(F32)
    ns = 2 * hpg
    s_sc, p_sc, al_sc = slots[:ns], slots[ns:2 * ns], slots[2 * ns:]

    def slot(u):
        return (u // hpg % 2) * hpg + u % hpg

    def scores(u, far):
        h, kt = u % A_HEADS, u // A_HEADS
        keys = slice(kt * t, (kt + 1) * t)
        s = lax.dot_general(q_ref[h, 0], k_ref[h, 0, keys, :], _NT, preferred_element_type=F32) + mk_sc[:, keys]
        if not far:
            off0 = sub * jnp.clip(qi - (kp * kpt + kt), 0, nd_far) + (sub - 1)
            s = s + jnp.concatenate(
                [jnp.concatenate([bias_ref[off0 + a - b, h] for b in range(sub)], axis=1) for a in range(sub)],
                axis=0).astype(F32)
        s_sc[slot(u)][...] = s

    def softmax(u, far):
        h = u % A_HEADS
        s = s_sc[slot(u)][...]
        c = tab_ref[REL_BUCKETS - 1, h] * LOG2E if far else 0.0
        m_prev = m_sc[h]
        m_cur = jnp.broadcast_to(jnp.max(s, axis=-1, keepdims=True), (t, LANES))
        m_new = jnp.maximum(m_prev, m_cur + c)
        alpha = jnp.exp2(m_prev - m_new)
        p = jnp.exp2(s - jnp.concatenate([m_new - c] * (t // LANES), axis=1))
        p_part = p[:, :LANES]
        for j in range(1, t // LANES):
            p_part = p_part + p[:, j * LANES:(j + 1) * LANES]
        l_sc[h] = alpha * l_sc[h] + p_part
        m_sc[h] = m_new
        al_sc[slot(u)][...] = alpha
        p_sc[slot(u)][...] = p.astype(BF16)

    def values(u):
        h, kt = u % A_HEADS, u // A_HEADS
        acc_sc[h] = al_sc[slot(u)][...] * acc_sc[h] + jnp.dot(p_sc[slot(u)][...], v_ref[h, 0, kt * t:(kt + 1) * t, :],
                                                            preferred_element_type=F32)

    one = qt_ref[0] + 1

    def units(far):
        n_grp = n_unit // hpg
        for st in range(n_grp + 2):
            def stage(i, carry, st=st):
                for j in range(hpg):
                    if st < n_grp:
                        scores(st * hpg + j, far)
                    if 1 <= st <= n_grp:
                        softmax((st - 1) * hpg + j, far)
                    if st >= 2:
                        values((st - 2) * hpg + j)
                return carry

            lax.fori_loop(0, one, stage, 0)

    @pl.when(d_last >= nd_far)
    def _():
        units(True)

    @pl.when(d_last < nd_far)
    def _():
        units(False)

    @pl.when(kp == qi // kpt)
    def _():
        for h in range(A_HEADS):
            l = jnp.sum(l_sc[h], axis=-1, keepdims=True)
            o_ref[h, 0] = (acc_sc[h] * (1.0 / l)).astype(o_ref.dtype)


def _a_attention(qkv, mask4, bias, rel_table, *, t, kpt, nd_far, hpg=4):
    _, bsz, s_len, e = qkv.shape
    nq = s_len // t
    qt = np.concatenate([np.full(i // kpt + 1, i, np.int32) for i in range(nq)])
    kt = np.concatenate([np.arange(i // kpt + 1, dtype=np.int32) for i in range(nq)])
    spt = t // LANES
    grid_spec = pltpu.PrefetchScalarGridSpec(
        num_scalar_prefetch=2,
        grid=(bsz, len(qt)),
        in_specs=[
            pl.BlockSpec(memory_space=pltpu.SMEM),
            pl.BlockSpec((A_HEADS, 1, t, e), lambda b, s, qt, kt: (0, b, qt[s], 0)),
            pl.BlockSpec((A_HEADS, 1, kpt * t, e), lambda b, s, qt, kt: (1, b, kt[s], 0)),
            pl.BlockSpec((A_HEADS, 1, kpt * t, e), lambda b, s, qt, kt: (2, b, kt[s], 0)),
            pl.BlockSpec((1, kpt * spt, t, LANES), lambda b, s, qt, kt: (b, kt[s], qt[s], 0)),
            pl.BlockSpec(bias.shape, lambda b, s, qt, kt: (0, 0, 0, 0), pipeline_mode=pl.Buffered(1)),
        ],
        out_specs=pl.BlockSpec((A_HEADS, 1, t, e), lambda b, s, qt, kt: (0, b, qt[s], 0)),
        scratch_shapes=[pltpu.VMEM((A_HEADS, t, LANES), F32),
                        pltpu.VMEM((A_HEADS, t, LANES), F32),
                        pltpu.VMEM((A_HEADS, t, e), F32),
                        pltpu.VMEM((t, kpt * t), F32)]
        + [pltpu.VMEM((t, t), F32)] * (2 * hpg)
        + [pltpu.VMEM((t, t), BF16)] * (2 * hpg)
        + [pltpu.VMEM((t, LANES), F32)] * (2 * hpg),
    )
    return pl.pallas_call(
        functools.partial(_attn_kernel, t=t, kpt=kpt, nd_far=nd_far, hpg=hpg),
        out_shape=jax.ShapeDtypeStruct((A_HEADS, bsz, s_len, e), BF16),
        grid_spec=grid_spec,
        compiler_params=_cp("parallel", "arbitrary"),
        name="a_attention",
    )(jnp.asarray(qt), jnp.asarray(kt), rel_table, qkv, qkv, qkv, mask4, bias)


def _mixer_a(h, g, w_in, w_out, rel_table, bsz, s_len):
    n = h.shape[0]
    d = D_MODEL
    n_idx = IDX_HEADS * IDX_DIM + IDX_DIM + IDX_HEADS
    w_qkv = jnp.concatenate([w_in[:, :d] * (A_HEAD_DIM ** -0.5 * LOG2E), w_in[:, d:3 * d]], axis=1).astype(BF16)
    w_idx = jnp.pad(w_in[:, 3 * d:], ((0, 0), (0, 640 - n_idx))).astype(BF16)
    qkv = _norm_matmul(h, g, w_qkv, out_dtype=BF16, tn=1024, head_major=True, name="a_qkv")
    idx = _norm_matmul(h, g, w_idx, out_dtype=F32, tn=640, name="a_idx")
    qkv = qkv.reshape(3 * A_HEADS, bsz, s_len, A_HEAD_DIM)
    mask4 = _a_select(idx.reshape(bsz, s_len, 640))

    t = min(512, s_len)
    nd_far = 0
    while _rel_bucket_np(np.array(nd_far * t - (t - 1))) < REL_BUCKETS - 1:
        nd_far += 1
    sub = t // LANES
    offs = np.arange(-(sub - 1), sub * nd_far + sub)
    ii = np.arange(LANES)[:, None] - np.arange(LANES)[None, :]
    dist = (offs[:, None, None] * LANES + ii[None]).reshape(len(offs) * LANES, LANES)
    bias = _bias_tiles(rel_table, _rel_bucket_np(dist), BF16, tile=LANES, tr=LANES, scale=LOG2E)
    kpt = 2 if (s_len // t) % 2 == 0 else 1
    o = _a_attention(qkv, mask4, bias, rel_table, t=t, kpt=kpt, nd_far=nd_far)
    return _out_proj(o.reshape(A_HEADS, n, A_HEAD_DIM), w_out.astype(BF16), h, name="a_out")


def _bsp_kernel(u_ref, v_ref, lg_ref, lb_ref, wsp_ref, bsp_ref, wo_ref, r_ref, o_ref, wt_ref, y_ref, *, rows):
    @pl.when(pl.program_id(0) == 0)
    def _():
        r = lax.broadcasted_iota(I32, (B_CHUNK, B_CHUNK), 0)
        c = lax.broadcasted_iota(I32, (B_CHUNK, B_CHUNK), 1)
        for g in range(B_GROUPS):
            wt_ref[g] = jnp.where(r >= c, wsp_ref[g], 0.0).astype(BF16)

    vn = _layernorm(v_ref[...].astype(F32), lg_ref[...], lb_ref[...]).astype(BF16)
    gw = B_HALF // B_GROUPS
    for c in range(rows // B_CHUNK):
        rs = slice(c * B_CHUNK, (c + 1) * B_CHUNK)
        for g in range(B_GROUPS):
            cs = slice(g * gw, (g + 1) * gw)
            sv = jnp.dot(wt_ref[g], vn[rs, cs], preferred_element_type=F32) + bsp_ref[g]
            y_ref[rs, cs] = (u_ref[rs, cs].astype(F32) * sv).astype(BF16)
    o_ref[...] = r_ref[...] + jnp.dot(y_ref[...], wo_ref[...], preferred_element_type=F32)


def _b_spatial_out(z, ln_g, ln_b, w_sp, b_sp, w_out, resid, *, rows=512):
    n, d = resid.shape
    return pl.pallas_call(
        functools.partial(_bsp_kernel, rows=rows),
        out_shape=jax.ShapeDtypeStruct((n, d), F32),
        grid=(n // rows,),
        in_specs=[pl.BlockSpec((rows, B_HALF), lambda i: (i, 0)),
                  pl.BlockSpec((rows, B_HALF), lambda i: (i, 1)),
                  pl.BlockSpec((1, B_HALF), lambda i: (0, 0)),
                  pl.BlockSpec((1, B_HALF), lambda i: (0, 0)),
                  pl.BlockSpec((B_GROUPS, B_CHUNK, B_CHUNK), lambda i: (0, 0, 0)),
                  pl.BlockSpec((B_GROUPS, B_CHUNK, 1), lambda i: (0, 0, 0)),
                  pl.BlockSpec((B_HALF, d), lambda i: (0, 0), pipeline_mode=pl.Buffered(1)),
                  pl.BlockSpec((rows, d), lambda i: (i, 0))],
        out_specs=pl.BlockSpec((rows, d), lambda i: (i, 0)),
        scratch_shapes=[pltpu.VMEM((B_GROUPS, B_CHUNK, B_CHUNK), BF16), pltpu.VMEM((rows, B_HALF), BF16)],
        compiler_params=_cp("arbitrary"),
        name="b_spatial_out",
    )(z, z, ln_g.reshape(1, B_HALF), ln_b.reshape(1, B_HALF), w_sp, b_sp.reshape(B_GROUPS, B_CHUNK, 1),
      w_out, resid)


def _mixer_b(h, g, w_in, b_in, ln_g, ln_b, w_sp, b_sp, w_out):
    z = _norm_matmul(h, g, w_in.astype(BF16), b_in, mode="gelu", out_dtype=BF16, tn=1024, name="b_in")
    return _b_spatial_out(z, ln_g, ln_b, w_sp, b_sp, w_out.astype(BF16), h)


C_HALO = 32


def _conv_kernel(cur_ref, halo_ref, wdw_ref, bdw_ref, lg_ref, lb_ref, w2_ref, b2_ref, r_ref, o_ref,
                 ext_ref, yc_ref, *, ts, rt, ct):
    i = pl.program_id(1)
    ext_ref[C_HALO:, :] = cur_ref[0]

    @pl.when(i == 0)
    def _():
        ext_ref[:C_HALO, :] = jnp.zeros((C_HALO, D_MODEL), F32)

    @pl.when(i > 0)
    def _():
        ext_ref[:C_HALO, :] = halo_ref[0]

    off = C_HALO - (C_KERNEL - 1)

    def tile(it, carry):
        r0 = pl.multiple_of((it // (D_MODEL // ct)) * rt, rt)
        c0 = pl.multiple_of((it % (D_MODEL // ct)) * ct, ct)
        nwin = rt + C_HALO
        win = ext_ref[pl.ds(r0, nwin), pl.ds(c0, ct)]
        acc = jnp.zeros((rt, ct), F32) + bdw_ref[:, pl.ds(c0, ct)]
        for b in range(8):
            rolled = pltpu.roll(win, nwin - (off + b), axis=0)
            for a in range((C_KERNEL - b + 7) // 8):
                j = 8 * a + b
                acc = acc + rolled[8 * a:8 * a + rt] * wdw_ref[j:j + 1, pl.ds(c0, ct)]
        yc_ref[pl.ds(r0, rt), pl.ds(c0, ct)] = acc
        return carry

    lax.fori_loop(0, (ts // rt) * (D_MODEL // ct), tile, 0)
    y = _layernorm(yc_ref[...], lg_ref[...], lb_ref[...])
    y = (y * jax.nn.sigmoid(y)).astype(BF16)
    o_ref[0] = r_ref[0] + jnp.dot(y, w2_ref[...], preferred_element_type=F32) + b2_ref[...]


def _c_conv(y1, w_dw, b_dw, ln_g, ln_b, w2, b2, resid, *, ts=512, rt=128, ct=128):
    bsz, s_len, d = y1.shape
    hb = ts // C_HALO
    vec = lambda a: a.reshape(1, d)
    return pl.pallas_call(
        functools.partial(_conv_kernel, ts=ts, rt=rt, ct=ct),
        out_shape=jax.ShapeDtypeStruct((bsz, s_len, d), F32),
        grid=(bsz, s_len // ts),
        in_specs=[pl.BlockSpec((1, ts, d), lambda b, i: (b, i, 0)),
                  pl.BlockSpec((1, C_HALO, d), lambda b, i: (b, jnp.maximum(i * hb - 1, 0), 0)),
                  pl.BlockSpec((C_KERNEL, d), lambda b, i: (0, 0)),
                  pl.BlockSpec((1, d), lambda b, i: (0, 0)),
                  pl.BlockSpec((1, d), lambda b, i: (0, 0)),
                  pl.BlockSpec((1, d), lambda b, i: (0, 0)),
                  pl.BlockSpec((d, d), lambda b, i: (0, 0)),
                  pl.BlockSpec((1, d), lambda b, i: (0, 0)),
                  pl.BlockSpec((1, ts, d), lambda b, i: (b, i, 0))],
        out_specs=pl.BlockSpec((1, ts, d), lambda b, i: (b, i, 0)),
        scratch_shapes=[pltpu.VMEM((ts + C_HALO, d), F32), pltpu.VMEM((ts, d), F32)],
        compiler_params=_cp("parallel", "parallel"),
        name="c_conv",
    )(y1, y1, w_dw, vec(b_dw), vec(ln_g), vec(ln_b), w2, vec(b2), resid)


def _mixer_c(h, g, w_pw1, b_pw1, w_dw, b_dw, ln_g, ln_b, w_pw2, b_pw2, bsz, s_len):
    n, d = h.shape
    y1 = _norm_matmul(h, g, w_pw1.astype(BF16), b_pw1, mode="glu", out_dtype=F32, tn=512, name="c_pw1")
    out = _c_conv(y1.reshape(bsz, s_len, d), w_dw, b_dw, ln_g, ln_b, w_pw2.astype(BF16), b_pw2,
                  h.reshape(bsz, s_len, d))
    return out.reshape(n, d)


def _dattn_kernel(q_ref, kc_ref, kp_ref, vc_ref, vp_ref, bias_ref, o_ref, lse_ref):
    nb = pl.program_id(2)
    first = jnp.where(nb == 0, NEG, 0.0)
    e = D_HEAD_DIM
    col = lax.broadcasted_iota(I32, (D_BLOCK, 2 * D_BLOCK), 1)
    first_mask = jnp.where(col < D_BLOCK, first, 0.0)
    low = lax.broadcasted_iota(I32, (D_BLOCK, LANES), 1) < e
    for hp in range(D_HEADS // 2):
        ps = slice(hp * LANES, (hp + 1) * LANES)
        q2 = q_ref[:, ps]
        kk = jnp.concatenate([kp_ref[:, ps], kc_ref[:, ps]], axis=0)
        vv = jnp.concatenate([vp_ref[:, ps], vc_ref[:, ps]], axis=0)
        outs, lses = [], []
        for hh in range(2):
            qh = jnp.where(low if hh == 0 else ~low, q2, jnp.zeros_like(q2))
            s = lax.dot_general(qh, kk, _NT, preferred_element_type=F32) * (e ** -0.5) + bias_ref[2 * hp + hh]
            s = s + first_mask
            m = jnp.max(s, axis=-1, keepdims=True)
            p = jnp.exp(s - m)
            l = jnp.sum(p, axis=-1, keepdims=True)
            outs.append(jnp.dot(p.astype(BF16), vv, preferred_element_type=F32) / l)
            lses.append(jnp.broadcast_to(m + jnp.log(l), (D_BLOCK, LANES)))
        o_ref[0, :, ps] = jnp.where(low, outs[0], outs[1])
        lse_ref[0, :, ps] = jnp.where(low, lses[0], lses[1])


def _dproj_kernel(x_ref, g_ref, w_ref, o_ref, xs_ref, *, dil):
    span, d = x_ref.shape
    nslab = d // LANES
    for j in range(nslab):
        xs_ref[j] = x_ref[:, j * LANES:(j + 1) * LANES]
    rows = 4 * D_BLOCK
    for c in range(span // rows):
        xr = jnp.concatenate(
            [jnp.concatenate([xs_ref[j, pl.ds(r, D_BLOCK, stride=dil), :] for j in range(nslab)], axis=1)
             for r in range(4 * c, 4 * c + 4)], axis=0)
        y = jnp.dot(_rms_bf16(xr, g_ref[...]), w_ref[...], preferred_element_type=F32)
        o_ref[c * rows:(c + 1) * rows, :] = y.astype(o_ref.dtype)


def _d_proj(h, g, w, dil):
    n, d = h.shape
    m = w.shape[1]
    span = D_BLOCK * dil
    return pl.pallas_call(
        functools.partial(_dproj_kernel, dil=dil),
        out_shape=jax.ShapeDtypeStruct((n, m), BF16),
        grid=(n // span,),
        in_specs=[pl.BlockSpec((span, d), lambda i: (i, 0)),
                  pl.BlockSpec((1, d), lambda i: (0, 0)),
                  pl.BlockSpec((d, m), lambda i: (0, 0))],
        out_specs=pl.BlockSpec((span, m), lambda i: (i, 0)),
        scratch_shapes=[pltpu.VMEM((d // LANES, span, LANES), F32)],
        compiler_params=_cp("parallel"),
        name=f"d_in_dil{dil}",
    )(h, g.reshape(1, d), w)


def _d_group(proj, bias_g, gidx, dil, bsz, s_len):
    n_sub = s_len // dil
    nb = n_sub // D_BLOCK
    w = D_HEADS * D_HEAD_DIM
    blk = (D_BLOCK, w)
    rb = lambda b, r, n: b * (s_len // D_BLOCK) + n * dil + r
    cur = lambda col: (lambda b, r, n: (rb(b, r, n), col))
    prev = lambda col: (lambda b, r, n: (rb(b, r, jnp.maximum(n - 1, 0)), col))
    out_blk = (1, D_BLOCK, w)
    out_sds = jax.ShapeDtypeStruct((bsz, n_sub, dil * w), F32)
    o, lse = pl.pallas_call(
        _dattn_kernel,
        out_shape=(out_sds, out_sds),
        grid=(bsz, dil, nb),
        in_specs=[pl.BlockSpec(blk, cur(0)),
                  pl.BlockSpec(blk, cur(1)),
                  pl.BlockSpec(blk, prev(1)),
                  pl.BlockSpec(blk, cur(2)),
                  pl.BlockSpec(blk, prev(2)),
                  pl.BlockSpec((D_HEADS, D_BLOCK, 2 * D_BLOCK), lambda b, r, n: (0, 0, 0))],
        out_specs=(pl.BlockSpec(out_blk, lambda b, r, n: (b, n, r)),
                   pl.BlockSpec(out_blk, lambda b, r, n: (b, n, r))),
        compiler_params=_cp("parallel", "parallel", "arbitrary"),
        name=f"d_attn_g{gidx}",
    )(proj, proj, proj, proj, proj, bias_g)
    return o.reshape(bsz * s_len, w), lse.reshape(bsz * s_len, w)


def _dmerge_kernel(o0, o1, o2, l0, l1, l2, w_ref, r_ref, out_ref):
    a0, a1, a2 = l0[...], l1[...], l2[...]
    mx = jnp.maximum(jnp.maximum(a0, a1), a2)
    e0, e1, e2 = jnp.exp(a0 - mx), jnp.exp(a1 - mx), jnp.exp(a2 - mx)
    y = (e0 * o0[...] + e1 * o1[...] + e2 * o2[...]) / (e0 + e1 + e2)
    out_ref[...] = r_ref[...] + jnp.dot(y.astype(BF16), w_ref[...], preferred_element_type=F32)


def _d_merge(outs, lses, w_out, resid, *, tm=512):
    n, w = outs[0].shape
    d = resid.shape[1]
    row = pl.BlockSpec((tm, w), lambda i: (i, 0))
    return pl.pallas_call(
        _dmerge_kernel,
        out_shape=jax.ShapeDtypeStruct((n, d), F32),
        grid=(n // tm,),
        in_specs=[row] * 6 + [pl.BlockSpec((w, d), lambda i: (0, 0)),
                              pl.BlockSpec((tm, d), lambda i: (i, 0))],
        out_specs=pl.BlockSpec((tm, d), lambda i: (i, 0)),
        compiler_params=_cp("parallel"),
        name="d_merge",
    )(*outs, *lses, w_out, resid)


def _mixer_d(h, g, w_in, w_out, rel_table, bsz, s_len):
    w = D_HEADS * D_HEAD_DIM
    ng = len(D_PAIRS)
    p_loc = np.arange(D_BLOCK)[:, None]
    j_loc = np.arange(2 * D_BLOCK)[None, :]
    m = p_loc + D_BLOCK - j_loc
    idx = []
    for window, dil in D_PAIRS:
        steps = window // dil
        idx.append(np.where((m >= 0) & (m <= steps), _rel_bucket_np(m * dil), -1))
    bias = _bias_tiles(rel_table, np.concatenate(idx, 0).astype(np.int32), F32, tile=D_BLOCK, tr=D_BLOCK)
    outs, lses = [], []
    for gi, (_, dil) in enumerate(D_PAIRS):
        cols = [w_in[:, (part * ng + gi) * w:(part * ng + gi + 1) * w] for part in range(3)]
        w_g = jnp.concatenate(cols, axis=1).astype(BF16)
        if dil == 1:
            proj = _norm_matmul(h, g, w_g, out_dtype=BF16, tn=3 * w, name="d_in_dil1")
        else:
            proj = _d_proj(h, g, w_g, dil)
        o, lse = _d_group(proj, bias[gi], gi, dil, bsz, s_len)
        outs.append(o)
        lses.append(lse)
    return _d_merge(outs, lses, w_out.astype(BF16), h)


def kernel(x, norm_g, final_g, ffn_w_in, ffn_w_out, rel_table, a_w_in, a_w_out, b_w_in, b_b_in, b_ln_g, b_ln_b, b_w_sp, b_b_sp, b_w_out, c_w_pw1, c_b_pw1, c_w_dw, c_b_dw, c_ln_g, c_ln_b, c_w_pw2, c_b_pw2, d_w_in, d_w_out):
    bsz, s_len, d = x.shape
    depth = norm_g.shape[0]
    h = x.reshape(bsz * s_len, d)
    for i in range(depth):
        kind, j = i % 4, i // 4
        h = _ffn(h, norm_g[i, 0], ffn_w_in[i, 0].astype(BF16), ffn_w_out[i, 0].astype(BF16))
        g = norm_g[i, 1]
        if kind == 0:
            h = _mixer_a(h, g, a_w_in[j], a_w_out[j], rel_table, bsz, s_len)
        elif kind == 1:
            h = _mixer_b(h, g, b_w_in[j], b_b_in[j], b_ln_g[j], b_ln_b[j], b_w_sp[j], b_b_sp[j], b_w_out[j])
        elif kind == 2:
            h = _mixer_c(h, g, c_w_pw1[j], c_b_pw1[j], c_w_dw[j], c_b_dw[j], c_ln_g[j], c_ln_b[j],
                         c_w_pw2[j], c_b_pw2[j], bsz, s_len)
        else:
            h = _mixer_d(h, g, d_w_in[j], d_w_out[j], rel_table, bsz, s_len)
        h = _ffn(h, norm_g[i, 2], ffn_w_in[i, 1].astype(BF16), ffn_w_out[i, 1].astype(BF16),
                 final_g if i == depth - 1 else None)
    return h.reshape(bsz, s_len, d)
```

```python
import functools
import math

import numpy as np
import jax
import jax.numpy as jnp
from jax import lax
from jax.experimental import pallas as pl
from jax.experimental.pallas import tpu as pltpu

F32, BF16, I32 = jnp.float32, jnp.bfloat16, jnp.int32

D_MODEL = 1024
EPS = 1e-6
D_FF = 2816
REL_BUCKETS = 32
REL_MAX_DIST = 2048
A_HEADS = 8
A_HEAD_DIM = 128
IDX_HEADS = 8
IDX_DIM = 64
TOPK_MAX = 256
B_CHUNK = 128
B_HALF = 3 * D_MODEL
B_GROUPS = 8
C_KERNEL = 31
D_PAIRS = ((128, 1), (512, 4), (2048, 16))
D_HEADS = 8
D_HEAD_DIM = 64
D_BLOCK = 128

LANES = 128
MXU_TILE = 256
VMEM_LIMIT = 56 * 1024 * 1024
NEG = -1e30
INT_MIN = -2 ** 31
LOG2E = math.log2(math.e)

_NT = (((1,), (1,)), ((), ()))


def _cp(*sem):
    return pltpu.CompilerParams(dimension_semantics=sem, vmem_limit_bytes=VMEM_LIMIT)


def _rms_bf16(x, g):
    ms = jnp.mean(x * x, axis=-1, keepdims=True)
    return (x * lax.rsqrt(ms + EPS) * g).astype(BF16)


def _layernorm(x, g, b):
    mu = jnp.mean(x, axis=-1, keepdims=True)
    xc = x - mu
    var = jnp.mean(xc * xc, axis=-1, keepdims=True)
    return xc * lax.rsqrt(var + EPS) * g + b


def _ffn_kernel(x_ref, xnext_ref, g_ref, wi_ref, wo_ref, *refs, chunks):
    o_ref, xn_ref = refs[-2], refs[-1]
    i = pl.program_id(0)
    slot = i % 2

    @pl.when(i == 0)
    def _():
        xn_ref[0] = _rms_bf16(x_ref[...], g_ref[...])

    xn = xn_ref[slot]
    acc = None
    c0 = 0
    for tf in chunks:
        gate = jnp.dot(xn, wi_ref[:, c0:c0 + tf], preferred_element_type=F32)
        up = jnp.dot(xn, wi_ref[:, D_FF + c0:D_FF + c0 + tf], preferred_element_type=F32)
        a = (gate * jax.nn.sigmoid(gate) * up).astype(BF16)
        part = jnp.dot(a, wo_ref[c0:c0 + tf, :], preferred_element_type=F32)
        acc = part if acc is None else acc + part
        c0 += tf
    out = x_ref[...] + 0.5 * acc
    if len(refs) == 3:
        ms = jnp.mean(out * out, axis=-1, keepdims=True)
        out = out * lax.rsqrt(ms + EPS) * refs[0][...]
    o_ref[...] = out
    xn_ref[1 - slot] = _rms_bf16(xnext_ref[...], g_ref[...])


def _ffn(h, g, w_in, w_out, final_g=None, *, tm=512):
    n, d = h.shape
    last = n // tm - 1
    n_tiles = D_FF // MXU_TILE
    chunks = ((n_tiles + 1) // 2 * MXU_TILE, n_tiles // 2 * MXU_TILE)
    resident = dict(pipeline_mode=pl.Buffered(1))
    vec = pl.BlockSpec((1, d), lambda i: (0, 0))
    in_specs = [pl.BlockSpec((tm, d), lambda i: (i, 0)),
                pl.BlockSpec((tm, d), lambda i: (jnp.minimum(i + 1, last), 0)),
                vec,
                pl.BlockSpec((d, 2 * D_FF), lambda i: (0, 0), **resident),
                pl.BlockSpec((D_FF, d), lambda i: (0, 0), **resident)]
    args = [h, h, g.reshape(1, d), w_in, w_out]
    if final_g is not None:
        in_specs.append(vec)
        args.append(final_g.reshape(1, d))
    return pl.pallas_call(
        functools.partial(_ffn_kernel, chunks=chunks),
        out_shape=jax.ShapeDtypeStruct((n, d), F32),
        grid=(n // tm,),
        in_specs=in_specs,
        out_specs=pl.BlockSpec((tm, d), lambda i: (i, 0)),
        scratch_shapes=[pltpu.VMEM((2, tm, d), BF16)],
        compiler_params=_cp("arbitrary"),
        name="ffn",
    )(*args)


def _nm_kernel(mode, has_bias, tn, x_ref, g_ref, w_ref, *refs):
    b_ref = refs[0] if has_bias else None
    o_ref, xn_ref = refs[-2], refs[-1]
    s = pl.program_id(0)
    head_major = len(o_ref.shape) == 3
    m_out = o_ref.shape[0] * LANES if head_major else o_ref.shape[1]

    @pl.when(s == 0)
    def _():
        xn_ref[0] = _rms_bf16(x_ref[...], g_ref[...])

    @pl.when(s > 0)
    def _():
        xn = xn_ref[(s - 1) % 2]

        def column_block(c0):
            y = jnp.dot(xn, w_ref[:, c0:c0 + tn], preferred_element_type=F32)
            return y + b_ref[:, c0:c0 + tn] if has_bias else y

        for j in range(m_out // tn):
            y = column_block(j * tn)
            if mode == "gelu":
                y = jax.nn.gelu(y)
            elif mode == "glu":
                y = y * jax.nn.sigmoid(column_block(m_out + j * tn))
            if head_major:
                for hh in range(tn // LANES):
                    o_ref[j * (tn // LANES) + hh] = y[:, hh * LANES:(hh + 1) * LANES].astype(o_ref.dtype)
            else:
                o_ref[:, j * tn:(j + 1) * tn] = y.astype(o_ref.dtype)
        xn_ref[s % 2] = _rms_bf16(x_ref[...], g_ref[...])


def _norm_matmul(h, g, w, b=None, *, mode="plain", out_dtype=BF16, tm=512, tn=512, head_major=False, name="proj"):
    n, d = h.shape
    m_total = w.shape[1]
    m_out = m_total // 2 if mode == "glu" else m_total
    nt = n // tm
    has_bias = b is not None
    prev = lambda s: jnp.maximum(s - 1, 0)
    if head_major:
        out_shape = jax.ShapeDtypeStruct((m_out // LANES, n, LANES), out_dtype)
        out_spec = pl.BlockSpec((m_out // LANES, tm, LANES), lambda s: (0, prev(s), 0))
    else:
        out_shape = jax.ShapeDtypeStruct((n, m_out), out_dtype)
        out_spec = pl.BlockSpec((tm, m_out), lambda s: (prev(s), 0))
    resident = dict(pipeline_mode=pl.Buffered(1))
    in_specs = [pl.BlockSpec((tm, d), lambda s: (jnp.minimum(s, nt - 1), 0)),
                pl.BlockSpec((1, d), lambda s: (0, 0)),
                pl.BlockSpec((d, m_total), lambda s: (0, 0), **resident)]
    args = [h, g.reshape(1, d), w]
    if has_bias:
        in_specs.append(pl.BlockSpec((1, m_total), lambda s: (0, 0)))
        args.append(b.reshape(1, m_total))
    return pl.pallas_call(
        functools.partial(_nm_kernel, mode, has_bias, tn),
        out_shape=out_shape,
        grid=(nt + 1,),
        in_specs=in_specs,
        out_specs=out_spec,
        scratch_shapes=[pltpu.VMEM((2, tm, d), BF16)],
        compiler_params=_cp("arbitrary"),
        name=name,
    )(*args)


def _op_kernel(y_ref, w_ref, r_ref, o_ref):
    y = jnp.concatenate([y_ref[hh] for hh in range(y_ref.shape[0])], axis=1)
    o_ref[...] = r_ref[...] + jnp.dot(y, w_ref[...], preferred_element_type=F32)


def _out_proj(y, w, resid, *, tm=512, name="out_proj"):
    k, d = w.shape
    n = resid.shape[0]
    return pl.pallas_call(
        _op_kernel,
        out_shape=jax.ShapeDtypeStruct((n, d), F32),
        grid=(n // tm,),
        in_specs=[pl.BlockSpec((k // LANES, tm, LANES), lambda i: (0, i, 0)),
                  pl.BlockSpec((k, d), lambda i: (0, 0)),
                  pl.BlockSpec((tm, d), lambda i: (i, 0))],
        out_specs=pl.BlockSpec((tm, d), lambda i: (i, 0)),
        compiler_params=_cp("parallel"),
        name=name,
    )(y, w, resid)


def _rel_bucket_np(dist):
    max_exact = REL_BUCKETS // 2
    d = np.maximum(dist, 0)
    df = np.maximum(d, 1).astype(np.float32)
    large = max_exact + (np.log(df / np.float32(max_exact)) / np.float32(math.log(REL_MAX_DIST / max_exact))
                         * np.float32(REL_BUCKETS - max_exact)).astype(np.int32)
    large = np.minimum(large, REL_BUCKETS - 1)
    return np.where(d < max_exact, d, large).astype(np.int32)


def _bias_kernel(scale, tab_ref, idx_ref, o_ref):
    idx = idx_ref[...]
    for h in range(o_ref.shape[1]):
        acc = jnp.zeros(idx.shape, F32)
        for b in range(REL_BUCKETS):
            acc = jnp.where(idx == b, tab_ref[b, h] * scale, acc)
        o_ref[0, h] = jnp.where(idx < 0, NEG, acc).astype(o_ref.dtype)


def _bias_tiles(rel_table, bucket_idx, out_dtype, *, tile, tr, scale=1.0):
    r, c = bucket_idx.shape
    nh = rel_table.shape[1]
    per = tile // tr
    return pl.pallas_call(
        functools.partial(_bias_kernel, scale),
        out_shape=jax.ShapeDtypeStruct((r // tile, nh, tile, c), out_dtype),
        grid=(r // tr,),
        in_specs=[pl.BlockSpec(memory_space=pltpu.SMEM),
                  pl.BlockSpec((tr, c), lambda i: (i, 0))],
        out_specs=pl.BlockSpec((1, nh, tr, c), lambda i: (i // per, 0, i % per, 0)),
        compiler_params=_cp("parallel"),
        name="bias_tiles",
    )(rel_table, jnp.asarray(bucket_idx))


def _asel_kernel(qi_ref, side_ref, kall_ref, m_ref, keys_ref, wb_ref, x_ref, si_ref, sf_ref, qb_ref,
                 *, tq, cw, s_len, topk):
    qb = pl.program_id(1)
    spc = cw // LANES
    n_chunks = (qb * tq + tq + cw - 1) // cw
    n_slabs = n_chunks * spc
    nslab_all = s_len // LANES

    qb_ref[...] = qi_ref[0].astype(BF16)
    wi = side_ref[0, :, IDX_DIM:IDX_DIM + IDX_HEADS] * (IDX_HEADS ** -0.5 * IDX_DIM ** -0.5)
    for h in range(IDX_HEADS):
        wb_ref[h] = jnp.broadcast_to(wi[:, h:h + 1], (tq, LANES))
    q_pos = qb * tq + lax.broadcasted_iota(I32, (tq, cw), 0)
    lane_pos = lax.broadcasted_iota(I32, (tq, cw), 1)

    def to_key(x):
        bits = pltpu.bitcast(x, I32)
        return bits ^ ((bits >> 31) & 0x7FFFFFFF)

    def build(c, mx):
        k0 = pl.multiple_of(c * cw, cw)
        kc = kall_ref[0, pl.ds(k0, cw), :][:, :IDX_DIM].astype(BF16)
        sc = jnp.zeros((tq, cw), F32)
        for h in range(IDX_HEADS):
            r = lax.dot_general(qb_ref[:, h * IDX_DIM:(h + 1) * IDX_DIM], kc, _NT,
                                preferred_element_type=F32)
            sc = sc + jnp.maximum(r, 0.0) * jnp.concatenate([wb_ref[h]] * spc, axis=1)
        sc = sc + 0.0
        causal = k0 + lane_pos <= q_pos
        key = jnp.where(causal, to_key(sc), INT_MIN)
        scm = jnp.where(causal, sc, -jnp.inf)
        for j in range(spc):
            keys_ref[c * spc + j] = key[:, j * LANES:(j + 1) * LANES]
            mx = jnp.maximum(mx, scm[:, j * LANES:(j + 1) * LANES])
        return mx

    mx = lax.fori_loop(0, n_chunks, build, jnp.full((tq, LANES), -jnp.inf, F32))
    key_max = to_key(jnp.broadcast_to(jnp.max(mx, axis=1, keepdims=True), (tq, LANES)))

    def lane_sum(acc):
        return jnp.broadcast_to(jnp.sum(acc.astype(F32), axis=1, keepdims=True), (tq, LANES))

    def slab_loop(body, init):
        def chunk(c, carry):
            for j in range(spc):
                carry = body(c * spc + j, carry)
            return carry
        return lax.fori_loop(0, n_chunks, chunk, init)

    zero = jnp.zeros((tq, LANES), I32)

    def count_ge(cand):
        parts = []
        for r0 in range(0, tq, 128):
            rows = slice(r0, r0 + 128)
            cand_r = cand[rows]
            parts.append(slab_loop(lambda c, acc: acc + jnp.where(keys_ref[c, rows, :] >= cand_r, 1, 0), zero[rows]))
        return lane_sum(parts[0] if len(parts) == 1 else jnp.concatenate(parts, axis=0))

    n_real = (q_pos[:, :LANES] + 1).astype(F32)
    lo0 = jnp.full((tq, LANES), INT_MIN + 1, I32)
    si_ref[0] = lo0
    si_ref[1] = jnp.where(n_real > topk, key_max + 1, lo0 + 1)
    sf_ref[0] = n_real
    sf_ref[1] = jnp.zeros((tq, LANES), F32)

    def midpoint(lo, hi):
        return lo + lax.shift_right_logical(hi - lo, 1)

    def search_pass(cand):
        lo, hi, clo, chi = si_ref[0], si_ref[1], sf_ref[0], sf_ref[1]
        cnt = count_ge(cand)
        ge = cnt >= topk
        lo, clo = jnp.where(ge, cand, lo), jnp.where(ge, cnt, clo)
        hi = jnp.where(ge, hi, jnp.maximum(cand, lo + 1))
        hi = jnp.where(clo == topk, lo + 1, hi)
        si_ref[0], si_ref[1] = lo, hi
        sf_ref[0], sf_ref[1] = clo, jnp.where(ge, chi, cnt)
        open_rows = jnp.where((hi - lo) != 1, 1.0, 0.0)
        return jnp.max(open_rows, axis=0, keepdims=True)[0, 0]

    def guided(first):
        lo, hi = si_ref[0], si_ref[1]
        is_open = (hi - lo) != 1
        cand = midpoint(lo, hi)
        cand = jnp.where(is_open & (lo < 0) & (hi == 1), 0, cand)
        cand = jnp.where(is_open & (lo < 0) & (hi > 1), 1, cand)
        if first:
            probe = hi - (1 << 25)
            cand = jnp.where(is_open & (probe > 1), probe, cand)
        return search_pass(cand)

    guided(True)
    guided(False)
    pending = guided(False)

    def search_body(carry):
        it, _ = carry
        return it + 1, search_pass(midpoint(si_ref[0], si_ref[1]))

    lax.while_loop(lambda carry: (carry[0] < 40) & (carry[1] > 0.0), search_body, (jnp.int32(0), pending))
    tau = si_ref[0]
    clo, chi = sf_ref[0], sf_ref[1]

    r_need = topk - chi
    tied = clo > topk
    x_ref[...] = jnp.full((tq, LANES), s_len, I32)
    lane = lax.broadcasted_iota(I32, (tq, LANES), 1)

    @pl.when(jnp.max(jnp.where(tied, 1.0, 0.0)) > 0.0)
    def _():
        nbits = int(math.log2(s_len)) + 1

        def tie_pass(i, x):
            cand = x + lax.shift_left(jnp.int32(1), nbits - 1 - i)

            def body(c, acc):
                idx = c * LANES + lane
                return acc + jnp.where((keys_ref[c] == tau) & (idx < cand), 1, 0)

            cnt = lane_sum(slab_loop(body, zero))
            return jnp.where(cnt < r_need, cand, x)

        x = lax.fori_loop(0, nbits, tie_pass, zero)
        x_ref[...] = jnp.where(tied, x, s_len)

    x_cut = x_ref[...]

    def emit(c, carry):
        k = keys_ref[c]
        idx = c * LANES + lane
        sel = (k > tau) | ((k == tau) & (idx <= x_cut))
        m_ref[0, c] = jnp.where(sel, 0.0, NEG).astype(m_ref.dtype)
        return carry

    slab_loop(emit, 0)

    def fill(c, carry):
        m_ref[0, c] = jnp.full((tq, LANES), NEG, m_ref.dtype)
        return carry

    lax.fori_loop(n_slabs, nslab_all, fill, 0)


def _a_select(idx, *, tq=128, cw=1024):
    bsz, s_len, width = idx.shape
    nq = IDX_HEADS * IDX_DIM
    side = nq // LANES
    assert width == nq + LANES
    topk = min(TOPK_MAX, s_len // 4)
    nslab = s_len // LANES
    return pl.pallas_call(
        functools.partial(_asel_kernel, tq=tq, cw=cw, s_len=s_len, topk=float(topk)),
        out_shape=jax.ShapeDtypeStruct((bsz, nslab, s_len, LANES), BF16),
        grid=(bsz, s_len // tq),
        in_specs=[pl.BlockSpec((1, tq, nq), lambda b, i: (b, i, 0)),
                  pl.BlockSpec((1, tq, LANES), lambda b, i: (b, i, side)),
                  pl.BlockSpec((1, s_len, LANES), lambda b, i: (b, 0, side), pipeline_mode=pl.Buffered(1))],
        out_specs=pl.BlockSpec((1, nslab, tq, LANES), lambda b, i: (b, 0, i, 0)),
        scratch_shapes=[pltpu.VMEM((nslab, tq, LANES), I32),
                        pltpu.VMEM((IDX_HEADS, tq, LANES), F32),
                        pltpu.VMEM((tq, LANES), I32),
                        pltpu.VMEM((2, tq, LANES), I32),
                        pltpu.VMEM((2, tq, LANES), F32),
                        pltpu.VMEM((tq, nq), BF16)],
        compiler_params=_cp("arbitrary", "arbitrary"),
        name="a_select",
    )(idx, idx, idx)


def _attn_kernel(qt_ref, kt_ref, tab_ref, q_ref, k_ref, v_ref, msk_ref, bias_ref, o_ref,
                 m_sc, l_sc, acc_sc, mk_sc, *slots, t, kpt, nd_far, hpg):
    step = pl.program_id(1)
    qi = qt_ref[step]
    kp = kt_ref[step]
    d_last = qi - (kp * kpt + kpt - 1)
    n_unit = kpt * A_HEADS
    sub = t // LANES

    @pl.when(kp == 0)
    def _():
        m_sc[...] = jnp.full(m_sc.shape, NEG, F32)
        l_sc[...] = jnp.zeros_like(l_sc)
        acc_sc[...] = jnp.zeros_like(acc_sc)

    mk_sc[...] = jnp.concatenate([msk_ref[0, j] for j in range(kpt * sub)], axis=1).astype(F32)
    ns = 2 * hpg
    s_sc, p_sc, al_sc = slots[:ns], slots[ns:2 * ns], slots[2 * ns:]

    def slot(u):
        return (u // hpg % 2) * hpg + u % hpg

    def scores(u, far):
        h, kt = u % A_HEADS, u // A_HEADS
        keys = slice(kt * t, (kt + 1) * t)
        s = lax.dot_general(q_ref[h, 0], k_ref[h, 0, keys, :], _NT, preferred_element_type=F32) + mk_sc[:, keys]
        if not far:
            off0 = sub * jnp.clip(qi - (kp * kpt + kt), 0, nd_far) + (sub - 1)
            s = s + jnp.concatenate(
                [jnp.concatenate([bias_ref[off0 + a - b, h] for b in range(sub)], axis=1) for a in range(sub)],
                axis=0).astype(F32)
        s_sc[slot(u)][...] = s

    def softmax(u, far):
        h = u % A_HEADS
        s = s_sc[slot(u)][...]
        c = tab_ref[REL_BUCKETS - 1, h] * LOG2E if far else 0.0
        m_prev = m_sc[h]
        m_cur = jnp.broadcast_to(jnp.max(s, axis=-1, keepdims=True), (t, LANES))
        m_new = jnp.maximum(m_prev, m_cur + c)
        alpha = jnp.exp2(m_prev - m_new)
        p = jnp.exp2(s - jnp.concatenate([m_new - c] * (t // LANES), axis=1))
        p_part = p[:, :LANES]
        for j in range(1, t // LANES):
            p_part = p_part + p[:, j * LANES:(j + 1) * LANES]
        l_sc[h] = alpha * l_sc[h] + p_part
        m_sc[h] = m_new
        al_sc[slot(u)][...] = alpha
        p_sc[slot(u)][...] = p.astype(BF16)

    def values(u):
        h, kt = u % A_HEADS, u // A_HEADS
        acc_sc[h] = al_sc[slot(u)][...] * acc_sc[h] + jnp.dot(p_sc[slot(u)][...], v_ref[h, 0, kt * t:(kt + 1) * t, :],
                                                            preferred_element_type=F32)

    one = qt_ref[0] + 1

    def units(far):
        n_grp = n_unit // hpg
        for st in range(n_grp + 2):
            def stage(i, carry, st=st):
                for j in range(hpg):
                    if st < n_grp:
                        scores(st * hpg + j, far)
                    if 1 <= st <= n_grp:
                        softmax((st - 1) * hpg + j, far)
                    if st >= 2:
                        values((st - 2) * hpg + j)
                return carry

            lax.fori_loop(0, one, stage, 0)

    @pl.when(d_last >= nd_far)
    def _():
        units(True)

    @pl.when(d_last < nd_far)
    def _():
        units(False)

    @pl.when(kp == qi // kpt)
    def _():
        for h in range(A_HEADS):
            l = jnp.sum(l_sc[h], axis=-1, keepdims=True)
            o_ref[h, 0] = (acc_sc[h] * (1.0 / l)).astype(o_ref.dtype)


def _a_attention(qkv, mask4, bias, rel_table, *, t, kpt, nd_far, hpg=4):
    _, bsz, s_len, e = qkv.shape
    nq = s_len // t
    qt = np.concatenate([np.full(i // kpt + 1, i, np.int32) for i in range(nq)])
    kt = np.concatenate([np.arange(i // kpt + 1, dtype=np.int32) for i in range(nq)])
    spt = t // LANES
    grid_spec = pltpu.PrefetchScalarGridSpec(
        num_scalar_prefetch=2,
        grid=(bsz, len(qt)),
        in_specs=[
            pl.BlockSpec(memory_space=pltpu.SMEM),
            pl.BlockSpec((A_HEADS, 1, t, e), lambda b, s, qt, kt: (0, b, qt[s], 0)),
            pl.BlockSpec((A_HEADS, 1, kpt * t, e), lambda b, s, qt, kt: (1, b, kt[s], 0)),
            pl.BlockSpec((A_HEADS, 1, kpt * t, e), lambda b, s, qt, kt: (2, b, kt[s], 0)),
            pl.BlockSpec((1, kpt * spt, t, LANES), lambda b, s, qt, kt: (b, kt[s], qt[s], 0)),
            pl.BlockSpec(bias.shape, lambda b, s, qt, kt: (0, 0, 0, 0), pipeline_mode=pl.Buffered(1)),
        ],
        out_specs=pl.BlockSpec((A_HEADS, 1, t, e), lambda b, s, qt, kt: (0, b, qt[s], 0)),
        scratch_shapes=[pltpu.VMEM((A_HEADS, t, LANES), F32),
                        pltpu.VMEM((A_HEADS, t, LANES), F32),
                        pltpu.VMEM((A_HEADS, t, e), F32),
                        pltpu.VMEM((t, kpt * t), F32)]
        + [pltpu.VMEM((t, t), F32)] * (2 * hpg)
        + [pltpu.VMEM((t, t), BF16)] * (2 * hpg)
        + [pltpu.VMEM((t, LANES), F32)] * (2 * hpg),
    )
    return pl.pallas_call(
        functools.partial(_attn_kernel, t=t, kpt=kpt, nd_far=nd_far, hpg=hpg),
        out_shape=jax.ShapeDtypeStruct((A_HEADS, bsz, s_len, e), BF16),
        grid_spec=grid_spec,
        compiler_params=_cp("parallel", "arbitrary"),
        name="a_attention",
    )(jnp.asarray(qt), jnp.asarray(kt), rel_table, qkv, qkv, qkv, mask4, bias)


def _mixer_a(h, g, w_in, w_out, rel_table, bsz, s_len):
    n = h.shape[0]
    d = D_MODEL
    n_idx = IDX_HEADS * IDX_DIM + IDX_DIM + IDX_HEADS
    w_qkv = jnp.concatenate([w_in[:, :d] * (A_HEAD_DIM ** -0.5 * LOG2E), w_in[:, d:3 * d]], axis=1).astype(BF16)
    w_idx = jnp.pad(w_in[:, 3 * d:], ((0, 0), (0, 640 - n_idx))).astype(BF16)
    qkv = _norm_matmul(h, g, w_qkv, out_dtype=BF16, tn=1024, head_major=True, name="a_qkv")
    idx = _norm_matmul(h, g, w_idx, out_dtype=F32, tn=640, name="a_idx")
    qkv = qkv.reshape(3 * A_HEADS, bsz, s_len, A_HEAD_DIM)
    mask4 = _a_select(idx.reshape(bsz, s_len, 640))

    t = min(512, s_len)
    nd_far = 0
    while _rel_bucket_np(np.array(nd_far * t - (t - 1))) < REL_BUCKETS - 1:
        nd_far += 1
    sub = t // LANES
    offs = np.arange(-(sub - 1), sub * nd_far + sub)
    ii = np.arange(LANES)[:, None] - np.arange(LANES)[None, :]
    dist = (offs[:, None, None] * LANES + ii[None]).reshape(len(offs) * LANES, LANES)
    bias = _bias_tiles(rel_table, _rel_bucket_np(dist), BF16, tile=LANES, tr=LANES, scale=LOG2E)
    kpt = 2 if (s_len // t) % 2 == 0 else 1
    o = _a_attention(qkv, mask4, bias, rel_table, t=t, kpt=kpt, nd_far=nd_far)
    return _out_proj(o.reshape(A_HEADS, n, A_HEAD_DIM), w_out.astype(BF16), h, name="a_out")


def _bsp_kernel(u_ref, v_ref, lg_ref, lb_ref, wsp_ref, bsp_ref, wo_ref, r_ref, o_ref, wt_ref, y_ref, *, rows):
    @pl.when(pl.program_id(0) == 0)
    def _():
        r = lax.broadcasted_iota(I32, (B_CHUNK, B_CHUNK), 0)
        c = lax.broadcasted_iota(I32, (B_CHUNK, B_CHUNK), 1)
        for g in range(B_GROUPS):
            wt_ref[g] = jnp.where(r >= c, wsp_ref[g], 0.0).astype(BF16)

    vn = _layernorm(v_ref[...].astype(F32), lg_ref[...], lb_ref[...]).astype(BF16)
    gw = B_HALF // B_GROUPS
    for c in range(rows // B_CHUNK):
        rs = slice(c * B_CHUNK, (c + 1) * B_CHUNK)
        for g in range(B_GROUPS):
            cs = slice(g * gw, (g + 1) * gw)
            sv = jnp.dot(wt_ref[g], vn[rs, cs], preferred_element_type=F32) + bsp_ref[g]
            y_ref[rs, cs] = (u_ref[rs, cs].astype(F32) * sv).astype(BF16)
    o_ref[...] = r_ref[...] + jnp.dot(y_ref[...], wo_ref[...], preferred_element_type=F32)


def _b_spatial_out(z, ln_g, ln_b, w_sp, b_sp, w_out, resid, *, rows=512):
    n, d = resid.shape
    return pl.pallas_call(
        functools.partial(_bsp_kernel, rows=rows),
        out_shape=jax.ShapeDtypeStruct((n, d), F32),
        grid=(n // rows,),
        in_specs=[pl.BlockSpec((rows, B_HALF), lambda i: (i, 0)),
                  pl.BlockSpec((rows, B_HALF), lambda i: (i, 1)),
                  pl.BlockSpec((1, B_HALF), lambda i: (0, 0)),
                  pl.BlockSpec((1, B_HALF), lambda i: (0, 0)),
                  pl.BlockSpec((B_GROUPS, B_CHUNK, B_CHUNK), lambda i: (0, 0, 0)),
                  pl.BlockSpec((B_GROUPS, B_CHUNK, 1), lambda i: (0, 0, 0)),
                  pl.BlockSpec((B_HALF, d), lambda i: (0, 0), pipeline_mode=pl.Buffered(1)),
                  pl.BlockSpec((rows, d), lambda i: (i, 0))],
        out_specs=pl.BlockSpec((rows, d), lambda i: (i, 0)),
        scratch_shapes=[pltpu.VMEM((B_GROUPS, B_CHUNK, B_CHUNK), BF16), pltpu.VMEM((rows, B_HALF), BF16)],
        compiler_params=_cp("arbitrary"),
        name="b_spatial_out",
    )(z, z, ln_g.reshape(1, B_HALF), ln_b.reshape(1, B_HALF), w_sp, b_sp.reshape(B_GROUPS, B_CHUNK, 1),
      w_out, resid)


def _mixer_b(h, g, w_in, b_in, ln_g, ln_b, w_sp, b_sp, w_out):
    z = _norm_matmul(h, g, w_in.astype(BF16), b_in, mode="gelu", out_dtype=BF16, tn=1024, name="b_in")
    return _b_spatial_out(z, ln_g, ln_b, w_sp, b_sp, w_out.astype(BF16), h)


C_HALO = 32


def _conv_kernel(cur_ref, halo_ref, wdw_ref, bdw_ref, lg_ref, lb_ref, w2_ref, b2_ref, r_ref, o_ref,
                 ext_ref, yc_ref, *, ts, rt, ct):
    i = pl.program_id(1)
    ext_ref[C_HALO:, :] = cur_ref[0]

    @pl.when(i == 0)
    def _():
        ext_ref[:C_HALO, :] = jnp.zeros((C_HALO, D_MODEL), F32)

    @pl.when(i > 0)
    def _():
        ext_ref[:C_HALO, :] = halo_ref[0]

    off = C_HALO - (C_KERNEL - 1)

    def tile(it, carry):
        r0 = pl.multiple_of((it // (D_MODEL // ct)) * rt, rt)
        c0 = pl.multiple_of((it % (D_MODEL // ct)) * ct, ct)
        nwin = rt + C_HALO
        win = ext_ref[pl.ds(r0, nwin), pl.ds(c0, ct)]
        acc = jnp.zeros((rt, ct), F32) + bdw_ref[:, pl.ds(c0, ct)]
        for b in range(8):
            rolled = pltpu.roll(win, nwin - (off + b), axis=0)
            for a in range((C_KERNEL - b + 7) // 8):
                j = 8 * a + b
                acc = acc + rolled[8 * a:8 * a + rt] * wdw_ref[j:j + 1, pl.ds(c0, ct)]
        yc_ref[pl.ds(r0, rt), pl.ds(c0, ct)] = acc
        return carry

    lax.fori_loop(0, (ts // rt) * (D_MODEL // ct), tile, 0)
    y = _layernorm(yc_ref[...], lg_ref[...], lb_ref[...])
    y = (y * jax.nn.sigmoid(y)).astype(BF16)
    o_ref[0] = r_ref[0] + jnp.dot(y, w2_ref[...], preferred_element_type=F32) + b2_ref[...]


def _c_conv(y1, w_dw, b_dw, ln_g, ln_b, w2, b2, resid, *, ts=512, rt=128, ct=128):
    bsz, s_len, d = y1.shape
    hb = ts // C_HALO
    vec = lambda a: a.reshape(1, d)
    return pl.pallas_call(
        functools.partial(_conv_kernel, ts=ts, rt=rt, ct=ct),
        out_shape=jax.ShapeDtypeStruct((bsz, s_len, d), F32),
        grid=(bsz, s_len // ts),
        in_specs=[pl.BlockSpec((1, ts, d), lambda b, i: (b, i, 0)),
                  pl.BlockSpec((1, C_HALO, d), lambda b, i: (b, jnp.maximum(i * hb - 1, 0), 0)),
                  pl.BlockSpec((C_KERNEL, d), lambda b, i: (0, 0)),
                  pl.BlockSpec((1, d), lambda b, i: (0, 0)),
                  pl.BlockSpec((1, d), lambda b, i: (0, 0)),
                  pl.BlockSpec((1, d), lambda b, i: (0, 0)),
                  pl.BlockSpec((d, d), lambda b, i: (0, 0)),
                  pl.BlockSpec((1, d), lambda b, i: (0, 0)),
                  pl.BlockSpec((1, ts, d), lambda b, i: (b, i, 0))],
        out_specs=pl.BlockSpec((1, ts, d), lambda b, i: (b, i, 0)),
        scratch_shapes=[pltpu.VMEM((ts + C_HALO, d), F32), pltpu.VMEM((ts, d), F32)],
        compiler_params=_cp("parallel", "parallel"),
        name="c_conv",
    )(y1, y1, w_dw, vec(b_dw), vec(ln_g), vec(ln_b), w2, vec(b2), resid)


def _mixer_c(h, g, w_pw1, b_pw1, w_dw, b_dw, ln_g, ln_b, w_pw2, b_pw2, bsz, s_len):
    n, d = h.shape
    y1 = _norm_matmul(h, g, w_pw1.astype(BF16), b_pw1, mode="glu", out_dtype=F32, tn=512, name="c_pw1")
    out = _c_conv(y1.reshape(bsz, s_len, d), w_dw, b_dw, ln_g, ln_b, w_pw2.astype(BF16), b_pw2,
                  h.reshape(bsz, s_len, d))
    return out.reshape(n, d)


def _dattn_kernel(q_ref, kc_ref, kp_ref, vc_ref, vp_ref, bias_ref, o_ref, lse_ref):
    nb = pl.program_id(2)
    first = jnp.where(nb == 0, NEG, 0.0)
    e = D_HEAD_DIM
    col = lax.broadcasted_iota(I32, (D_BLOCK, 2 * D_BLOCK), 1)
    first_mask = jnp.where(col < D_BLOCK, first, 0.0)
    low = lax.broadcasted_iota(I32, (D_BLOCK, LANES), 1) < e
    for hp in range(D_HEADS // 2):
        ps = slice(hp * LANES, (hp + 1) * LANES)
        q2 = q_ref[:, ps]
        kk = jnp.concatenate([kp_ref[:, ps], kc_ref[:, ps]], axis=0)
        vv = jnp.concatenate([vp_ref[:, ps], vc_ref[:, ps]], axis=0)
        outs, lses = [], []
        for hh in range(2):
            qh = jnp.where(low if hh == 0 else ~low, q2, jnp.zeros_like(q2))
            s = lax.dot_general(qh, kk, _NT, preferred_element_type=F32) * (e ** -0.5) + bias_ref[2 * hp + hh]
            s = s + first_mask
            m = jnp.max(s, axis=-1, keepdims=True)
            p = jnp.exp(s - m)
            l = jnp.sum(p, axis=-1, keepdims=True)
            outs.append(jnp.dot(p.astype(BF16), vv, preferred_element_type=F32) / l)
            lses.append(jnp.broadcast_to(m + jnp.log(l), (D_BLOCK, LANES)))
        o_ref[0, :, ps] = jnp.where(low, outs[0], outs[1])
        lse_ref[0, :, ps] = jnp.where(low, lses[0], lses[1])


def _dproj_kernel(x_ref, g_ref, w_ref, o_ref, xs_ref, *, dil):
    span, d = x_ref.shape
    nslab = d // LANES
    for j in range(nslab):
        xs_ref[j] = x_ref[:, j * LANES:(j + 1) * LANES]
    rows = 4 * D_BLOCK
    for c in range(span // rows):
        xr = jnp.concatenate(
            [jnp.concatenate([xs_ref[j, pl.ds(r, D_BLOCK, stride=dil), :] for j in range(nslab)], axis=1)
             for r in range(4 * c, 4 * c + 4)], axis=0)
        y = jnp.dot(_rms_bf16(xr, g_ref[...]), w_ref[...], preferred_element_type=F32)
        o_ref[c * rows:(c + 1) * rows, :] = y.astype(o_ref.dtype)


def _d_proj(h, g, w, dil):
    n, d = h.shape
    m = w.shape[1]
    span = D_BLOCK * dil
    return pl.pallas_call(
        functools.partial(_dproj_kernel, dil=dil),
        out_shape=jax.ShapeDtypeStruct((n, m), BF16),
        grid=(n // span,),
        in_specs=[pl.BlockSpec((span, d), lambda i: (i, 0)),
                  pl.BlockSpec((1, d), lambda i: (0, 0)),
                  pl.BlockSpec((d, m), lambda i: (0, 0))],
        out_specs=pl.BlockSpec((span, m), lambda i: (i, 0)),
        scratch_shapes=[pltpu.VMEM((d // LANES, span, LANES), F32)],
        compiler_params=_cp("parallel"),
        name=f"d_in_dil{dil}",
    )(h, g.reshape(1, d), w)


def _d_group(proj, bias_g, gidx, dil, bsz, s_len):
    n_sub = s_len // dil
    nb = n_sub // D_BLOCK
    w = D_HEADS * D_HEAD_DIM
    blk = (D_BLOCK, w)
    rb = lambda b, r, n: b * (s_len // D_BLOCK) + n * dil + r
    cur = lambda col: (lambda b, r, n: (rb(b, r, n), col))
    prev = lambda col: (lambda b, r, n: (rb(b, r, jnp.maximum(n - 1, 0)), col))
    out_blk = (1, D_BLOCK, w)
    out_sds = jax.ShapeDtypeStruct((bsz, n_sub, dil * w), F32)
    o, lse = pl.pallas_call(
        _dattn_kernel,
        out_shape=(out_sds, out_sds),
        grid=(bsz, dil, nb),
        in_specs=[pl.BlockSpec(blk, cur(0)),
                  pl.BlockSpec(blk, cur(1)),
                  pl.BlockSpec(blk, prev(1)),
                  pl.BlockSpec(blk, cur(2)),
                  pl.BlockSpec(blk, prev(2)),
                  pl.BlockSpec((D_HEADS, D_BLOCK, 2 * D_BLOCK), lambda b, r, n: (0, 0, 0))],
        out_specs=(pl.BlockSpec(out_blk, lambda b, r, n: (b, n, r)),
                   pl.BlockSpec(out_blk, lambda b, r, n: (b, n, r))),
        compiler_params=_cp("parallel", "parallel", "arbitrary"),
        name=f"d_attn_g{gidx}",
    )(proj, proj, proj, proj, proj, bias_g)
    return o.reshape(bsz * s_len, w), lse.reshape(bsz * s_len, w)


def _dmerge_kernel(o0, o1, o2, l0, l1, l2, w_ref, r_ref, out_ref):
    a0, a1, a2 = l0[...], l1[...], l2[...]
    mx = jnp.maximum(jnp.maximum(a0, a1), a2)
    e0, e1, e2 = jnp.exp(a0 - mx), jnp.exp(a1 - mx), jnp.exp(a2 - mx)
    y = (e0 * o0[...] + e1 * o1[...] + e2 * o2[...]) / (e0 + e1 + e2)
    out_ref[...] = r_ref[...] + jnp.dot(y.astype(BF16), w_ref[...], preferred_element_type=F32)


def _d_merge(outs, lses, w_out, resid, *, tm=512):
    n, w = outs[0].shape
    d = resid.shape[1]
    row = pl.BlockSpec((tm, w), lambda i: (i, 0))
    return pl.pallas_call(
        _dmerge_kernel,
        out_shape=jax.ShapeDtypeStruct((n, d), F32),
        grid=(n // tm,),
        in_specs=[row] * 6 + [pl.BlockSpec((w, d), lambda i: (0, 0)),
                              pl.BlockSpec((tm, d), lambda i: (i, 0))],
        out_specs=pl.BlockSpec((tm, d), lambda i: (i, 0)),
        compiler_params=_cp("parallel"),
        name="d_merge",
    )(*outs, *lses, w_out, resid)


def _mixer_d(h, g, w_in, w_out, rel_table, bsz, s_len):
    w = D_HEADS * D_HEAD_DIM
    ng = len(D_PAIRS)
    p_loc = np.arange(D_BLOCK)[:, None]
    j_loc = np.arange(2 * D_BLOCK)[None, :]
    m = p_loc + D_BLOCK - j_loc
    idx = []
    for window, dil in D_PAIRS:
        steps = window // dil
        idx.append(np.where((m >= 0) & (m <= steps), _rel_bucket_np(m * dil), -1))
    bias = _bias_tiles(rel_table, np.concatenate(idx, 0).astype(np.int32), F32, tile=D_BLOCK, tr=D_BLOCK)
    outs, lses = [], []
    for gi, (_, dil) in enumerate(D_PAIRS):
        cols = [w_in[:, (part * ng + gi) * w:(part * ng + gi + 1) * w] for part in range(3)]
        w_g = jnp.concatenate(cols, axis=1).astype(BF16)
        if dil == 1:
            proj = _norm_matmul(h, g, w_g, out_dtype=BF16, tn=3 * w, name="d_in_dil1")
        else:
            proj = _d_proj(h, g, w_g, dil)
        o, lse = _d_group(proj, bias[gi], gi, dil, bsz, s_len)
        outs.append(o)
        lses.append(lse)
    return _d_merge(outs, lses, w_out.astype(BF16), h)


def kernel(x, norm_g, final_g, ffn_w_in, ffn_w_out, rel_table, a_w_in, a_w_out, b_w_in, b_b_in, b_ln_g, b_ln_b, b_w_sp, b_b_sp, b_w_out, c_w_pw1, c_b_pw1, c_w_dw, c_b_dw, c_ln_g, c_ln_b, c_w_pw2, c_b_pw2, d_w_in, d_w_out):
    bsz, s_len, d = x.shape
    depth = norm_g.shape[0]
    h = x.reshape(bsz * s_len, d)
    for i in range(depth):
        kind, j = i % 4, i // 4
        h = _ffn(h, norm_g[i, 0], ffn_w_in[i, 0].astype(BF16), ffn_w_out[i, 0].astype(BF16))
        g = norm_g[i, 1]
        if kind == 0:
            h = _mixer_a(h, g, a_w_in[j], a_w_out[j], rel_table, bsz, s_len)
        elif kind == 1:
            h = _mixer_b(h, g, b_w_in[j], b_b_in[j], b_ln_g[j], b_ln_b[j], b_w_sp[j], b_b_sp[j], b_w_out[j])
        elif kind == 2:
            h = _mixer_c(h, g, c_w_pw1[j], c_b_pw1[j], c_w_dw[j], c_b_dw[j], c_ln_g[j], c_ln_b[j],
                         c_w_pw2[j], c_b_pw2[j], bsz, s_len)
        else:
            h = _mixer_d(h, g, d_w_in[j], d_w_out[j], rel_table, bsz, s_len)
        h = _ffn(h, norm_g[i, 2], ffn_w_in[i, 1].astype(BF16), ffn_w_out[i, 1].astype(BF16),
                 final_g if i == depth - 1 else None)
    return h.reshape(bsz, s_len, d)
```

```python
import functools
import math

import numpy as np
import jax
import jax.numpy as jnp
from jax import lax
from jax.experimental import pallas as pl
from jax.experimental.pallas import tpu as pltpu

F32, BF16, I32 = jnp.float32, jnp.bfloat16, jnp.int32

D_MODEL = 1024
EPS = 1e-6
D_FF = 2816
REL_BUCKETS = 32
REL_MAX_DIST = 2048
A_HEADS = 8
A_HEAD_DIM = 128
IDX_HEADS = 8
IDX_DIM = 64
TOPK_MAX = 256
B_CHUNK = 128
B_HALF = 3 * D_MODEL
B_GROUPS = 8
C_KERNEL = 31
D_PAIRS = ((128, 1), (512, 4), (2048, 16))
D_HEADS = 8
D_HEAD_DIM = 64
D_BLOCK = 128

LANES = 128
MXU_TILE = 256
VMEM_LIMIT = 56 * 1024 * 1024
NEG = -1e30
INT_MIN = -2 ** 31
KEY_UNIT = 1 << 16
MIN_NORMAL_TOP = 0x0080
COARSE_PASSES = 8
LOG2E = math.log2(math.e)

_NT = (((1,), (1,)), ((), ()))


def _cp(*sem):
    return pltpu.CompilerParams(dimension_semantics=sem, vmem_limit_bytes=VMEM_LIMIT)


def _rms_bf16(x, g):
    ms = jnp.mean(x * x, axis=-1, keepdims=True)
    return (x * lax.rsqrt(ms + EPS) * g).astype(BF16)


def _layernorm(x, g, b):
    mu = jnp.mean(x, axis=-1, keepdims=True)
    xc = x - mu
    var = jnp.mean(xc * xc, axis=-1, keepdims=True)
    return xc * lax.rsqrt(var + EPS) * g + b


def _ffn_kernel(x_ref, xnext_ref, g_ref, wi_ref, wo_ref, *refs, chunks):
    o_ref, xn_ref = refs[-2], refs[-1]
    i = pl.program_id(0)
    slot = i % 2

    @pl.when(i == 0)
    def _():
        xn_ref[0] = _rms_bf16(x_ref[...], g_ref[...])

    xn = xn_ref[slot]
    acc = None
    c0 = 0
    for tf in chunks:
        gate = jnp.dot(xn, wi_ref[:, c0:c0 + tf], preferred_element_type=F32)
        up = jnp.dot(xn, wi_ref[:, D_FF + c0:D_FF + c0 + tf], preferred_element_type=F32)
        a = (gate * jax.nn.sigmoid(gate) * up).astype(BF16)
        part = jnp.dot(a, wo_ref[c0:c0 + tf, :], preferred_element_type=F32)
        acc = part if acc is None else acc + part
        c0 += tf
    out = x_ref[...] + 0.5 * acc
    if len(refs) == 3:
        ms = jnp.mean(out * out, axis=-1, keepdims=True)
        out = out * lax.rsqrt(ms + EPS) * refs[0][...]
    o_ref[...] = out
    xn_ref[1 - slot] = _rms_bf16(xnext_ref[...], g_ref[...])


def _ffn(h, g, w_in, w_out, final_g=None, *, tm=512):
    n, d = h.shape
    last = n // tm - 1
    n_tiles = D_FF // MXU_TILE
    chunks = ((n_tiles + 1) // 2 * MXU_TILE, n_tiles // 2 * MXU_TILE)
    resident = dict(pipeline_mode=pl.Buffered(1))
    vec = pl.BlockSpec((1, d), lambda i: (0, 0))
    in_specs = [pl.BlockSpec((tm, d), lambda i: (i, 0)),
                pl.BlockSpec((tm, d), lambda i: (jnp.minimum(i + 1, last), 0)),
                vec,
                pl.BlockSpec((d, 2 * D_FF), lambda i: (0, 0), **resident),
                pl.BlockSpec((D_FF, d), lambda i: (0, 0), **resident)]
    args = [h, h, g.reshape(1, d), w_in, w_out]
    if final_g is not None:
        in_specs.append(vec)
        args.append(final_g.reshape(1, d))
    return pl.pallas_call(
        functools.partial(_ffn_kernel, chunks=chunks),
        out_shape=jax.ShapeDtypeStruct((n, d), F32),
        grid=(n // tm,),
        in_specs=in_specs,
        out_specs=pl.BlockSpec((tm, d), lambda i: (i, 0)),
        scratch_shapes=[pltpu.VMEM((2, tm, d), BF16)],
        compiler_params=_cp("arbitrary"),
        name="ffn",
    )(*args)


def _nm_kernel(mode, has_bias, tn, x_ref, g_ref, w_ref, *refs):
    b_ref = refs[0] if has_bias else None
    o_ref, xn_ref = refs[-2], refs[-1]
    s = pl.program_id(0)
    head_major = len(o_ref.shape) == 3
    m_out = o_ref.shape[0] * LANES if head_major else o_ref.shape[1]

    @pl.when(s == 0)
    def _():
        xn_ref[0] = _rms_bf16(x_ref[...], g_ref[...])

    @pl.when(s > 0)
    def _():
        xn = xn_ref[(s - 1) % 2]

        def column_block(c0):
            y = jnp.dot(xn, w_ref[:, c0:c0 + tn], preferred_element_type=F32)
            return y + b_ref[:, c0:c0 + tn] if has_bias else y

        for j in range(m_out // tn):
            y = column_block(j * tn)
            if mode == "gelu":
                y = jax.nn.gelu(y)
            elif mode == "glu":
                y = y * jax.nn.sigmoid(column_block(m_out + j * tn))
            if head_major:
                for hh in range(tn // LANES):
                    o_ref[j * (tn // LANES) + hh] = y[:, hh * LANES:(hh + 1) * LANES].astype(o_ref.dtype)
            else:
                o_ref[:, j * tn:(j + 1) * tn] = y.astype(o_ref.dtype)
        xn_ref[s % 2] = _rms_bf16(x_ref[...], g_ref[...])


def _norm_matmul(h, g, w, b=None, *, mode="plain", out_dtype=BF16, tm=512, tn=512, head_major=False, name="proj"):
    n, d = h.shape
    m_total = w.shape[1]
    m_out = m_total // 2 if mode == "glu" else m_total
    nt = n // tm
    has_bias = b is not None
    prev = lambda s: jnp.maximum(s - 1, 0)
    if head_major:
        out_shape = jax.ShapeDtypeStruct((m_out // LANES, n, LANES), out_dtype)
        out_spec = pl.BlockSpec((m_out // LANES, tm, LANES), lambda s: (0, prev(s), 0))
    else:
        out_shape = jax.ShapeDtypeStruct((n, m_out), out_dtype)
        out_spec = pl.BlockSpec((tm, m_out), lambda s: (prev(s), 0))
    resident = dict(pipeline_mode=pl.Buffered(1))
    in_specs = [pl.BlockSpec((tm, d), lambda s: (jnp.minimum(s, nt - 1), 0)),
                pl.BlockSpec((1, d), lambda s: (0, 0)),
                pl.BlockSpec((d, m_total), lambda s: (0, 0), **resident)]
    args = [h, g.reshape(1, d), w]
    if has_bias:
        in_specs.append(pl.BlockSpec((1, m_total), lambda s: (0, 0)))
        args.append(b.reshape(1, m_total))
    return pl.pallas_call(
        functools.partial(_nm_kernel, mode, has_bias, tn),
        out_shape=out_shape,
        grid=(nt + 1,),
        in_specs=in_specs,
        out_specs=out_spec,
        scratch_shapes=[pltpu.VMEM((2, tm, d), BF16)],
        compiler_params=_cp("arbitrary"),
        name=name,
    )(*args)


def _op_kernel(y_ref, w_ref, r_ref, o_ref):
    y = jnp.concatenate([y_ref[hh] for hh in range(y_ref.shape[0])], axis=1)
    o_ref[...] = r_ref[...] + jnp.dot(y, w_ref[...], preferred_element_type=F32)


def _out_proj(y, w, resid, *, tm=512, name="out_proj"):
    k, d = w.shape
    n = resid.shape[0]
    return pl.pallas_call(
        _op_kernel,
        out_shape=jax.ShapeDtypeStruct((n, d), F32),
        grid=(n // tm,),
        in_specs=[pl.BlockSpec((k // LANES, tm, LANES), lambda i: (0, i, 0)),
                  pl.BlockSpec((k, d), lambda i: (0, 0)),
                  pl.BlockSpec((tm, d), lambda i: (i, 0))],
        out_specs=pl.BlockSpec((tm, d), lambda i: (i, 0)),
        compiler_params=_cp("parallel"),
        name=name,
    )(y, w, resid)


def _rel_bucket_np(dist):
    max_exact = REL_BUCKETS // 2
    d = np.maximum(dist, 0)
    df = np.maximum(d, 1).astype(np.float32)
    large = max_exact + (np.log(df / np.float32(max_exact)) / np.float32(math.log(REL_MAX_DIST / max_exact))
                         * np.float32(REL_BUCKETS - max_exact)).astype(np.int32)
    large = np.minimum(large, REL_BUCKETS - 1)
    return np.where(d < max_exact, d, large).astype(np.int32)


def _bias_kernel(scale, tab_ref, idx_ref, o_ref):
    idx = idx_ref[...]
    for h in range(o_ref.shape[1]):
        acc = jnp.zeros(idx.shape, F32)
        for b in range(REL_BUCKETS):
            acc = jnp.where(idx == b, tab_ref[b, h] * scale, acc)
        o_ref[0, h] = jnp.where(idx < 0, NEG, acc).astype(o_ref.dtype)


def _bias_tiles(rel_table, bucket_idx, out_dtype, *, tile, tr, scale=1.0):
    r, c = bucket_idx.shape
    nh = rel_table.shape[1]
    per = tile // tr
    return pl.pallas_call(
        functools.partial(_bias_kernel, scale),
        out_shape=jax.ShapeDtypeStruct((r // tile, nh, tile, c), out_dtype),
        grid=(r // tr,),
        in_specs=[pl.BlockSpec(memory_space=pltpu.SMEM),
                  pl.BlockSpec((tr, c), lambda i: (i, 0))],
        out_specs=pl.BlockSpec((1, nh, tr, c), lambda i: (i // per, 0, i % per, 0)),
        compiler_params=_cp("parallel"),
        name="bias_tiles",
    )(rel_table, jnp.asarray(bucket_idx))


def _asel_kernel(qi_ref, side_ref, kall_ref, m_ref, keys_ref, wb_ref, x_ref, si_ref, sf_ref, qb_ref, kb_ref,
                 *, tq, cw, s_len, topk):
    qb = pl.program_id(1)
    spc = cw // LANES
    n_chunks = (qb * tq + tq + cw - 1) // cw
    n_slabs = n_chunks * spc
    nslab_all = s_len // LANES

    qb_ref[...] = qi_ref[0].astype(BF16)
    wi = side_ref[0, :, IDX_DIM:IDX_DIM + IDX_HEADS] * (IDX_HEADS ** -0.5 * IDX_DIM ** -0.5)
    for h in range(IDX_HEADS):
        wb_ref[h] = jnp.broadcast_to(wi[:, h:h + 1], (tq, LANES))
    q_pos = qb * tq + lax.broadcasted_iota(I32, (tq, cw), 0)
    lane_pos = lax.broadcasted_iota(I32, (tq, cw), 1)

    def to_key(x):
        bits = pltpu.bitcast(x, I32)
        return bits ^ ((bits >> 31) & 0x7FFFFFFF)

    def build(c, mx):
        k0 = pl.multiple_of(c * cw, cw)
        kc = kall_ref[0, pl.ds(k0, cw), :][:, :IDX_DIM].astype(BF16)
        sc = jnp.zeros((tq, cw), F32)
        for h in range(IDX_HEADS):
            r = lax.dot_general(qb_ref[:, h * IDX_DIM:(h + 1) * IDX_DIM], kc, _NT,
                                preferred_element_type=F32)
            sc = sc + jnp.maximum(r, 0.0) * jnp.concatenate([wb_ref[h]] * spc, axis=1)
        sc = sc + 0.0
        causal = k0 + lane_pos <= q_pos
        key = jnp.where(causal, to_key(sc), INT_MIN)
        scm = jnp.where(causal, sc, -jnp.inf)
        top = pltpu.bitcast(pltpu.bitcast(scm, I32) & -KEY_UNIT, F32).astype(BF16)
        for j in range(spc):
            keys_ref[c * spc + j] = key[:, j * LANES:(j + 1) * LANES]
            kb_ref[c * spc + j] = top[:, j * LANES:(j + 1) * LANES]
            mx = jnp.maximum(mx, scm[:, j * LANES:(j + 1) * LANES])
        return mx

    mx = lax.fori_loop(0, n_chunks, build, jnp.full((tq, LANES), -jnp.inf, F32))
    key_max = to_key(jnp.broadcast_to(jnp.max(mx, axis=1, keepdims=True), (tq, LANES)))

    def lane_sum(acc):
        return jnp.broadcast_to(jnp.sum(acc.astype(F32), axis=1, keepdims=True), (tq, LANES))

    def slab_loop(body, init):
        def chunk(c, carry):
            for j in range(spc):
                carry = body(c * spc + j, carry)
            return carry
        return lax.fori_loop(0, n_chunks, chunk, init)

    zero = jnp.zeros((tq, LANES), I32)

    def count_ge(cand):
        parts = []
        for r0 in range(0, tq, 128):
            rows = slice(r0, r0 + 128)
            cand_r = cand[rows]
            parts.append(slab_loop(lambda c, acc: acc + jnp.where(keys_ref[c, rows, :] >= cand_r, 1, 0), zero[rows]))
        return lane_sum(parts[0] if len(parts) == 1 else jnp.concatenate(parts, axis=0))

    def count_ge_coarse(cand):
        c16 = cand >> 16
        pos = jnp.where((c16 > 0) & (c16 < MIN_NORMAL_TOP), MIN_NORMAL_TOP, c16)
        neg = jnp.minimum(-c16 - 1, 0x7F7F)
        neg = jnp.where(neg < MIN_NORMAL_TOP, 0, neg)
        bits = jnp.where(c16 >= 0, pos << 16, (neg << 16) + INT_MIN)
        cand_b = pltpu.bitcast(bits, F32).astype(BF16)
        one_b, zero_b = jnp.ones((tq, LANES), BF16), jnp.zeros((tq, LANES), BF16)
        acc = slab_loop(lambda c, a: a + jnp.where(kb_ref[c] >= cand_b, one_b, zero_b), zero_b)
        return lane_sum(acc)

    n_real = (q_pos[:, :LANES] + 1).astype(F32)
    lo0 = jnp.full((tq, LANES), INT_MIN, I32)
    si_ref[0] = lo0
    si_ref[1] = jnp.where(n_real > topk, ((key_max >> 16) + 1) << 16, lo0 + KEY_UNIT)
    sf_ref[0] = n_real
    sf_ref[1] = jnp.zeros((tq, LANES), F32)

    def midpoint(lo, hi, unit):
        half = lax.shift_right_logical(hi - lo, 1)
        return lo + (half & -unit)

    def search_pass(cand, unit, count_fn):
        lo, hi, clo, chi = si_ref[0], si_ref[1], sf_ref[0], sf_ref[1]
        cnt = count_fn(cand)
        ge = cnt >= topk
        lo, clo = jnp.where(ge, cand, lo), jnp.where(ge, cnt, clo)
        hi = jnp.where(ge, hi, jnp.maximum(cand, lo + unit))
        hi = jnp.where(clo == topk, lo + unit, hi)
        si_ref[0], si_ref[1] = lo, hi
        sf_ref[0], sf_ref[1] = clo, jnp.where(ge, chi, cnt)
        open_rows = jnp.where((hi - lo) != unit, 1.0, 0.0)
        return jnp.max(open_rows, axis=0, keepdims=True)[0, 0]

    def bisect(unit, count_fn, pending, max_passes):
        def body(carry):
            return carry[0] + 1, search_pass(midpoint(si_ref[0], si_ref[1], unit), unit, count_fn)
        lax.while_loop(lambda carry: (carry[0] < max_passes) & (carry[1] > 0.0), body, (jnp.int32(0), pending))

    def coarse_guided(first):
        lo, hi = si_ref[0], si_ref[1]
        is_open = (hi - lo) != KEY_UNIT
        cand = midpoint(lo, hi, KEY_UNIT)
        cand = jnp.where(is_open & (lo < 0) & (hi > 0), 0, cand)
        if first:
            probe = hi - (1 << 25)
            cand = jnp.where(is_open & (probe > 0), probe, cand)
        return search_pass(cand, KEY_UNIT, count_ge_coarse)

    coarse_guided(True)
    bisect(KEY_UNIT, count_ge_coarse, coarse_guided(False), COARSE_PASSES)

    lo, hi, clo = si_ref[0], si_ref[1], sf_ref[0]
    lo = jnp.maximum(lo, INT_MIN + 1)
    done = (clo <= topk)
    si_ref[0], si_ref[1] = lo, jnp.where(done, lo + 1, hi)
    first_fine = jnp.where((lo == 0) & ~done, 1, midpoint(lo, si_ref[1], 1))
    bisect(1, count_ge, search_pass(first_fine, 1, count_ge), 40)
    tau = si_ref[0]
    clo, chi = sf_ref[0], sf_ref[1]

    r_need = topk - chi
    tied = clo > topk
    x_ref[...] = jnp.full((tq, LANES), s_len, I32)
    lane = lax.broadcasted_iota(I32, (tq, LANES), 1)

    @pl.when(jnp.max(jnp.where(tied, 1.0, 0.0)) > 0.0)
    def _():
        nbits = int(math.log2(s_len)) + 1

        def tie_pass(i, x):
            cand = x + lax.shift_left(jnp.int32(1), nbits - 1 - i)

            def body(c, acc):
                idx = c * LANES + lane
                return acc + jnp.where((keys_ref[c] == tau) & (idx < cand), 1, 0)

            cnt = lane_sum(slab_loop(body, zero))
            return jnp.where(cnt < r_need, cand, x)

        x = lax.fori_loop(0, nbits, tie_pass, zero)
        x_ref[...] = jnp.where(tied, x, s_len)

    x_cut = x_ref[...]

    def emit(c, carry):
        k = keys_ref[c]
        idx = c * LANES + lane
        sel = (k > tau) | ((k == tau) & (idx <= x_cut))
        m_ref[0, c] = jnp.where(sel, 0.0, NEG).astype(m_ref.dtype)
        return carry

    slab_loop(emit, 0)

    def fill(c, carry):
        m_ref[0, c] = jnp.full((tq, LANES), NEG, m_ref.dtype)
        return carry

    lax.fori_loop(n_slabs, nslab_all, fill, 0)


def _a_select(idx, *, tq=128, cw=1024):
    bsz, s_len, width = idx.shape
    nq = IDX_HEADS * IDX_DIM
    side = nq // LANES
    assert width == nq + LANES
    topk = min(TOPK_MAX, s_len // 4)
    nslab = s_len // LANES
    return pl.pallas_call(
        functools.partial(_asel_kernel, tq=tq, cw=cw, s_len=s_len, topk=float(topk)),
        out_shape=jax.ShapeDtypeStruct((bsz, nslab, s_len, LANES), BF16),
        grid=(bsz, s_len // tq),
        in_specs=[pl.BlockSpec((1, tq, nq), lambda b, i: (b, i, 0)),
                  pl.BlockSpec((1, tq, LANES), lambda b, i: (b, i, side)),
                  pl.BlockSpec((1, s_len, LANES), lambda b, i: (b, 0, side), pipeline_mode=pl.Buffered(1))],
        out_specs=pl.BlockSpec((1, nslab, tq, LANES), lambda b, i: (b, 0, i, 0)),
        scratch_shapes=[pltpu.VMEM((nslab, tq, LANES), I32),
                        pltpu.VMEM((IDX_HEADS, tq, LANES), F32),
                        pltpu.VMEM((tq, LANES), I32),
                        pltpu.VMEM((2, tq, LANES), I32),
                        pltpu.VMEM((2, tq, LANES), F32),
                        pltpu.VMEM((tq, nq), BF16),
                        pltpu.VMEM((nslab, tq, LANES), BF16)],
        compiler_params=_cp("arbitrary", "arbitrary"),
        name="a_select",
    )(idx, idx, idx)


def _attn_kernel(qt_ref, kt_ref, tab_ref, q_ref, k_ref, v_ref, msk_ref, bias_ref, o_ref,
                 m_sc, l_sc, acc_sc, mk_sc, *slots, t, kpt, nd_far, hpg):
    step = pl.program_id(1)
    qi = qt_ref[step]
    kp = kt_ref[step]
    d_last = qi - (kp * kpt + kpt - 1)
    n_unit = kpt * A_HEADS
    sub = t // LANES

    @pl.when(kp == 0)
    def _():
        m_sc[...] = jnp.full(m_sc.shape, NEG, F32)
        l_sc[...] = jnp.zeros_like(l_sc)
        acc_sc[...] = jnp.zeros_like(acc_sc)

    mk_sc[...] = jnp.concatenate([msk_ref[0, j] for j in range(kpt * sub)], axis=1).astype(F32)
    ns = 2 * hpg
    s_sc, p_sc, al_sc = slots[:ns], slots[ns:2 * ns], slots[2 * ns:]

    def slot(u):
        return (u // hpg % 2) * hpg + u % hpg

    def scores(u, far):
        h, kt = u % A_HEADS, u // A_HEADS
        keys = slice(kt * t, (kt + 1) * t)
        s = lax.dot_general(q_ref[h, 0], k_ref[h, 0, keys, :], _NT, preferred_element_type=F32) + mk_sc[:, keys]
        if not far:
            off0 = sub * jnp.clip(qi - (kp * kpt + kt), 0, nd_far) + (sub - 1)
            s = s + jnp.concatenate(
                [jnp.concatenate([bias_ref[off0 + a - b, h] for b in range(sub)], axis=1) for a in range(sub)],
                axis=0).astype(F32)
        s_sc[slot(u)][...] = s

    def softmax(u, far):
        h = u % A_HEADS
        s = s_sc[slot(u)][...]
        c = tab_ref[REL_BUCKETS - 1, h] * LOG2E if far else 0.0
        m_prev = m_sc[h]
        m_cur = jnp.broadcast_to(jnp.max(s, axis=-1, keepdims=True), (t, LANES))
        m_new = jnp.maximum(m_prev, m_cur + c)
        alpha = jnp.exp2(m_prev - m_new)
        p = jnp.exp2(s - jnp.concatenate([m_new - c] * (t // LANES), axis=1))
        p_part = p[:, :LANES]
        for j in range(1, t // LANES):
            p_part = p_part + p[:, j * LANES:(j + 1) * LANES]
        l_sc[h] = alpha * l_sc[h] + p_part
        m_sc[h] = m_new
        al_sc[slot(u)][...] = alpha
        p_sc[slot(u)][...] = p.astype(BF16)

    def values(u):
        h, kt = u % A_HEADS, u // A_HEADS
        acc_sc[h] = al_sc[slot(u)][...] * acc_sc[h] + jnp.dot(p_sc[slot(u)][...], v_ref[h, 0, kt * t:(kt + 1) * t, :],
                                                            preferred_element_type=F32)

    one = qt_ref[0] + 1

    def units(far):
        n_grp = n_unit // hpg
        for st in range(n_grp + 2):
            def stage(i, carry, st=st):
                for j in range(hpg):
                    if st < n_grp:
                        scores(st * hpg + j, far)
                    if 1 <= st <= n_grp:
                        softmax((st - 1) * hpg + j, far)
                    if st >= 2:
                        values((st - 2) * hpg + j)
                return carry

            lax.fori_loop(0, one, stage, 0)

    @pl.when(d_last >= nd_far)
    def _():
        units(True)

    @pl.when(d_last < nd_far)
    def _():
        units(False)

    @pl.when(kp == qi // kpt)
    def _():
        for h in range(A_HEADS):
            l = jnp.sum(l_sc[h], axis=-1, keepdims=True)
            o_ref[h, 0] = (acc_sc[h] * (1.0 / l)).astype(o_ref.dtype)


def _a_attention(qkv, mask4, bias, rel_table, *, t, kpt, nd_far, hpg=4):
    _, bsz, s_len, e = qkv.shape
    nq = s_len // t
    qt = np.concatenate([np.full(i // kpt + 1, i, np.int32) for i in range(nq)])
    kt = np.concatenate([np.arange(i // kpt + 1, dtype=np.int32) for i in range(nq)])
    spt = t // LANES
    grid_spec = pltpu.PrefetchScalarGridSpec(
        num_scalar_prefetch=2,
        grid=(bsz, len(qt)),
        in_specs=[
            pl.BlockSpec(memory_space=pltpu.SMEM),
            pl.BlockSpec((A_HEADS, 1, t, e), lambda b, s, qt, kt: (0, b, qt[s], 0)),
            pl.BlockSpec((A_HEADS, 1, kpt * t, e), lambda b, s, qt, kt: (1, b, kt[s], 0)),
            pl.BlockSpec((A_HEADS, 1, kpt * t, e), lambda b, s, qt, kt: (2, b, kt[s], 0)),
            pl.BlockSpec((1, kpt * spt, t, LANES), lambda b, s, qt, kt: (b, kt[s], qt[s], 0)),
            pl.BlockSpec(bias.shape, lambda b, s, qt, kt: (0, 0, 0, 0), pipeline_mode=pl.Buffered(1)),
        ],
        out_specs=pl.BlockSpec((A_HEADS, 1, t, e), lambda b, s, qt, kt: (0, b, qt[s], 0)),
        scratch_shapes=[pltpu.VMEM((A_HEADS, t, LANES), F32),
                        pltpu.VMEM((A_HEADS, t, LANES), F32),
                        pltpu.VMEM((A_HEADS, t, e), F32),
                        pltpu.VMEM((t, kpt * t), F32)]
        + [pltpu.VMEM((t, t), F32)] * (2 * hpg)
        + [pltpu.VMEM((t, t), BF16)] * (2 * hpg)
        + [pltpu.VMEM((t, LANES), F32)] * (2 * hpg),
    )
    return pl.pallas_call(
        functools.partial(_attn_kernel, t=t, kpt=kpt, nd_far=nd_far, hpg=hpg),
        out_shape=jax.ShapeDtypeStruct((A_HEADS, bsz, s_len, e), BF16),
        grid_spec=grid_spec,
        compiler_params=_cp("parallel", "arbitrary"),
        name="a_attention",
    )(jnp.asarray(qt), jnp.asarray(kt), rel_table, qkv, qkv, qkv, mask4, bias)


def _mixer_a(h, g, w_in, w_out, rel_table, bsz, s_len):
    n = h.shape[0]
    d = D_MODEL
    n_idx = IDX_HEADS * IDX_DIM + IDX_DIM + IDX_HEADS
    w_qkv = jnp.concatenate([w_in[:, :d] * (A_HEAD_DIM ** -0.5 * LOG2E), w_in[:, d:3 * d]], axis=1).astype(BF16)
    w_idx = jnp.pad(w_in[:, 3 * d:], ((0, 0), (0, 640 - n_idx))).astype(BF16)
    qkv = _norm_matmul(h, g, w_qkv, out_dtype=BF16, tn=1024, head_major=True, name="a_qkv")
    idx = _norm_matmul(h, g, w_idx, out_dtype=F32, tn=640, name="a_idx")
    qkv = qkv.reshape(3 * A_HEADS, bsz, s_len, A_HEAD_DIM)
    mask4 = _a_select(idx.reshape(bsz, s_len, 640))

    t = min(512, s_len)
    nd_far = 0
    while _rel_bucket_np(np.array(nd_far * t - (t - 1))) < REL_BUCKETS - 1:
        nd_far += 1
    sub = t // LANES
    offs = np.arange(-(sub - 1), sub * nd_far + sub)
    ii = np.arange(LANES)[:, None] - np.arange(LANES)[None, :]
    dist = (offs[:, None, None] * LANES + ii[None]).reshape(len(offs) * LANES, LANES)
    bias = _bias_tiles(rel_table, _rel_bucket_np(dist), BF16, tile=LANES, tr=LANES, scale=LOG2E)
    kpt = 2 if (s_len // t) % 2 == 0 else 1
    o = _a_attention(qkv, mask4, bias, rel_table, t=t, kpt=kpt, nd_far=nd_far)
    return _out_proj(o.reshape(A_HEADS, n, A_HEAD_DIM), w_out.astype(BF16), h, name="a_out")


def _bsp_kernel(u_ref, v_ref, lg_ref, lb_ref, wsp_ref, bsp_ref, wo_ref, r_ref, o_ref, wt_ref, y_ref, *, rows):
    @pl.when(pl.program_id(0) == 0)
    def _():
        r = lax.broadcasted_iota(I32, (B_CHUNK, B_CHUNK), 0)
        c = lax.broadcasted_iota(I32, (B_CHUNK, B_CHUNK), 1)
        for g in range(B_GROUPS):
            wt_ref[g] = jnp.where(r >= c, wsp_ref[g], 0.0).astype(BF16)

    vn = _layernorm(v_ref[...].astype(F32), lg_ref[...], lb_ref[...]).astype(BF16)
    gw = B_HALF // B_GROUPS
    for c in range(rows // B_CHUNK):
        rs = slice(c * B_CHUNK, (c + 1) * B_CHUNK)
        for g in range(B_GROUPS):
            cs = slice(g * gw, (g + 1) * gw)
            sv = jnp.dot(wt_ref[g], vn[rs, cs], preferred_element_type=F32) + bsp_ref[g]
            y_ref[rs, cs] = (u_ref[rs, cs].astype(F32) * sv).astype(BF16)
    o_ref[...] = r_ref[...] + jnp.dot(y_ref[...], wo_ref[...], preferred_element_type=F32)


def _b_spatial_out(z, ln_g, ln_b, w_sp, b_sp, w_out, resid, *, rows=512):
    n, d = resid.shape
    return pl.pallas_call(
        functools.partial(_bsp_kernel, rows=rows),
        out_shape=jax.ShapeDtypeStruct((n, d), F32),
        grid=(n // rows,),
        in_specs=[pl.BlockSpec((rows, B_HALF), lambda i: (i, 0)),
                  pl.BlockSpec((rows, B_HALF), lambda i: (i, 1)),
                  pl.BlockSpec((1, B_HALF), lambda i: (0, 0)),
                  pl.BlockSpec((1, B_HALF), lambda i: (0, 0)),
                  pl.BlockSpec((B_GROUPS, B_CHUNK, B_CHUNK), lambda i: (0, 0, 0)),
                  pl.BlockSpec((B_GROUPS, B_CHUNK, 1), lambda i: (0, 0, 0)),
                  pl.BlockSpec((B_HALF, d), lambda i: (0, 0), pipeline_mode=pl.Buffered(1)),
                  pl.BlockSpec((rows, d), lambda i: (i, 0))],
        out_specs=pl.BlockSpec((rows, d), lambda i: (i, 0)),
        scratch_shapes=[pltpu.VMEM((B_GROUPS, B_CHUNK, B_CHUNK), BF16), pltpu.VMEM((rows, B_HALF), BF16)],
        compiler_params=_cp("arbitrary"),
        name="b_spatial_out",
    )(z, z, ln_g.reshape(1, B_HALF), ln_b.reshape(1, B_HALF), w_sp, b_sp.reshape(B_GROUPS, B_CHUNK, 1),
      w_out, resid)


def _mixer_b(h, g, w_in, b_in, ln_g, ln_b, w_sp, b_sp, w_out):
    z = _norm_matmul(h, g, w_in.astype(BF16), b_in, mode="gelu", out_dtype=BF16, tn=1024, name="b_in")
    return _b_spatial_out(z, ln_g, ln_b, w_sp, b_sp, w_out.astype(BF16), h)


C_HALO = 32


def _conv_kernel(cur_ref, halo_ref, wdw_ref, bdw_ref, lg_ref, lb_ref, w2_ref, b2_ref, r_ref, o_ref,
                 ext_ref, yc_ref, *, ts, rt, ct):
    i = pl.program_id(1)
    ext_ref[C_HALO:, :] = cur_ref[0]

    @pl.when(i == 0)
    def _():
        ext_ref[:C_HALO, :] = jnp.zeros((C_HALO, D_MODEL), F32)

    @pl.when(i > 0)
    def _():
        ext_ref[:C_HALO, :] = halo_ref[0]

    off = C_HALO - (C_KERNEL - 1)

    def tile(it, carry):
        r0 = pl.multiple_of((it // (D_MODEL // ct)) * rt, rt)
        c0 = pl.multiple_of((it % (D_MODEL // ct)) * ct, ct)
        nwin = rt + C_HALO
        win = ext_ref[pl.ds(r0, nwin), pl.ds(c0, ct)]
        acc = jnp.zeros((rt, ct), F32) + bdw_ref[:, pl.ds(c0, ct)]
        for b in range(8):
            rolled = pltpu.roll(win, nwin - (off + b), axis=0)
            for a in range((C_KERNEL - b + 7) // 8):
                j = 8 * a + b
                acc = acc + rolled[8 * a:8 * a + rt] * wdw_ref[j:j + 1, pl.ds(c0, ct)]
        yc_ref[pl.ds(r0, rt), pl.ds(c0, ct)] = acc
        return carry

    lax.fori_loop(0, (ts // rt) * (D_MODEL // ct), tile, 0)
    y = _layernorm(yc_ref[...], lg_ref[...], lb_ref[...])
    y = (y * jax.nn.sigmoid(y)).astype(BF16)
    o_ref[0] = r_ref[0] + jnp.dot(y, w2_ref[...], preferred_element_type=F32) + b2_ref[...]


def _c_conv(y1, w_dw, b_dw, ln_g, ln_b, w2, b2, resid, *, ts=512, rt=128, ct=128):
    bsz, s_len, d = y1.shape
    hb = ts // C_HALO
    vec = lambda a: a.reshape(1, d)
    return pl.pallas_call(
        functools.partial(_conv_kernel, ts=ts, rt=rt, ct=ct),
        out_shape=jax.ShapeDtypeStruct((bsz, s_len, d), F32),
        grid=(bsz, s_len // ts),
        in_specs=[pl.BlockSpec((1, ts, d), lambda b, i: (b, i, 0)),
                  pl.BlockSpec((1, C_HALO, d), lambda b, i: (b, jnp.maximum(i * hb - 1, 0), 0)),
                  pl.BlockSpec((C_KERNEL, d), lambda b, i: (0, 0)),
                  pl.BlockSpec((1, d), lambda b, i: (0, 0)),
                  pl.BlockSpec((1, d), lambda b, i: (0, 0)),
                  pl.BlockSpec((1, d), lambda b, i: (0, 0)),
                  pl.BlockSpec((d, d), lambda b, i: (0, 0)),
                  pl.BlockSpec((1, d), lambda b, i: (0, 0)),
                  pl.BlockSpec((1, ts, d), lambda b, i: (b, i, 0))],
        out_specs=pl.BlockSpec((1, ts, d), lambda b, i: (b, i, 0)),
        scratch_shapes=[pltpu.VMEM((ts + C_HALO, d), F32), pltpu.VMEM((ts, d), F32)],
        compiler_params=_cp("parallel", "parallel"),
        name="c_conv",
    )(y1, y1, w_dw, vec(b_dw), vec(ln_g), vec(ln_b), w2, vec(b2), resid)


def _mixer_c(h, g, w_pw1, b_pw1, w_dw, b_dw, ln_g, ln_b, w_pw2, b_pw2, bsz, s_len):
    n, d = h.shape
    y1 = _norm_matmul(h, g, w_pw1.astype(BF16), b_pw1, mode="glu", out_dtype=F32, tn=512, name="c_pw1")
    out = _c_conv(y1.reshape(bsz, s_len, d), w_dw, b_dw, ln_g, ln_b, w_pw2.astype(BF16), b_pw2,
                  h.reshape(bsz, s_len, d))
    return out.reshape(n, d)


def _dattn_kernel(q_ref, kc_ref, kp_ref, vc_ref, vp_ref, bias_ref, o_ref, lse_ref):
    nb = pl.program_id(2)
    first = jnp.where(nb == 0, NEG, 0.0)
    e = D_HEAD_DIM
    col = lax.broadcasted_iota(I32, (D_BLOCK, 2 * D_BLOCK), 1)
    first_mask = jnp.where(col < D_BLOCK, first, 0.0)
    low = lax.broadcasted_iota(I32, (D_BLOCK, LANES), 1) < e
    for hp in range(D_HEADS // 2):
        ps = slice(hp * LANES, (hp + 1) * LANES)
        q2 = q_ref[:, ps]
        kk = jnp.concatenate([kp_ref[:, ps], kc_ref[:, ps]], axis=0)
        vv = jnp.concatenate([vp_ref[:, ps], vc_ref[:, ps]], axis=0)
        outs, lses = [], []
        for hh in range(2):
            qh = jnp.where(low if hh == 0 else ~low, q2, jnp.zeros_like(q2))
            s = lax.dot_general(qh, kk, _NT, preferred_element_type=F32) * (e ** -0.5) + bias_ref[2 * hp + hh]
            s = s + first_mask
            m = jnp.max(s, axis=-1, keepdims=True)
            p = jnp.exp(s - m)
            l = jnp.sum(p, axis=-1, keepdims=True)
            outs.append(jnp.dot(p.astype(BF16), vv, preferred_element_type=F32) / l)
            lses.append(jnp.broadcast_to(m + jnp.log(l), (D_BLOCK, LANES)))
        o_ref[0, :, ps] = jnp.where(low, outs[0], outs[1])
        lse_ref[0, :, ps] = jnp.where(low, lses[0], lses[1])


def _dproj_kernel(x_ref, g_ref, w_ref, o_ref, xs_ref, *, dil):
    span, d = x_ref.shape
    nslab = d // LANES
    for j in range(nslab):
        xs_ref[j] = x_ref[:, j * LANES:(j + 1) * LANES]
    rows = 4 * D_BLOCK
    for c in range(span // rows):
        xr = jnp.concatenate(
            [jnp.concatenate([xs_ref[j, pl.ds(r, D_BLOCK, stride=dil), :] for j in range(nslab)], axis=1)
             for r in range(4 * c, 4 * c + 4)], axis=0)
        y = jnp.dot(_rms_bf16(xr, g_ref[...]), w_ref[...], preferred_element_type=F32)
        o_ref[c * rows:(c + 1) * rows, :] = y.astype(o_ref.dtype)


def _d_proj(h, g, w, dil):
    n, d = h.shape
    m = w.shape[1]
    span = D_BLOCK * dil
    return pl.pallas_call(
        functools.partial(_dproj_kernel, dil=dil),
        out_shape=jax.ShapeDtypeStruct((n, m), BF16),
        grid=(n // span,),
        in_specs=[pl.BlockSpec((span, d), lambda i: (i, 0)),
                  pl.BlockSpec((1, d), lambda i: (0, 0)),
                  pl.BlockSpec((d, m), lambda i: (0, 0))],
        out_specs=pl.BlockSpec((span, m), lambda i: (i, 0)),
        scratch_shapes=[pltpu.VMEM((d // LANES, span, LANES), F32)],
        compiler_params=_cp("parallel"),
        name=f"d_in_dil{dil}",
    )(h, g.reshape(1, d), w)


def _d_group(proj, bias_g, gidx, dil, bsz, s_len):
    n_sub = s_len // dil
    nb = n_sub // D_BLOCK
    w = D_HEADS * D_HEAD_DIM
    blk = (D_BLOCK, w)
    rb = lambda b, r, n: b * (s_len // D_BLOCK) + n * dil + r
    cur = lambda col: (lambda b, r, n: (rb(b, r, n), col))
    prev = lambda col: (lambda b, r, n: (rb(b, r, jnp.maximum(n - 1, 0)), col))
    out_blk = (1, D_BLOCK, w)
    out_sds = jax.ShapeDtypeStruct((bsz, n_sub, dil * w), F32)
    o, lse = pl.pallas_call(
        _dattn_kernel,
        out_shape=(out_sds, out_sds),
        grid=(bsz, dil, nb),
        in_specs=[pl.BlockSpec(blk, cur(0)),
                  pl.BlockSpec(blk, cur(1)),
                  pl.BlockSpec(blk, prev(1)),
                  pl.BlockSpec(blk, cur(2)),
                  pl.BlockSpec(blk, prev(2)),
                  pl.BlockSpec((D_HEADS, D_BLOCK, 2 * D_BLOCK), lambda b, r, n: (0, 0, 0))],
        out_specs=(pl.BlockSpec(out_blk, lambda b, r, n: (b, n, r)),
                   pl.BlockSpec(out_blk, lambda b, r, n: (b, n, r))),
        compiler_params=_cp("parallel", "parallel", "arbitrary"),
        name=f"d_attn_g{gidx}",
    )(proj, proj, proj, proj, proj, bias_g)
    return o.reshape(bsz * s_len, w), lse.reshape(bsz * s_len, w)


def _dmerge_kernel(o0, o1, o2, l0, l1, l2, w_ref, r_ref, out_ref):
    a0, a1, a2 = l0[...], l1[...], l2[...]
    mx = jnp.maximum(jnp.maximum(a0, a1), a2)
    e0, e1, e2 = jnp.exp(a0 - mx), jnp.exp(a1 - mx), jnp.exp(a2 - mx)
    y = (e0 * o0[...] + e1 * o1[...] + e2 * o2[...]) / (e0 + e1 + e2)
    out_ref[...] = r_ref[...] + jnp.dot(y.astype(BF16), w_ref[...], preferred_element_type=F32)


def _d_merge(outs, lses, w_out, resid, *, tm=512):
    n, w = outs[0].shape
    d = resid.shape[1]
    row = pl.BlockSpec((tm, w), lambda i: (i, 0))
    return pl.pallas_call(
        _dmerge_kernel,
        out_shape=jax.ShapeDtypeStruct((n, d), F32),
        grid=(n // tm,),
        in_specs=[row] * 6 + [pl.BlockSpec((w, d), lambda i: (0, 0)),
                              pl.BlockSpec((tm, d), lambda i: (i, 0))],
        out_specs=pl.BlockSpec((tm, d), lambda i: (i, 0)),
        compiler_params=_cp("parallel"),
        name="d_merge",
    )(*outs, *lses, w_out, resid)


def _mixer_d(h, g, w_in, w_out, rel_table, bsz, s_len):
    w = D_HEADS * D_HEAD_DIM
    ng = len(D_PAIRS)
    p_loc = np.arange(D_BLOCK)[:, None]
    j_loc = np.arange(2 * D_BLOCK)[None, :]
    m = p_loc + D_BLOCK - j_loc
    idx = []
    for window, dil in D_PAIRS:
        steps = window // dil
        idx.append(np.where((m >= 0) & (m <= steps), _rel_bucket_np(m * dil), -1))
    bias = _bias_tiles(rel_table, np.concatenate(idx, 0).astype(np.int32), F32, tile=D_BLOCK, tr=D_BLOCK)
    outs, lses = [], []
    for gi, (_, dil) in enumerate(D_PAIRS):
        cols = [w_in[:, (part * ng + gi) * w:(part * ng + gi + 1) * w] for part in range(3)]
        w_g = jnp.concatenate(cols, axis=1).astype(BF16)
        if dil == 1:
            proj = _norm_matmul(h, g, w_g, out_dtype=BF16, tn=3 * w, name="d_in_dil1")
        else:
            proj = _d_proj(h, g, w_g, dil)
        o, lse = _d_group(proj, bias[gi], gi, dil, bsz, s_len)
        outs.append(o)
        lses.append(lse)
    return _d_merge(outs, lses, w_out.astype(BF16), h)


def kernel(x, norm_g, final_g, ffn_w_in, ffn_w_out, rel_table, a_w_in, a_w_out, b_w_in, b_b_in, b_ln_g, b_ln_b, b_w_sp, b_b_sp, b_w_out, c_w_pw1, c_b_pw1, c_w_dw, c_b_dw, c_ln_g, c_ln_b, c_w_pw2, c_b_pw2, d_w_in, d_w_out):
    bsz, s_len, d = x.shape
    depth = norm_g.shape[0]
    h = x.reshape(bsz * s_len, d)
    for i in range(depth):
        kind, j = i % 4, i // 4
        h = _ffn(h, norm_g[i, 0], ffn_w_in[i, 0].astype(BF16), ffn_w_out[i, 0].astype(BF16))
        g = norm_g[i, 1]
        if kind == 0:
            h = _mixer_a(h, g, a_w_in[j], a_w_out[j], rel_table, bsz, s_len)
        elif kind == 1:
            h = _mixer_b(h, g, b_w_in[j], b_b_in[j], b_ln_g[j], b_ln_b[j], b_w_sp[j], b_b_sp[j], b_w_out[j])
        elif kind == 2:
            h = _mixer_c(h, g, c_w_pw1[j], c_b_pw1[j], c_w_dw[j], c_b_dw[j], c_ln_g[j], c_ln_b[j],
                         c_w_pw2[j], c_b_pw2[j], bsz, s_len)
        else:
            h = _mixer_d(h, g, d_w_in[j], d_w_out[j], rel_table, bsz, s_len)
        h = _ffn(h, norm_g[i, 2], ffn_w_in[i, 1].astype(BF16), ffn_w_out[i, 1].astype(BF16),
                 final_g if i == depth - 1 else None)
    return h.reshape(bsz, s_len, d)
```

```python
import functools
import math

import numpy as np
import jax
import jax.numpy as jnp
from jax import lax
from jax.experimental import pallas as pl
from jax.experimental.pallas import tpu as pltpu

F32, BF16, I32 = jnp.float32, jnp.bfloat16, jnp.int32

D_MODEL = 1024
EPS = 1e-6
D_FF = 2816
REL_BUCKETS = 32
REL_MAX_DIST = 2048
A_HEADS = 8
A_HEAD_DIM = 128
IDX_HEADS = 8
IDX_DIM = 64
TOPK_MAX = 256
B_CHUNK = 128
B_HALF = 3 * D_MODEL
B_GROUPS = 8
C_KERNEL = 31
D_PAIRS = ((128, 1), (512, 4), (2048, 16))
D_HEADS = 8
D_HEAD_DIM = 64
D_BLOCK = 128

LANES = 128
MXU_TILE = 256
VMEM_LIMIT = 56 * 1024 * 1024
NEG = -1e30
INT_MIN = -2 ** 31
LOG2E = math.log2(math.e)

_NT = (((1,), (1,)), ((), ()))


def _cp(*sem):
    return pltpu.CompilerParams(dimension_semantics=sem, vmem_limit_bytes=VMEM_LIMIT)


def _rms_bf16(x, g):
    ms = jnp.mean(x * x, axis=-1, keepdims=True)
    return (x * lax.rsqrt(ms + EPS) * g).astype(BF16)


def _layernorm(x, g, b):
    mu = jnp.mean(x, axis=-1, keepdims=True)
    xc = x - mu
    var = jnp.mean(xc * xc, axis=-1, keepdims=True)
    return xc * lax.rsqrt(var + EPS) * g + b


def _ffn_kernel(x_ref, xnext_ref, g_ref, wi_ref, wo_ref, *refs, chunks):
    o_ref, xn_ref = refs[-2], refs[-1]
    i = pl.program_id(0)
    slot = i % 2

    @pl.when(i == 0)
    def _():
        xn_ref[0] = _rms_bf16(x_ref[...], g_ref[...])

    xn = xn_ref[slot]
    acc = None
    c0 = 0
    for tf in chunks:
        gate = jnp.dot(xn, wi_ref[:, c0:c0 + tf], preferred_element_type=F32)
        up = jnp.dot(xn, wi_ref[:, D_FF + c0:D_FF + c0 + tf], preferred_element_type=F32)
        a = (gate * jax.nn.sigmoid(gate) * up).astype(BF16)
        part = jnp.dot(a, wo_ref[c0:c0 + tf, :], preferred_element_type=F32)
        acc = part if acc is None else acc + part
        c0 += tf
    out = x_ref[...] + 0.5 * acc
    if len(refs) == 3:
        ms = jnp.mean(out * out, axis=-1, keepdims=True)
        out = out * lax.rsqrt(ms + EPS) * refs[0][...]
    o_ref[...] = out
    xn_ref[1 - slot] = _rms_bf16(xnext_ref[...], g_ref[...])


def _ffn(h, g, w_in, w_out, final_g=None, *, tm=512):
    n, d = h.shape
    last = n // tm - 1
    n_tiles = D_FF // MXU_TILE
    chunks = ((n_tiles + 1) // 2 * MXU_TILE, n_tiles // 2 * MXU_TILE)
    resident = dict(pipeline_mode=pl.Buffered(1))
    vec = pl.BlockSpec((1, d), lambda i: (0, 0))
    in_specs = [pl.BlockSpec((tm, d), lambda i: (i, 0)),
                pl.BlockSpec((tm, d), lambda i: (jnp.minimum(i + 1, last), 0)),
                vec,
                pl.BlockSpec((d, 2 * D_FF), lambda i: (0, 0), **resident),
                pl.BlockSpec((D_FF, d), lambda i: (0, 0), **resident)]
    args = [h, h, g.reshape(1, d), w_in, w_out]
    if final_g is not None:
        in_specs.append(vec)
        args.append(final_g.reshape(1, d))
    return pl.pallas_call(
        functools.partial(_ffn_kernel, chunks=chunks),
        out_shape=jax.ShapeDtypeStruct((n, d), F32),
        grid=(n // tm,),
        in_specs=in_specs,
        out_specs=pl.BlockSpec((tm, d), lambda i: (i, 0)),
        scratch_shapes=[pltpu.VMEM((2, tm, d), BF16)],
        compiler_params=_cp("arbitrary"),
        name="ffn",
    )(*args)


def _nm_kernel(mode, has_bias, tn, x_ref, g_ref, w_ref, *refs):
    b_ref = refs[0] if has_bias else None
    o_ref, xn_ref = refs[-2], refs[-1]
    s = pl.program_id(0)
    head_major = len(o_ref.shape) == 3
    m_out = o_ref.shape[0] * LANES if head_major else o_ref.shape[1]

    @pl.when(s == 0)
    def _():
        xn_ref[0] = _rms_bf16(x_ref[...], g_ref[...])

    @pl.when(s > 0)
    def _():
        xn = xn_ref[(s - 1) % 2]

        def column_block(c0):
            y = jnp.dot(xn, w_ref[:, c0:c0 + tn], preferred_element_type=F32)
            return y + b_ref[:, c0:c0 + tn] if has_bias else y

        for j in range(m_out // tn):
            y = column_block(j * tn)
            if mode == "gelu":
                y = jax.nn.gelu(y)
            elif mode == "glu":
                y = y * jax.nn.sigmoid(column_block(m_out + j * tn))
            if head_major:
                for hh in range(tn // LANES):
                    o_ref[j * (tn // LANES) + hh] = y[:, hh * LANES:(hh + 1) * LANES].astype(o_ref.dtype)
            else:
                o_ref[:, j * tn:(j + 1) * tn] = y.astype(o_ref.dtype)
        xn_ref[s % 2] = _rms_bf16(x_ref[...], g_ref[...])


def _norm_matmul(h, g, w, b=None, *, mode="plain", out_dtype=BF16, tm=512, tn=512, head_major=False, name="proj"):
    n, d = h.shape
    m_total = w.shape[1]
    m_out = m_total // 2 if mode == "glu" else m_total
    nt = n // tm
    has_bias = b is not None
    prev = lambda s: jnp.maximum(s - 1, 0)
    if head_major:
        out_shape = jax.ShapeDtypeStruct((m_out // LANES, n, LANES), out_dtype)
        out_spec = pl.BlockSpec((m_out // LANES, tm, LANES), lambda s: (0, prev(s), 0))
    else:
        out_shape = jax.ShapeDtypeStruct((n, m_out), out_dtype)
        out_spec = pl.BlockSpec((tm, m_out), lambda s: (prev(s), 0))
    resident = dict(pipeline_mode=pl.Buffered(1))
    in_specs = [pl.BlockSpec((tm, d), lambda s: (jnp.minimum(s, nt - 1), 0)),
                pl.BlockSpec((1, d), lambda s: (0, 0)),
                pl.BlockSpec((d, m_total), lambda s: (0, 0), **resident)]
    args = [h, g.reshape(1, d), w]
    if has_bias:
        in_specs.append(pl.BlockSpec((1, m_total), lambda s: (0, 0)))
        args.append(b.reshape(1, m_total))
    return pl.pallas_call(
        functools.partial(_nm_kernel, mode, has_bias, tn),
        out_shape=out_shape,
        grid=(nt + 1,),
        in_specs=in_specs,
        out_specs=out_spec,
        scratch_shapes=[pltpu.VMEM((2, tm, d), BF16)],
        compiler_params=_cp("arbitrary"),
        name=name,
    )(*args)


def _op_kernel(y_ref, w_ref, r_ref, o_ref):
    y = jnp.concatenate([y_ref[hh] for hh in range(y_ref.shape[0])], axis=1)
    o_ref[...] = r_ref[...] + jnp.dot(y, w_ref[...], preferred_element_type=F32)


def _out_proj(y, w, resid, *, tm=512, name="out_proj"):
    k, d = w.shape
    n = resid.shape[0]
    return pl.pallas_call(
        _op_kernel,
        out_shape=jax.ShapeDtypeStruct((n, d), F32),
        grid=(n // tm,),
        in_specs=[pl.BlockSpec((k // LANES, tm, LANES), lambda i: (0, i, 0)),
                  pl.BlockSpec((k, d), lambda i: (0, 0)),
                  pl.BlockSpec((tm, d), lambda i: (i, 0))],
        out_specs=pl.BlockSpec((tm, d), lambda i: (i, 0)),
        compiler_params=_cp("parallel"),
        name=name,
    )(y, w, resid)


def _rel_bucket_np(dist):
    max_exact = REL_BUCKETS // 2
    d = np.maximum(dist, 0)
    df = np.maximum(d, 1).astype(np.float32)
    large = max_exact + (np.log(df / np.float32(max_exact)) / np.float32(math.log(REL_MAX_DIST / max_exact))
                         * np.float32(REL_BUCKETS - max_exact)).astype(np.int32)
    large = np.minimum(large, REL_BUCKETS - 1)
    return np.where(d < max_exact, d, large).astype(np.int32)


def _bias_kernel(scale, tab_ref, idx_ref, o_ref):
    idx = idx_ref[...]
    for h in range(o_ref.shape[1]):
        acc = jnp.zeros(idx.shape, F32)
        for b in range(REL_BUCKETS):
            acc = jnp.where(idx == b, tab_ref[b, h] * scale, acc)
        o_ref[0, h] = jnp.where(idx < 0, NEG, acc).astype(o_ref.dtype)


def _bias_tiles(rel_table, bucket_idx, out_dtype, *, tile, tr, scale=1.0):
    r, c = bucket_idx.shape
    nh = rel_table.shape[1]
    per = tile // tr
    return pl.pallas_call(
        functools.partial(_bias_kernel, scale),
        out_shape=jax.ShapeDtypeStruct((r // tile, nh, tile, c), out_dtype),
        grid=(r // tr,),
        in_specs=[pl.BlockSpec(memory_space=pltpu.SMEM),
                  pl.BlockSpec((tr, c), lambda i: (i, 0))],
        out_specs=pl.BlockSpec((1, nh, tr, c), lambda i: (i // per, 0, i % per, 0)),
        compiler_params=_cp("parallel"),
        name="bias_tiles",
    )(rel_table, jnp.asarray(bucket_idx))


def _asel_kernel(qi_ref, side_ref, kall_ref, m_ref, keys_ref, wb_ref, x_ref, si_ref, sf_ref, qb_ref,
                 *, tq, cw, s_len, topk):
    qb = pl.program_id(1)
    spc = cw // LANES
    n_chunks = (qb * tq + tq + cw - 1) // cw
    n_slabs = n_chunks * spc
    nslab_all = s_len // LANES

    qb_ref[...] = qi_ref[0].astype(BF16)
    wi = side_ref[0, :, IDX_DIM:IDX_DIM + IDX_HEADS] * (IDX_HEADS ** -0.5 * IDX_DIM ** -0.5)
    for h in range(IDX_HEADS):
        wb_ref[h] = jnp.broadcast_to(wi[:, h:h + 1], (tq, LANES))
    q_pos = qb * tq + lax.broadcasted_iota(I32, (tq, cw), 0)
    lane_pos = lax.broadcasted_iota(I32, (tq, cw), 1)

    def to_key(x):
        bits = pltpu.bitcast(x, I32)
        return bits ^ ((bits >> 31) & 0x7FFFFFFF)

    def build(c, mx):
        k0 = pl.multiple_of(c * cw, cw)
        kc = kall_ref[0, pl.ds(k0, cw), :][:, :IDX_DIM].astype(BF16)
        sc = jnp.zeros((tq, cw), F32)
        for h in range(IDX_HEADS):
            r = lax.dot_general(qb_ref[:, h * IDX_DIM:(h + 1) * IDX_DIM], kc, _NT,
                                preferred_element_type=F32)
            sc = sc + jnp.maximum(r, 0.0) * jnp.concatenate([wb_ref[h]] * spc, axis=1)
        sc = sc + 0.0
        causal = k0 + lane_pos <= q_pos
        key = jnp.where(causal, to_key(sc), INT_MIN)
        scm = jnp.where(causal, sc, -jnp.inf)
        for j in range(spc):
            keys_ref[c * spc + j] = key[:, j * LANES:(j + 1) * LANES]
            mx = jnp.maximum(mx, scm[:, j * LANES:(j + 1) * LANES])
        return mx

    mx = lax.fori_loop(0, n_chunks, build, jnp.full((tq, LANES), -jnp.inf, F32))
    key_max = to_key(jnp.broadcast_to(jnp.max(mx, axis=1, keepdims=True), (tq, LANES)))

    def lane_sum(acc):
        return jnp.broadcast_to(jnp.sum(acc.astype(F32), axis=1, keepdims=True), (tq, LANES))

    def slab_loop(body, init):
        def chunk(c, carry):
            for j in range(spc):
                carry = body(c * spc + j, carry)
            return carry
        return lax.fori_loop(0, n_chunks, chunk, init)

    zero = jnp.zeros((tq, LANES), I32)

    def count_ge(cand):
        parts = []
        for r0 in range(0, tq, 128):
            rows = slice(r0, r0 + 128)
            cand_r = cand[rows]
            parts.append(slab_loop(lambda c, acc: acc + jnp.where(keys_ref[c, rows, :] >= cand_r, 1, 0), zero[rows]))
        return lane_sum(parts[0] if len(parts) == 1 else jnp.concatenate(parts, axis=0))

    n_real = (q_pos[:, :LANES] + 1).astype(F32)
    lo0 = jnp.full((tq, LANES), INT_MIN + 1, I32)
    si_ref[0] = lo0
    si_ref[1] = jnp.where(n_real > topk, key_max + 1, lo0 + 1)
    sf_ref[0] = n_real
    sf_ref[1] = jnp.zeros((tq, LANES), F32)

    def midpoint(lo, hi):
        return lo + lax.shift_right_logical(hi - lo, 1)

    def search_pass(cand):
        lo, hi, clo, chi = si_ref[0], si_ref[1], sf_ref[0], sf_ref[1]
        cnt = count_ge(cand)
        ge = cnt >= topk
        lo, clo = jnp.where(ge, cand, lo), jnp.where(ge, cnt, clo)
        hi = jnp.where(ge, hi, jnp.maximum(cand, lo + 1))
        hi = jnp.where(clo == topk, lo + 1, hi)
        si_ref[0], si_ref[1] = lo, hi
        sf_ref[0], sf_ref[1] = clo, jnp.where(ge, chi, cnt)
        open_rows = jnp.where((hi - lo) != 1, 1.0, 0.0)
        return jnp.max(open_rows, axis=0, keepdims=True)[0, 0]

    def guided(first):
        lo, hi = si_ref[0], si_ref[1]
        is_open = (hi - lo) != 1
        cand = midpoint(lo, hi)
        cand = jnp.where(is_open & (lo < 0) & (hi == 1), 0, cand)
        cand = jnp.where(is_open & (lo < 0) & (hi > 1), 1, cand)
        if first:
            probe = hi - (1 << 25)
            cand = jnp.where(is_open & (probe > 1), probe, cand)
        return search_pass(cand)

    guided(True)
    guided(False)
    pending = guided(False)

    def search_body(carry):
        it, _ = carry
        return it + 1, search_pass(midpoint(si_ref[0], si_ref[1]))

    lax.while_loop(lambda carry: (carry[0] < 40) & (carry[1] > 0.0), search_body, (jnp.int32(0), pending))
    tau = si_ref[0]
    clo, chi = sf_ref[0], sf_ref[1]

    r_need = topk - chi
    tied = clo > topk
    x_ref[...] = jnp.full((tq, LANES), s_len, I32)
    lane = lax.broadcasted_iota(I32, (tq, LANES), 1)

    @pl.when(jnp.max(jnp.where(tied, 1.0, 0.0)) > 0.0)
    def _():
        nbits = int(math.log2(s_len)) + 1

        def tie_pass(i, x):
            cand = x + lax.shift_left(jnp.int32(1), nbits - 1 - i)

            def body(c, acc):
                idx = c * LANES + lane
                return acc + jnp.where((keys_ref[c] == tau) & (idx < cand), 1, 0)

            cnt = lane_sum(slab_loop(body, zero))
            return jnp.where(cnt < r_need, cand, x)

        x = lax.fori_loop(0, nbits, tie_pass, zero)
        x_ref[...] = jnp.where(tied, x, s_len)

    x_cut = x_ref[...]

    def emit(c, carry):
        k = keys_ref[c]
        idx = c * LANES + lane
        sel = (k > tau) | ((k == tau) & (idx <= x_cut))
        m_ref[0, c] = jnp.where(sel, 0.0, NEG).astype(m_ref.dtype)
        return carry

    slab_loop(emit, 0)

    def fill(c, carry):
        m_ref[0, c] = jnp.full((tq, LANES), NEG, m_ref.dtype)
        return carry

    lax.fori_loop(n_slabs, nslab_all, fill, 0)


def _a_select(idx, *, tq=128, cw=1024):
    bsz, s_len, width = idx.shape
    nq = IDX_HEADS * IDX_DIM
    side = nq // LANES
    assert width == nq + LANES
    topk = min(TOPK_MAX, s_len // 4)
    nslab = s_len // LANES
    return pl.pallas_call(
        functools.partial(_asel_kernel, tq=tq, cw=cw, s_len=s_len, topk=float(topk)),
        out_shape=jax.ShapeDtypeStruct((bsz, nslab, s_len, LANES), F32),
        grid=(bsz, s_len // tq),
        in_specs=[pl.BlockSpec((1, tq, nq), lambda b, i: (b, i, 0)),
                  pl.BlockSpec((1, tq, LANES), lambda b, i: (b, i, side)),
                  pl.BlockSpec((1, s_len, LANES), lambda b, i: (b, 0, side), pipeline_mode=pl.Buffered(1))],
        out_specs=pl.BlockSpec((1, nslab, tq, LANES), lambda b, i: (b, 0, i, 0)),
        scratch_shapes=[pltpu.VMEM((nslab, tq, LANES), I32),
                        pltpu.VMEM((IDX_HEADS, tq, LANES), F32),
                        pltpu.VMEM((tq, LANES), I32),
                        pltpu.VMEM((2, tq, LANES), I32),
                        pltpu.VMEM((2, tq, LANES), F32),
                        pltpu.VMEM((tq, nq), BF16)],
        compiler_params=_cp("arbitrary", "arbitrary"),
        name="a_select",
    )(idx, idx, idx)


def _attn_kernel(qt_ref, kt_ref, tab_ref, q_ref, k_ref, v_ref, msk_ref, bias_ref, o_ref,
                 m_sc, l_sc, acc_sc, *slots, t, kpt, nd_far, hpg):
    step = pl.program_id(1)
    qi = qt_ref[step]
    kp = kt_ref[step]
    d_last = qi - (kp * kpt + kpt - 1)
    n_unit = kpt * A_HEADS
    sub = t // LANES

    @pl.when(kp == 0)
    def _():
        m_sc[...] = jnp.full(m_sc.shape, NEG, F32)
        l_sc[...] = jnp.zeros_like(l_sc)
        acc_sc[...] = jnp.zeros_like(acc_sc)

    ns = 2 * hpg
    s_sc, p_sc, al_sc = slots[:ns], slots[ns:2 * ns], slots[2 * ns:]

    def slot(u):
        return (u // hpg % 2) * hpg + u % hpg

    def scores(u, far):
        h, kt = u % A_HEADS, u // A_HEADS
        keys = slice(kt * t, (kt + 1) * t)
        mask = jnp.concatenate([msk_ref[0, kt * sub + j] for j in range(sub)], axis=1)
        s = lax.dot_general(q_ref[h, 0], k_ref[h, 0, keys, :], _NT, preferred_element_type=F32) + mask
        if not far:
            off0 = sub * jnp.clip(qi - (kp * kpt + kt), 0, nd_far) + (sub - 1)
            s = s + jnp.concatenate(
                [jnp.concatenate([bias_ref[off0 + a - b, h] for b in range(sub)], axis=1) for a in range(sub)],
                axis=0).astype(F32)
        s_sc[slot(u)][...] = s

    def softmax(u, far):
        h = u % A_HEADS
        s = s_sc[slot(u)][...]
        c = tab_ref[REL_BUCKETS - 1, h] * LOG2E if far else 0.0
        m_prev = m_sc[h]
        m_cur = jnp.broadcast_to(jnp.max(s, axis=-1, keepdims=True), (t, LANES))
        m_new = jnp.maximum(m_prev, m_cur + c)
        alpha = jnp.exp2(m_prev - m_new)
        p = jnp.exp2(s - jnp.concatenate([m_new - c] * (t // LANES), axis=1))
        p_part = p[:, :LANES]
        for j in range(1, t // LANES):
            p_part = p_part + p[:, j * LANES:(j + 1) * LANES]
        l_sc[h] = alpha * l_sc[h] + p_part
        m_sc[h] = m_new
        al_sc[slot(u)][...] = alpha
        p_sc[slot(u)][...] = p.astype(BF16)

    def values(u):
        h, kt = u % A_HEADS, u // A_HEADS
        acc_sc[h] = al_sc[slot(u)][...] * acc_sc[h] + jnp.dot(p_sc[slot(u)][...], v_ref[h, 0, kt * t:(kt + 1) * t, :],
                                                            preferred_element_type=F32)

    one = qt_ref[0] + 1

    def units(far):
        n_grp = n_unit // hpg
        for st in range(n_grp + 2):
            def stage(i, carry, st=st):
                for j in range(hpg):
                    if st < n_grp:
                        scores(st * hpg + j, far)
                    if 1 <= st <= n_grp:
                        softmax((st - 1) * hpg + j, far)
                    if st >= 2:
                        values((st - 2) * hpg + j)
                return carry

            lax.fori_loop(0, one, stage, 0)

    @pl.when(d_last >= nd_far)
    def _():
        units(True)

    @pl.when(d_last < nd_far)
    def _():
        units(False)

    @pl.when(kp == qi // kpt)
    def _():
        for h in range(A_HEADS):
            l = jnp.sum(l_sc[h], axis=-1, keepdims=True)
            o_ref[h, 0] = (acc_sc[h] * (1.0 / l)).astype(o_ref.dtype)


def _a_attention(qkv, mask4, bias, rel_table, *, t, kpt, nd_far, hpg=4):
    _, bsz, s_len, e = qkv.shape
    nq = s_len // t
    qt = np.concatenate([np.full(i // kpt + 1, i, np.int32) for i in range(nq)])
    kt = np.concatenate([np.arange(i // kpt + 1, dtype=np.int32) for i in range(nq)])
    spt = t // LANES
    grid_spec = pltpu.PrefetchScalarGridSpec(
        num_scalar_prefetch=2,
        grid=(bsz, len(qt)),
        in_specs=[
            pl.BlockSpec(memory_space=pltpu.SMEM),
            pl.BlockSpec((A_HEADS, 1, t, e), lambda b, s, qt, kt: (0, b, qt[s], 0)),
            pl.BlockSpec((A_HEADS, 1, kpt * t, e), lambda b, s, qt, kt: (1, b, kt[s], 0)),
            pl.BlockSpec((A_HEADS, 1, kpt * t, e), lambda b, s, qt, kt: (2, b, kt[s], 0)),
            pl.BlockSpec((1, kpt * spt, t, LANES), lambda b, s, qt, kt: (b, kt[s], qt[s], 0)),
            pl.BlockSpec(bias.shape, lambda b, s, qt, kt: (0, 0, 0, 0), pipeline_mode=pl.Buffered(1)),
        ],
        out_specs=pl.BlockSpec((A_HEADS, 1, t, e), lambda b, s, qt, kt: (0, b, qt[s], 0)),
        scratch_shapes=[pltpu.VMEM((A_HEADS, t, LANES), F32),
                        pltpu.VMEM((A_HEADS, t, LANES), F32),
                        pltpu.VMEM((A_HEADS, t, e), F32)]
        + [pltpu.VMEM((t, t), F32)] * (2 * hpg)
        + [pltpu.VMEM((t, t), BF16)] * (2 * hpg)
        + [pltpu.VMEM((t, LANES), F32)] * (2 * hpg),
    )
    return pl.pallas_call(
        functools.partial(_attn_kernel, t=t, kpt=kpt, nd_far=nd_far, hpg=hpg),
        out_shape=jax.ShapeDtypeStruct((A_HEADS, bsz, s_len, e), BF16),
        grid_spec=grid_spec,
        compiler_params=_cp("parallel", "arbitrary"),
        name="a_attention",
    )(jnp.asarray(qt), jnp.asarray(kt), rel_table, qkv, qkv, qkv, mask4, bias)


def _mixer_a(h, g, w_in, w_out, rel_table, bsz, s_len):
    n = h.shape[0]
    d = D_MODEL
    n_idx = IDX_HEADS * IDX_DIM + IDX_DIM + IDX_HEADS
    w_qkv = jnp.concatenate([w_in[:, :d] * (A_HEAD_DIM ** -0.5 * LOG2E), w_in[:, d:3 * d]], axis=1).astype(BF16)
    w_idx = jnp.pad(w_in[:, 3 * d:], ((0, 0), (0, 640 - n_idx))).astype(BF16)
    qkv = _norm_matmul(h, g, w_qkv, out_dtype=BF16, tn=1024, head_major=True, name="a_qkv")
    idx = _norm_matmul(h, g, w_idx, out_dtype=F32, tn=640, name="a_idx")
    qkv = qkv.reshape(3 * A_HEADS, bsz, s_len, A_HEAD_DIM)
    mask4 = _a_select(idx.reshape(bsz, s_len, 640))

    t = min(512, s_len)
    nd_far = 0
    while _rel_bucket_np(np.array(nd_far * t - (t - 1))) < REL_BUCKETS - 1:
        nd_far += 1
    sub = t // LANES
    offs = np.arange(-(sub - 1), sub * nd_far + sub)
    ii = np.arange(LANES)[:, None] - np.arange(LANES)[None, :]
    dist = (offs[:, None, None] * LANES + ii[None]).reshape(len(offs) * LANES, LANES)
    bias = _bias_tiles(rel_table, _rel_bucket_np(dist), BF16, tile=LANES, tr=LANES, scale=LOG2E)
    kpt = 2 if (s_len // t) % 2 == 0 else 1
    o = _a_attention(qkv, mask4, bias, rel_table, t=t, kpt=kpt, nd_far=nd_far)
    return _out_proj(o.reshape(A_HEADS, n, A_HEAD_DIM), w_out.astype(BF16), h, name="a_out")


def _bsp_kernel(u_ref, v_ref, lg_ref, lb_ref, wsp_ref, bsp_ref, wo_ref, r_ref, o_ref, wt_ref, y_ref, *, rows):
    @pl.when(pl.program_id(0) == 0)
    def _():
        r = lax.broadcasted_iota(I32, (B_CHUNK, B_CHUNK), 0)
        c = lax.broadcasted_iota(I32, (B_CHUNK, B_CHUNK), 1)
        for g in range(B_GROUPS):
            wt_ref[g] = jnp.where(r >= c, wsp_ref[g], 0.0).astype(BF16)

    vn = _layernorm(v_ref[...].astype(F32), lg_ref[...], lb_ref[...]).astype(BF16)
    gw = B_HALF // B_GROUPS
    for c in range(rows // B_CHUNK):
        rs = slice(c * B_CHUNK, (c + 1) * B_CHUNK)
        for g in range(B_GROUPS):
            cs = slice(g * gw, (g + 1) * gw)
            sv = jnp.dot(wt_ref[g], vn[rs, cs], preferred_element_type=F32) + bsp_ref[g]
            y_ref[rs, cs] = (u_ref[rs, cs].astype(F32) * sv).astype(BF16)
    o_ref[...] = r_ref[...] + jnp.dot(y_ref[...], wo_ref[...], preferred_element_type=F32)


def _b_spatial_out(z, ln_g, ln_b, w_sp, b_sp, w_out, resid, *, rows=512):
    n, d = resid.shape
    return pl.pallas_call(
        functools.partial(_bsp_kernel, rows=rows),
        out_shape=jax.ShapeDtypeStruct((n, d), F32),
        grid=(n // rows,),
        in_specs=[pl.BlockSpec((rows, B_HALF), lambda i: (i, 0)),
                  pl.BlockSpec((rows, B_HALF), lambda i: (i, 1)),
                  pl.BlockSpec((1, B_HALF), lambda i: (0, 0)),
                  pl.BlockSpec((1, B_HALF), lambda i: (0, 0)),
                  pl.BlockSpec((B_GROUPS, B_CHUNK, B_CHUNK), lambda i: (0, 0, 0)),
                  pl.BlockSpec((B_GROUPS, B_CHUNK, 1), lambda i: (0, 0, 0)),
                  pl.BlockSpec((B_HALF, d), lambda i: (0, 0), pipeline_mode=pl.Buffered(1)),
                  pl.BlockSpec((rows, d), lambda i: (i, 0))],
        out_specs=pl.BlockSpec((rows, d), lambda i: (i, 0)),
        scratch_shapes=[pltpu.VMEM((B_GROUPS, B_CHUNK, B_CHUNK), BF16), pltpu.VMEM((rows, B_HALF), BF16)],
        compiler_params=_cp("arbitrary"),
        name="b_spatial_out",
    )(z, z, ln_g.reshape(1, B_HALF), ln_b.reshape(1, B_HALF), w_sp, b_sp.reshape(B_GROUPS, B_CHUNK, 1),
      w_out, resid)


def _mixer_b(h, g, w_in, b_in, ln_g, ln_b, w_sp, b_sp, w_out):
    z = _norm_matmul(h, g, w_in.astype(BF16), b_in, mode="gelu", out_dtype=BF16, tn=1024, name="b_in")
    return _b_spatial_out(z, ln_g, ln_b, w_sp, b_sp, w_out.astype(BF16), h)


C_HALO = 32


def _conv_kernel(cur_ref, halo_ref, wdw_ref, bdw_ref, lg_ref, lb_ref, w2_ref, b2_ref, r_ref, o_ref,
                 ext_ref, yc_ref, *, ts, rt, ct):
    i = pl.program_id(1)
    ext_ref[C_HALO:, :] = cur_ref[0]

    @pl.when(i == 0)
    def _():
        ext_ref[:C_HALO, :] = jnp.zeros((C_HALO, D_MODEL), F32)

    @pl.when(i > 0)
    def _():
        ext_ref[:C_HALO, :] = halo_ref[0]

    off = C_HALO - (C_KERNEL - 1)

    def tile(it, carry):
        r0 = pl.multiple_of((it // (D_MODEL // ct)) * rt, rt)
        c0 = pl.multiple_of((it % (D_MODEL // ct)) * ct, ct)
        nwin = rt + C_HALO
        win = ext_ref[pl.ds(r0, nwin), pl.ds(c0, ct)]
        acc = jnp.zeros((rt, ct), F32) + bdw_ref[:, pl.ds(c0, ct)]
        for b in range(8):
            rolled = pltpu.roll(win, nwin - (off + b), axis=0)
            for a in range((C_KERNEL - b + 7) // 8):
                j = 8 * a + b
                acc = acc + rolled[8 * a:8 * a + rt] * wdw_ref[j:j + 1, pl.ds(c0, ct)]
        yc_ref[pl.ds(r0, rt), pl.ds(c0, ct)] = acc
        return carry

    lax.fori_loop(0, (ts // rt) * (D_MODEL // ct), tile, 0)
    y = _layernorm(yc_ref[...], lg_ref[...], lb_ref[...])
    y = (y * jax.nn.sigmoid(y)).astype(BF16)
    o_ref[0] = r_ref[0] + jnp.dot(y, w2_ref[...], preferred_element_type=F32) + b2_ref[...]


def _c_conv(y1, w_dw, b_dw, ln_g, ln_b, w2, b2, resid, *, ts=512, rt=128, ct=128):
    bsz, s_len, d = y1.shape
    hb = ts // C_HALO
    vec = lambda a: a.reshape(1, d)
    return pl.pallas_call(
        functools.partial(_conv_kernel, ts=ts, rt=rt, ct=ct),
        out_shape=jax.ShapeDtypeStruct((bsz, s_len, d), F32),
        grid=(bsz, s_len // ts),
        in_specs=[pl.BlockSpec((1, ts, d), lambda b, i: (b, i, 0)),
                  pl.BlockSpec((1, C_HALO, d), lambda b, i: (b, jnp.maximum(i * hb - 1, 0), 0)),
                  pl.BlockSpec((C_KERNEL, d), lambda b, i: (0, 0)),
                  pl.BlockSpec((1, d), lambda b, i: (0, 0)),
                  pl.BlockSpec((1, d), lambda b, i: (0, 0)),
                  pl.BlockSpec((1, d), lambda b, i: (0, 0)),
                  pl.BlockSpec((d, d), lambda b, i: (0, 0)),
                  pl.BlockSpec((1, d), lambda b, i: (0, 0)),
                  pl.BlockSpec((1, ts, d), lambda b, i: (b, i, 0))],
        out_specs=pl.BlockSpec((1, ts, d), lambda b, i: (b, i, 0)),
        scratch_shapes=[pltpu.VMEM((ts + C_HALO, d), F32), pltpu.VMEM((ts, d), F32)],
        compiler_params=_cp("parallel", "parallel"),
        name="c_conv",
    )(y1, y1, w_dw, vec(b_dw), vec(ln_g), vec(ln_b), w2, vec(b2), resid)


def _mixer_c(h, g, w_pw1, b_pw1, w_dw, b_dw, ln_g, ln_b, w_pw2, b_pw2, bsz, s_len):
    n, d = h.shape
    y1 = _norm_matmul(h, g, w_pw1.astype(BF16), b_pw1, mode="glu", out_dtype=F32, tn=512, name="c_pw1")
    out = _c_conv(y1.reshape(bsz, s_len, d), w_dw, b_dw, ln_g, ln_b, w_pw2.astype(BF16), b_pw2,
                  h.reshape(bsz, s_len, d))
    return out.reshape(n, d)


def _dattn_kernel(q_ref, kc_ref, kp_ref, vc_ref, vp_ref, bias_ref, o_ref, lse_ref):
    nb = pl.program_id(2)
    first = jnp.where(nb == 0, NEG, 0.0)
    e = D_HEAD_DIM
    col = lax.broadcasted_iota(I32, (D_BLOCK, 2 * D_BLOCK), 1)
    first_mask = jnp.where(col < D_BLOCK, first, 0.0)
    low = lax.broadcasted_iota(I32, (D_BLOCK, LANES), 1) < e
    for hp in range(D_HEADS // 2):
        ps = slice(hp * LANES, (hp + 1) * LANES)
        q2 = q_ref[:, ps]
        kk = jnp.concatenate([kp_ref[:, ps], kc_ref[:, ps]], axis=0)
        vv = jnp.concatenate([vp_ref[:, ps], vc_ref[:, ps]], axis=0)
        outs, lses = [], []
        for hh in range(2):
            qh = jnp.where(low if hh == 0 else ~low, q2, jnp.zeros_like(q2))
            s = lax.dot_general(qh, kk, _NT, preferred_element_type=F32) * (e ** -0.5) + bias_ref[2 * hp + hh]
            s = s + first_mask
            m = jnp.max(s, axis=-1, keepdims=True)
            p = jnp.exp(s - m)
            l = jnp.sum(p, axis=-1, keepdims=True)
            outs.append(jnp.dot(p.astype(BF16), vv, preferred_element_type=F32) / l)
            lses.append(jnp.broadcast_to(m + jnp.log(l), (D_BLOCK, LANES)))
        o_ref[0, :, ps] = jnp.where(low, outs[0], outs[1])
        lse_ref[0, :, ps] = jnp.where(low, lses[0], lses[1])


def _dproj_kernel(x_ref, g_ref, w_ref, o_ref, xs_ref, *, dil):
    span, d = x_ref.shape
    nslab = d // LANES
    for j in range(nslab):
        xs_ref[j] = x_ref[:, j * LANES:(j + 1) * LANES]
    rows = 4 * D_BLOCK
    for c in range(span // rows):
        xr = jnp.concatenate(
            [jnp.concatenate([xs_ref[j, pl.ds(r, D_BLOCK, stride=dil), :] for j in range(nslab)], axis=1)
             for r in range(4 * c, 4 * c + 4)], axis=0)
        y = jnp.dot(_rms_bf16(xr, g_ref[...]), w_ref[...], preferred_element_type=F32)
        o_ref[c * rows:(c + 1) * rows, :] = y.astype(o_ref.dtype)


def _d_proj(h, g, w, dil):
    n, d = h.shape
    m = w.shape[1]
    span = D_BLOCK * dil
    return pl.pallas_call(
        functools.partial(_dproj_kernel, dil=dil),
        out_shape=jax.ShapeDtypeStruct((n, m), BF16),
        grid=(n // span,),
        in_specs=[pl.BlockSpec((span, d), lambda i: (i, 0)),
                  pl.BlockSpec((1, d), lambda i: (0, 0)),
                  pl.BlockSpec((d, m), lambda i: (0, 0))],
        out_specs=pl.BlockSpec((span, m), lambda i: (i, 0)),
        scratch_shapes=[pltpu.VMEM((d // LANES, span, LANES), F32)],
        compiler_params=_cp("parallel"),
        name=f"d_in_dil{dil}",
    )(h, g.reshape(1, d), w)


def _d_group(proj, bias_g, gidx, dil, bsz, s_len):
    n_sub = s_len // dil
    nb = n_sub // D_BLOCK
    w = D_HEADS * D_HEAD_DIM
    blk = (D_BLOCK, w)
    rb = lambda b, r, n: b * (s_len // D_BLOCK) + n * dil + r
    cur = lambda col: (lambda b, r, n: (rb(b, r, n), col))
    prev = lambda col: (lambda b, r, n: (rb(b, r, jnp.maximum(n - 1, 0)), col))
    out_blk = (1, D_BLOCK, w)
    out_sds = jax.ShapeDtypeStruct((bsz, n_sub, dil * w), F32)
    o, lse = pl.pallas_call(
        _dattn_kernel,
        out_shape=(out_sds, out_sds),
        grid=(bsz, dil, nb),
        in_specs=[pl.BlockSpec(blk, cur(0)),
                  pl.BlockSpec(blk, cur(1)),
                  pl.BlockSpec(blk, prev(1)),
                  pl.BlockSpec(blk, cur(2)),
                  pl.BlockSpec(blk, prev(2)),
                  pl.BlockSpec((D_HEADS, D_BLOCK, 2 * D_BLOCK), lambda b, r, n: (0, 0, 0))],
        out_specs=(pl.BlockSpec(out_blk, lambda b, r, n: (b, n, r)),
                   pl.BlockSpec(out_blk, lambda b, r, n: (b, n, r))),
        compiler_params=_cp("parallel", "parallel", "arbitrary"),
        name=f"d_attn_g{gidx}",
    )(proj, proj, proj, proj, proj, bias_g)
    return o.reshape(bsz * s_len, w), lse.reshape(bsz * s_len, w)


def _dmerge_kernel(o0, o1, o2, l0, l1, l2, w_ref, r_ref, out_ref):
    a0, a1, a2 = l0[...], l1[...], l2[...]
    mx = jnp.maximum(jnp.maximum(a0, a1), a2)
    e0, e1, e2 = jnp.exp(a0 - mx), jnp.exp(a1 - mx), jnp.exp(a2 - mx)
    y = (e0 * o0[...] + e1 * o1[...] + e2 * o2[...]) / (e0 + e1 + e2)
    out_ref[...] = r_ref[...] + jnp.dot(y.astype(BF16), w_ref[...], preferred_element_type=F32)


def _d_merge(outs, lses, w_out, resid, *, tm=512):
    n, w = outs[0].shape
    d = resid.shape[1]
    row = pl.BlockSpec((tm, w), lambda i: (i, 0))
    return pl.pallas_call(
        _dmerge_kernel,
        out_shape=jax.ShapeDtypeStruct((n, d), F32),
        grid=(n // tm,),
        in_specs=[row] * 6 + [pl.BlockSpec((w, d), lambda i: (0, 0)),
                              pl.BlockSpec((tm, d), lambda i: (i, 0))],
        out_specs=pl.BlockSpec((tm, d), lambda i: (i, 0)),
        compiler_params=_cp("parallel"),
        name="d_merge",
    )(*outs, *lses, w_out, resid)


def _mixer_d(h, g, w_in, w_out, rel_table, bsz, s_len):
    w = D_HEADS * D_HEAD_DIM
    ng = len(D_PAIRS)
    p_loc = np.arange(D_BLOCK)[:, None]
    j_loc = np.arange(2 * D_BLOCK)[None, :]
    m = p_loc + D_BLOCK - j_loc
    idx = []
    for window, dil in D_PAIRS:
        steps = window // dil
        idx.append(np.where((m >= 0) & (m <= steps), _rel_bucket_np(m * dil), -1))
    bias = _bias_tiles(rel_table, np.concatenate(idx, 0).astype(np.int32), F32, tile=D_BLOCK, tr=D_BLOCK)
    outs, lses = [], []
    for gi, (_, dil) in enumerate(D_PAIRS):
        cols = [w_in[:, (part * ng + gi) * w:(part * ng + gi + 1) * w] for part in range(3)]
        w_g = jnp.concatenate(cols, axis=1).astype(BF16)
        if dil == 1:
            proj = _norm_matmul(h, g, w_g, out_dtype=BF16, tn=3 * w, name="d_in_dil1")
        else:
            proj = _d_proj(h, g, w_g, dil)
        o, lse = _d_group(proj, bias[gi], gi, dil, bsz, s_len)
        outs.append(o)
        lses.append(lse)
    return _d_merge(outs, lses, w_out.astype(BF16), h)


def kernel(x, norm_g, final_g, ffn_w_in, ffn_w_out, rel_table, a_w_in, a_w_out, b_w_in, b_b_in, b_ln_g, b_ln_b, b_w_sp, b_b_sp, b_w_out, c_w_pw1, c_b_pw1, c_w_dw, c_b_dw, c_ln_g, c_ln_b, c_w_pw2, c_b_pw2, d_w_in, d_w_out):
    bsz, s_len, d = x.shape
    depth = norm_g.shape[0]
    h = x.reshape(bsz * s_len, d)
    for i in range(depth):
        kind, j = i % 4, i // 4
        h = _ffn(h, norm_g[i, 0], ffn_w_in[i, 0].astype(BF16), ffn_w_out[i, 0].astype(BF16))
        g = norm_g[i, 1]
        if kind == 0:
            h = _mixer_a(h, g, a_w_in[j], a_w_out[j], rel_table, bsz, s_len)
        elif kind == 1:
            h = _mixer_b(h, g, b_w_in[j], b_b_in[j], b_ln_g[j], b_ln_b[j], b_w_sp[j], b_b_sp[j], b_w_out[j])
        elif kind == 2:
            h = _mixer_c(h, g, c_w_pw1[j], c_b_pw1[j], c_w_dw[j], c_b_dw[j], c_ln_g[j], c_ln_b[j],
                         c_w_pw2[j], c_b_pw2[j], bsz, s_len)
        else:
            h = _mixer_d(h, g, d_w_in[j], d_w_out[j], rel_table, bsz, s_len)
        h = _ffn(h, norm_g[i, 2], ffn_w_in[i, 1].astype(BF16), ffn_w_out[i, 1].astype(BF16),
                 final_g if i == depth - 1 else None)
    return h.reshape(bsz, s_len, d)
```

```python
import functools
import math

import numpy as np
import jax
import jax.numpy as jnp
from jax import lax
from jax.experimental import pallas as pl
from jax.experimental.pallas import tpu as pltpu

F32, BF16, I32 = jnp.float32, jnp.bfloat16, jnp.int32

D_MODEL = 1024
EPS = 1e-6
D_FF = 2816
REL_BUCKETS = 32
REL_MAX_DIST = 2048
A_HEADS = 8
A_HEAD_DIM = 128
IDX_HEADS = 8
IDX_DIM = 64
TOPK_MAX = 256
B_CHUNK = 128
B_HALF = 3 * D_MODEL
B_GROUPS = 8
C_KERNEL = 31
D_PAIRS = ((128, 1), (512, 4), (2048, 16))
D_HEADS = 8
D_HEAD_DIM = 64
D_BLOCK = 128

LANES = 128
MXU_TILE = 256
VMEM_LIMIT = 56 * 1024 * 1024
NEG = -1e30
INT_MIN = -2 ** 31
LOG2E = math.log2(math.e)

_NT = (((1,), (1,)), ((), ()))


def _cp(*sem):
    return pltpu.CompilerParams(dimension_semantics=sem, vmem_limit_bytes=VMEM_LIMIT)


def _rms_bf16(x, g):
    ms = jnp.mean(x * x, axis=-1, keepdims=True)
    return (x * lax.rsqrt(ms + EPS) * g).astype(BF16)


def _layernorm(x, g, b):
    mu = jnp.mean(x, axis=-1, keepdims=True)
    xc = x - mu
    var = jnp.mean(xc * xc, axis=-1, keepdims=True)
    return xc * lax.rsqrt(var + EPS) * g + b


def _ffn_kernel(x_ref, xnext_ref, g_ref, wi_ref, wo_ref, *refs, chunks):
    o_ref, xn_ref = refs[-2], refs[-1]
    i = pl.program_id(0)
    slot = i % 2

    @pl.when(i == 0)
    def _():
        xn_ref[0] = _rms_bf16(x_ref[...], g_ref[...])

    xn = xn_ref[slot]
    acc = None
    c0 = 0
    for tf in chunks:
        gate = jnp.dot(xn, wi_ref[:, c0:c0 + tf], preferred_element_type=F32)
        up = jnp.dot(xn, wi_ref[:, D_FF + c0:D_FF + c0 + tf], preferred_element_type=F32)
        a = (gate * jax.nn.sigmoid(gate) * up).astype(BF16)
        part = jnp.dot(a, wo_ref[c0:c0 + tf, :], preferred_element_type=F32)
        acc = part if acc is None else acc + part
        c0 += tf
    out = x_ref[...] + 0.5 * acc
    if len(refs) == 3:
        ms = jnp.mean(out * out, axis=-1, keepdims=True)
        out = out * lax.rsqrt(ms + EPS) * refs[0][...]
    o_ref[...] = out
    xn_ref[1 - slot] = _rms_bf16(xnext_ref[...], g_ref[...])


def _ffn(h, g, w_in, w_out, final_g=None, *, tm=512):
    n, d = h.shape
    last = n // tm - 1
    n_tiles = D_FF // MXU_TILE
    chunks = ((n_tiles + 1) // 2 * MXU_TILE, n_tiles // 2 * MXU_TILE)
    resident = dict(pipeline_mode=pl.Buffered(1))
    vec = pl.BlockSpec((1, d), lambda i: (0, 0))
    in_specs = [pl.BlockSpec((tm, d), lambda i: (i, 0)),
                pl.BlockSpec((tm, d), lambda i: (jnp.minimum(i + 1, last), 0)),
                vec,
                pl.BlockSpec((d, 2 * D_FF), lambda i: (0, 0), **resident),
                pl.BlockSpec((D_FF, d), lambda i: (0, 0), **resident)]
    args = [h, h, g.reshape(1, d), w_in, w_out]
    if final_g is not None:
        in_specs.append(vec)
        args.append(final_g.reshape(1, d))
    return pl.pallas_call(
        functools.partial(_ffn_kernel, chunks=chunks),
        out_shape=jax.ShapeDtypeStruct((n, d), F32),
        grid=(n // tm,),
        in_specs=in_specs,
        out_specs=pl.BlockSpec((tm, d), lambda i: (i, 0)),
        scratch_shapes=[pltpu.VMEM((2, tm, d), BF16)],
        compiler_params=_cp("arbitrary"),
        name="ffn",
    )(*args)


def _nm_kernel(mode, has_bias, tn, x_ref, g_ref, w_ref, *refs):
    b_ref = refs[0] if has_bias else None
    o_ref, xn_ref = refs[-2], refs[-1]
    s = pl.program_id(0)
    head_major = len(o_ref.shape) == 3
    m_out = o_ref.shape[0] * LANES if head_major else o_ref.shape[1]

    @pl.when(s == 0)
    def _():
        xn_ref[0] = _rms_bf16(x_ref[...], g_ref[...])

    @pl.when(s > 0)
    def _():
        xn = xn_ref[(s - 1) % 2]

        def column_block(c0):
            y = jnp.dot(xn, w_ref[:, c0:c0 + tn], preferred_element_type=F32)
            return y + b_ref[:, c0:c0 + tn] if has_bias else y

        for j in range(m_out // tn):
            y = column_block(j * tn)
            if mode == "gelu":
                y = jax.nn.gelu(y)
            elif mode == "glu":
                y = y * jax.nn.sigmoid(column_block(m_out + j * tn))
            if head_major:
                for hh in range(tn // LANES):
                    o_ref[j * (tn // LANES) + hh] = y[:, hh * LANES:(hh + 1) * LANES].astype(o_ref.dtype)
            else:
                o_ref[:, j * tn:(j + 1) * tn] = y.astype(o_ref.dtype)
        xn_ref[s % 2] = _rms_bf16(x_ref[...], g_ref[...])


def _norm_matmul(h, g, w, b=None, *, mode="plain", out_dtype=BF16, tm=512, tn=512, head_major=False, name="proj"):
    n, d = h.shape
    m_total = w.shape[1]
    m_out = m_total // 2 if mode == "glu" else m_total
    nt = n // tm
    has_bias = b is not None
    prev = lambda s: jnp.maximum(s - 1, 0)
    if head_major:
        out_shape = jax.ShapeDtypeStruct((m_out // LANES, n, LANES), out_dtype)
        out_spec = pl.BlockSpec((m_out // LANES, tm, LANES), lambda s: (0, prev(s), 0))
    else:
        out_shape = jax.ShapeDtypeStruct((n, m_out), out_dtype)
        out_spec = pl.BlockSpec((tm, m_out), lambda s: (prev(s), 0))
    resident = dict(pipeline_mode=pl.Buffered(1))
    in_specs = [pl.BlockSpec((tm, d), lambda s: (jnp.minimum(s, nt - 1), 0)),
                pl.BlockSpec((1, d), lambda s: (0, 0)),
                pl.BlockSpec((d, m_total), lambda s: (0, 0), **resident)]
    args = [h, g.reshape(1, d), w]
    if has_bias:
        in_specs.append(pl.BlockSpec((1, m_total), lambda s: (0, 0)))
        args.append(b.reshape(1, m_total))
    return pl.pallas_call(
        functools.partial(_nm_kernel, mode, has_bias, tn),
        out_shape=out_shape,
        grid=(nt + 1,),
        in_specs=in_specs,
        out_specs=out_spec,
        scratch_shapes=[pltpu.VMEM((2, tm, d), BF16)],
        compiler_params=_cp("arbitrary"),
        name=name,
    )(*args)


def _rel_bucket_np(dist):
    max_exact = REL_BUCKETS // 2
    d = np.maximum(dist, 0)
    df = np.maximum(d, 1).astype(np.float32)
    large = max_exact + (np.log(df / np.float32(max_exact)) / np.float32(math.log(REL_MAX_DIST / max_exact))
                         * np.float32(REL_BUCKETS - max_exact)).astype(np.int32)
    large = np.minimum(large, REL_BUCKETS - 1)
    return np.where(d < max_exact, d, large).astype(np.int32)


def _bias_kernel(scale, tab_ref, idx_ref, o_ref):
    idx = idx_ref[...]
    for h in range(o_ref.shape[1]):
        acc = jnp.zeros(idx.shape, F32)
        for b in range(REL_BUCKETS):
            acc = jnp.where(idx == b, tab_ref[b, h] * scale, acc)
        o_ref[0, h] = jnp.where(idx < 0, NEG, acc).astype(o_ref.dtype)


def _bias_tiles(rel_table, bucket_idx, out_dtype, *, tile, tr, scale=1.0):
    r, c = bucket_idx.shape
    nh = rel_table.shape[1]
    per = tile // tr
    return pl.pallas_call(
        functools.partial(_bias_kernel, scale),
        out_shape=jax.ShapeDtypeStruct((r // tile, nh, tile, c), out_dtype),
        grid=(r // tr,),
        in_specs=[pl.BlockSpec(memory_space=pltpu.SMEM),
                  pl.BlockSpec((tr, c), lambda i: (i, 0))],
        out_specs=pl.BlockSpec((1, nh, tr, c), lambda i: (i // per, 0, i % per, 0)),
        compiler_params=_cp("parallel"),
        name="bias_tiles",
    )(rel_table, jnp.asarray(bucket_idx))


def _asel_kernel(qi_ref, side_ref, kall_ref, m_ref, keys_ref, wb_ref, x_ref, si_ref, sf_ref, qb_ref,
                 *, tq, cw, s_len, topk):
    qb = pl.program_id(1)
    spc = cw // LANES
    n_chunks = (qb * tq + tq + cw - 1) // cw
    n_slabs = n_chunks * spc
    nslab_all = s_len // LANES

    qb_ref[...] = qi_ref[0].astype(BF16)
    wi = side_ref[0, :, IDX_DIM:IDX_DIM + IDX_HEADS] * (IDX_HEADS ** -0.5 * IDX_DIM ** -0.5)
    for h in range(IDX_HEADS):
        wb_ref[h] = jnp.broadcast_to(wi[:, h:h + 1], (tq, LANES))
    q_pos = qb * tq + lax.broadcasted_iota(I32, (tq, cw), 0)
    lane_pos = lax.broadcasted_iota(I32, (tq, cw), 1)

    def to_key(x):
        bits = pltpu.bitcast(x, I32)
        return bits ^ ((bits >> 31) & 0x7FFFFFFF)

    def build(c, mx):
        k0 = pl.multiple_of(c * cw, cw)
        kc = kall_ref[0, pl.ds(k0, cw), :][:, :IDX_DIM].astype(BF16)
        sc = jnp.zeros((tq, cw), F32)
        for h in range(IDX_HEADS):
            r = lax.dot_general(qb_ref[:, h * IDX_DIM:(h + 1) * IDX_DIM], kc, _NT,
                                preferred_element_type=F32)
            sc = sc + jnp.maximum(r, 0.0) * jnp.concatenate([wb_ref[h]] * spc, axis=1)
        sc = sc + 0.0
        causal = k0 + lane_pos <= q_pos
        key = jnp.where(causal, to_key(sc), INT_MIN)
        scm = jnp.where(causal, sc, -jnp.inf)
        for j in range(spc):
            keys_ref[c * spc + j] = key[:, j * LANES:(j + 1) * LANES]
            mx = jnp.maximum(mx, scm[:, j * LANES:(j + 1) * LANES])
        return mx

    mx = lax.fori_loop(0, n_chunks, build, jnp.full((tq, LANES), -jnp.inf, F32))
    key_max = to_key(jnp.broadcast_to(jnp.max(mx, axis=1, keepdims=True), (tq, LANES)))

    def lane_sum(acc):
        return jnp.broadcast_to(jnp.sum(acc.astype(F32), axis=1, keepdims=True), (tq, LANES))

    def slab_loop(body, init):
        def chunk(c, carry):
            for j in range(spc):
                carry = body(c * spc + j, carry)
            return carry
        return lax.fori_loop(0, n_chunks, chunk, init)

    zero = jnp.zeros((tq, LANES), I32)

    def count_ge(cand):
        parts = []
        for r0 in range(0, tq, 128):
            rows = slice(r0, r0 + 128)
            cand_r = cand[rows]
            parts.append(slab_loop(lambda c, acc: acc + jnp.where(keys_ref[c, rows, :] >= cand_r, 1, 0), zero[rows]))
        return lane_sum(parts[0] if len(parts) == 1 else jnp.concatenate(parts, axis=0))

    n_real = (q_pos[:, :LANES] + 1).astype(F32)
    lo0 = jnp.full((tq, LANES), INT_MIN + 1, I32)
    si_ref[0] = lo0
    si_ref[1] = jnp.where(n_real > topk, key_max + 1, lo0 + 1)
    sf_ref[0] = n_real
    sf_ref[1] = jnp.zeros((tq, LANES), F32)

    def midpoint(lo, hi):
        return lo + lax.shift_right_logical(hi - lo, 1)

    def search_pass(cand):
        lo, hi, clo, chi = si_ref[0], si_ref[1], sf_ref[0], sf_ref[1]
        cnt = count_ge(cand)
        ge = cnt >= topk
        lo, clo = jnp.where(ge, cand, lo), jnp.where(ge, cnt, clo)
        hi = jnp.where(ge, hi, jnp.maximum(cand, lo + 1))
        hi = jnp.where(clo == topk, lo + 1, hi)
        si_ref[0], si_ref[1] = lo, hi
        sf_ref[0], sf_ref[1] = clo, jnp.where(ge, chi, cnt)
        open_rows = jnp.where((hi - lo) != 1, 1.0, 0.0)
        return jnp.max(open_rows, axis=0, keepdims=True)[0, 0]

    def guided(first):
        lo, hi = si_ref[0], si_ref[1]
        is_open = (hi - lo) != 1
        cand = midpoint(lo, hi)
        cand = jnp.where(is_open & (lo < 0) & (hi == 1), 0, cand)
        cand = jnp.where(is_open & (lo < 0) & (hi > 1), 1, cand)
        if first:
            probe = hi - (1 << 25)
            cand = jnp.where(is_open & (probe > 1), probe, cand)
        return search_pass(cand)

    guided(True)
    guided(False)
    pending = guided(False)

    def search_body(carry):
        it, _ = carry
        return it + 1, search_pass(midpoint(si_ref[0], si_ref[1]))

    lax.while_loop(lambda carry: (carry[0] < 40) & (carry[1] > 0.0), search_body, (jnp.int32(0), pending))
    tau = si_ref[0]
    clo, chi = sf_ref[0], sf_ref[1]

    r_need = topk - chi
    tied = clo > topk
    x_ref[...] = jnp.full((tq, LANES), s_len, I32)
    lane = lax.broadcasted_iota(I32, (tq, LANES), 1)

    @pl.when(jnp.max(jnp.where(tied, 1.0, 0.0)) > 0.0)
    def _():
        nbits = int(math.log2(s_len)) + 1

        def tie_pass(i, x):
            cand = x + lax.shift_left(jnp.int32(1), nbits - 1 - i)

            def body(c, acc):
                idx = c * LANES + lane
                return acc + jnp.where((keys_ref[c] == tau) & (idx < cand), 1, 0)

            cnt = lane_sum(slab_loop(body, zero))
            return jnp.where(cnt < r_need, cand, x)

        x = lax.fori_loop(0, nbits, tie_pass, zero)
        x_ref[...] = jnp.where(tied, x, s_len)

    x_cut = x_ref[...]

    def emit(c, carry):
        k = keys_ref[c]
        idx = c * LANES + lane
        sel = (k > tau) | ((k == tau) & (idx <= x_cut))
        m_ref[0, c] = jnp.where(sel, 0.0, NEG).astype(m_ref.dtype)
        return carry

    slab_loop(emit, 0)

    def fill(c, carry):
        m_ref[0, c] = jnp.full((tq, LANES), NEG, m_ref.dtype)
        return carry

    lax.fori_loop(n_slabs, nslab_all, fill, 0)


def _a_select(idx, *, tq=128, cw=1024):
    bsz, s_len, width = idx.shape
    nq = IDX_HEADS * IDX_DIM
    side = nq // LANES
    assert width == nq + LANES
    topk = min(TOPK_MAX, s_len // 4)
    nslab = s_len // LANES
    return pl.pallas_call(
        functools.partial(_asel_kernel, tq=tq, cw=cw, s_len=s_len, topk=float(topk)),
        out_shape=jax.ShapeDtypeStruct((bsz, nslab, s_len, LANES), F32),
        grid=(bsz, s_len // tq),
        in_specs=[pl.BlockSpec((1, tq, nq), lambda b, i: (b, i, 0)),
                  pl.BlockSpec((1, tq, LANES), lambda b, i: (b, i, side)),
                  pl.BlockSpec((1, s_len, LANES), lambda b, i: (b, 0, side), pipeline_mode=pl.Buffered(1))],
        out_specs=pl.BlockSpec((1, nslab, tq, LANES), lambda b, i: (b, 0, i, 0)),
        scratch_shapes=[pltpu.VMEM((nslab, tq, LANES), I32),
                        pltpu.VMEM((IDX_HEADS, tq, LANES), F32),
                        pltpu.VMEM((tq, LANES), I32),
                        pltpu.VMEM((2, tq, LANES), I32),
                        pltpu.VMEM((2, tq, LANES), F32),
                        pltpu.VMEM((tq, nq), BF16)],
        compiler_params=_cp("arbitrary", "arbitrary"),
        name="a_select",
    )(idx, idx, idx)


def _attn_kernel(qt_ref, kt_ref, tab_ref, q_ref, k_ref, v_ref, msk_ref, bias_ref, wo_ref, r_ref, o_ref,
                 m_sc, l_sc, acc_sc, *slots, t, kpt, nd_far, hpg):
    step = pl.program_id(1)
    qi = qt_ref[step]
    kp = kt_ref[step]
    d_last = qi - (kp * kpt + kpt - 1)
    n_unit = kpt * A_HEADS
    sub = t // LANES

    @pl.when(kp == 0)
    def _():
        m_sc[...] = jnp.full(m_sc.shape, NEG, F32)
        l_sc[...] = jnp.zeros_like(l_sc)
        acc_sc[...] = jnp.zeros_like(acc_sc)

    ns = 2 * hpg
    s_sc, p_sc, al_sc = slots[:ns], slots[ns:2 * ns], slots[2 * ns:]

    def slot(u):
        return (u // hpg % 2) * hpg + u % hpg

    def scores(u, far):
        h, kt = u % A_HEADS, u // A_HEADS
        keys = slice(kt * t, (kt + 1) * t)
        mask = jnp.concatenate([msk_ref[0, kt * sub + j] for j in range(sub)], axis=1)
        s = lax.dot_general(q_ref[h, 0], k_ref[h, 0, keys, :], _NT, preferred_element_type=F32) + mask
        if not far:
            off0 = sub * jnp.clip(qi - (kp * kpt + kt), 0, nd_far) + (sub - 1)
            s = s + jnp.concatenate(
                [jnp.concatenate([bias_ref[off0 + a - b, h] for b in range(sub)], axis=1) for a in range(sub)],
                axis=0).astype(F32)
        s_sc[slot(u)][...] = s

    def softmax(u, far):
        h = u % A_HEADS
        s = s_sc[slot(u)][...]
        c = tab_ref[REL_BUCKETS - 1, h] * LOG2E if far else 0.0
        m_prev = m_sc[h]
        m_cur = jnp.broadcast_to(jnp.max(s, axis=-1, keepdims=True), (t, LANES))
        m_new = jnp.maximum(m_prev, m_cur + c)
        alpha = jnp.exp2(m_prev - m_new)
        p = jnp.exp2(s - jnp.concatenate([m_new - c] * (t // LANES), axis=1))
        p_part = p[:, :LANES]
        for j in range(1, t // LANES):
            p_part = p_part + p[:, j * LANES:(j + 1) * LANES]
        l_sc[h] = alpha * l_sc[h] + p_part
        m_sc[h] = m_new
        al_sc[slot(u)][...] = alpha
        p_sc[slot(u)][...] = p.astype(BF16)

    def values(u):
        h, kt = u % A_HEADS, u // A_HEADS
        acc_sc[h] = al_sc[slot(u)][...] * acc_sc[h] + jnp.dot(p_sc[slot(u)][...], v_ref[h, 0, kt * t:(kt + 1) * t, :],
                                                            preferred_element_type=F32)

    one = qt_ref[0] + 1

    def units(far):
        n_grp = n_unit // hpg
        for st in range(n_grp + 2):
            def stage(i, carry, st=st):
                for j in range(hpg):
                    if st < n_grp:
                        scores(st * hpg + j, far)
                    if 1 <= st <= n_grp:
                        softmax((st - 1) * hpg + j, far)
                    if st >= 2:
                        values((st - 2) * hpg + j)
                return carry

            lax.fori_loop(0, one, stage, 0)

    @pl.when(d_last >= nd_far)
    def _():
        units(True)

    @pl.when(d_last < nd_far)
    def _():
        units(False)

    @pl.when(kp == qi // kpt)
    def _():
        heads = []
        for h in range(A_HEADS):
            l = jnp.sum(l_sc[h], axis=-1, keepdims=True)
            heads.append((acc_sc[h] * (1.0 / l)).astype(BF16))
        y = jnp.concatenate(heads, axis=1)
        o_ref[0] = r_ref[0] + jnp.dot(y, wo_ref[...], preferred_element_type=F32)


def _a_attention(qkv, mask4, bias, rel_table, w_out, resid, *, t, kpt, nd_far, hpg=4):
    _, bsz, s_len, e = qkv.shape
    d = w_out.shape[1]
    nq = s_len // t
    qt = np.concatenate([np.full(i // kpt + 1, i, np.int32) for i in range(nq)])
    kt = np.concatenate([np.arange(i // kpt + 1, dtype=np.int32) for i in range(nq)])
    spt = t // LANES
    grid_spec = pltpu.PrefetchScalarGridSpec(
        num_scalar_prefetch=2,
        grid=(bsz, len(qt)),
        in_specs=[
            pl.BlockSpec(memory_space=pltpu.SMEM),
            pl.BlockSpec((A_HEADS, 1, t, e), lambda b, s, qt, kt: (0, b, qt[s], 0)),
            pl.BlockSpec((A_HEADS, 1, kpt * t, e), lambda b, s, qt, kt: (1, b, kt[s], 0)),
            pl.BlockSpec((A_HEADS, 1, kpt * t, e), lambda b, s, qt, kt: (2, b, kt[s], 0)),
            pl.BlockSpec((1, kpt * spt, t, LANES), lambda b, s, qt, kt: (b, kt[s], qt[s], 0)),
            pl.BlockSpec(bias.shape, lambda b, s, qt, kt: (0, 0, 0, 0), pipeline_mode=pl.Buffered(1)),
            pl.BlockSpec(w_out.shape, lambda b, s, qt, kt: (0, 0), pipeline_mode=pl.Buffered(1)),
            pl.BlockSpec((1, t, d), lambda b, s, qt, kt: (b, qt[s], 0)),
        ],
        out_specs=pl.BlockSpec((1, t, d), lambda b, s, qt, kt: (b, qt[s], 0)),
        scratch_shapes=[pltpu.VMEM((A_HEADS, t, LANES), F32),
                        pltpu.VMEM((A_HEADS, t, LANES), F32),
                        pltpu.VMEM((A_HEADS, t, e), F32)]
        + [pltpu.VMEM((t, t), F32)] * (2 * hpg)
        + [pltpu.VMEM((t, t), BF16)] * (2 * hpg)
        + [pltpu.VMEM((t, LANES), F32)] * (2 * hpg),
    )
    return pl.pallas_call(
        functools.partial(_attn_kernel, t=t, kpt=kpt, nd_far=nd_far, hpg=hpg),
        out_shape=jax.ShapeDtypeStruct((bsz, s_len, d), F32),
        grid_spec=grid_spec,
        compiler_params=_cp("parallel", "arbitrary"),
        name="a_attention",
    )(jnp.asarray(qt), jnp.asarray(kt), rel_table, qkv, qkv, qkv, mask4, bias, w_out, resid)


def _mixer_a(h, g, w_in, w_out, rel_table, bsz, s_len):
    n = h.shape[0]
    d = D_MODEL
    n_idx = IDX_HEADS * IDX_DIM + IDX_DIM + IDX_HEADS
    w_qkv = jnp.concatenate([w_in[:, :d] * (A_HEAD_DIM ** -0.5 * LOG2E), w_in[:, d:3 * d]], axis=1).astype(BF16)
    w_idx = jnp.pad(w_in[:, 3 * d:], ((0, 0), (0, 640 - n_idx))).astype(BF16)
    qkv = _norm_matmul(h, g, w_qkv, out_dtype=BF16, tn=1024, head_major=True, name="a_qkv")
    idx = _norm_matmul(h, g, w_idx, out_dtype=F32, tn=640, name="a_idx")
    qkv = qkv.reshape(3 * A_HEADS, bsz, s_len, A_HEAD_DIM)
    mask4 = _a_select(idx.reshape(bsz, s_len, 640))

    t = min(512, s_len)
    nd_far = 0
    while _rel_bucket_np(np.array(nd_far * t - (t - 1))) < REL_BUCKETS - 1:
        nd_far += 1
    sub = t // LANES
    offs = np.arange(-(sub - 1), sub * nd_far + sub)
    ii = np.arange(LANES)[:, None] - np.arange(LANES)[None, :]
    dist = (offs[:, None, None] * LANES + ii[None]).reshape(len(offs) * LANES, LANES)
    bias = _bias_tiles(rel_table, _rel_bucket_np(dist), BF16, tile=LANES, tr=LANES, scale=LOG2E)
    kpt = 2 if (s_len // t) % 2 == 0 else 1
    out = _a_attention(qkv, mask4, bias, rel_table, w_out.astype(BF16), h.reshape(bsz, s_len, d),
                       t=t, kpt=kpt, nd_far=nd_far)
    return out.reshape(n, d)


def _bsp_kernel(u_ref, v_ref, lg_ref, lb_ref, wsp_ref, bsp_ref, wo_ref, r_ref, o_ref, wt_ref, y_ref, *, rows):
    @pl.when(pl.program_id(0) == 0)
    def _():
        r = lax.broadcasted_iota(I32, (B_CHUNK, B_CHUNK), 0)
        c = lax.broadcasted_iota(I32, (B_CHUNK, B_CHUNK), 1)
        for g in range(B_GROUPS):
            wt_ref[g] = jnp.where(r >= c, wsp_ref[g], 0.0).astype(BF16)

    vn = _layernorm(v_ref[...].astype(F32), lg_ref[...], lb_ref[...]).astype(BF16)
    gw = B_HALF // B_GROUPS
    for c in range(rows // B_CHUNK):
        rs = slice(c * B_CHUNK, (c + 1) * B_CHUNK)
        for g in range(B_GROUPS):
            cs = slice(g * gw, (g + 1) * gw)
            sv = jnp.dot(wt_ref[g], vn[rs, cs], preferred_element_type=F32) + bsp_ref[g]
            y_ref[rs, cs] = (u_ref[rs, cs].astype(F32) * sv).astype(BF16)
    o_ref[...] = r_ref[...] + jnp.dot(y_ref[...], wo_ref[...], preferred_element_type=F32)


def _b_spatial_out(z, ln_g, ln_b, w_sp, b_sp, w_out, resid, *, rows=512):
    n, d = resid.shape
    return pl.pallas_call(
        functools.partial(_bsp_kernel, rows=rows),
        out_shape=jax.ShapeDtypeStruct((n, d), F32),
        grid=(n // rows,),
        in_specs=[pl.BlockSpec((rows, B_HALF), lambda i: (i, 0)),
                  pl.BlockSpec((rows, B_HALF), lambda i: (i, 1)),
                  pl.BlockSpec((1, B_HALF), lambda i: (0, 0)),
                  pl.BlockSpec((1, B_HALF), lambda i: (0, 0)),
                  pl.BlockSpec((B_GROUPS, B_CHUNK, B_CHUNK), lambda i: (0, 0, 0)),
                  pl.BlockSpec((B_GROUPS, B_CHUNK, 1), lambda i: (0, 0, 0)),
                  pl.BlockSpec((B_HALF, d), lambda i: (0, 0), pipeline_mode=pl.Buffered(1)),
                  pl.BlockSpec((rows, d), lambda i: (i, 0))],
        out_specs=pl.BlockSpec((rows, d), lambda i: (i, 0)),
        scratch_shapes=[pltpu.VMEM((B_GROUPS, B_CHUNK, B_CHUNK), BF16), pltpu.VMEM((rows, B_HALF), BF16)],
        compiler_params=_cp("arbitrary"),
        name="b_spatial_out",
    )(z, z, ln_g.reshape(1, B_HALF), ln_b.reshape(1, B_HALF), w_sp, b_sp.reshape(B_GROUPS, B_CHUNK, 1),
      w_out, resid)


def _mixer_b(h, g, w_in, b_in, ln_g, ln_b, w_sp, b_sp, w_out):
    z = _norm_matmul(h, g, w_in.astype(BF16), b_in, mode="gelu", out_dtype=BF16, tn=1024, name="b_in")
    return _b_spatial_out(z, ln_g, ln_b, w_sp, b_sp, w_out.astype(BF16), h)


C_HALO = 32


def _conv_kernel(cur_ref, halo_ref, wdw_ref, bdw_ref, lg_ref, lb_ref, w2_ref, b2_ref, r_ref, o_ref,
                 ext_ref, yc_ref, *, ts, rt, ct):
    i = pl.program_id(1)
    ext_ref[C_HALO:, :] = cur_ref[0]

    @pl.when(i == 0)
    def _():
        ext_ref[:C_HALO, :] = jnp.zeros((C_HALO, D_MODEL), F32)

    @pl.when(i > 0)
    def _():
        ext_ref[:C_HALO, :] = halo_ref[0]

    off = C_HALO - (C_KERNEL - 1)

    def tile(it, carry):
        r0 = pl.multiple_of((it // (D_MODEL // ct)) * rt, rt)
        c0 = pl.multiple_of((it % (D_MODEL // ct)) * ct, ct)
        nwin = rt + C_HALO
        win = ext_ref[pl.ds(r0, nwin), pl.ds(c0, ct)]
        acc = jnp.zeros((rt, ct), F32) + bdw_ref[:, pl.ds(c0, ct)]
        for b in range(8):
            rolled = pltpu.roll(win, nwin - (off + b), axis=0)
            for a in range((C_KERNEL - b + 7) // 8):
                j = 8 * a + b
                acc = acc + rolled[8 * a:8 * a + rt] * wdw_ref[j:j + 1, pl.ds(c0, ct)]
        yc_ref[pl.ds(r0, rt), pl.ds(c0, ct)] = acc
        return carry

    lax.fori_loop(0, (ts // rt) * (D_MODEL // ct), tile, 0)
    y = _layernorm(yc_ref[...], lg_ref[...], lb_ref[...])
    y = (y * jax.nn.sigmoid(y)).astype(BF16)
    o_ref[0] = r_ref[0] + jnp.dot(y, w2_ref[...], preferred_element_type=F32) + b2_ref[...]


def _c_conv(y1, w_dw, b_dw, ln_g, ln_b, w2, b2, resid, *, ts=512, rt=128, ct=128):
    bsz, s_len, d = y1.shape
    hb = ts // C_HALO
    vec = lambda a: a.reshape(1, d)
    return pl.pallas_call(
        functools.partial(_conv_kernel, ts=ts, rt=rt, ct=ct),
        out_shape=jax.ShapeDtypeStruct((bsz, s_len, d), F32),
        grid=(bsz, s_len // ts),
        in_specs=[pl.BlockSpec((1, ts, d), lambda b, i: (b, i, 0)),
                  pl.BlockSpec((1, C_HALO, d), lambda b, i: (b, jnp.maximum(i * hb - 1, 0), 0)),
                  pl.BlockSpec((C_KERNEL, d), lambda b, i: (0, 0)),
                  pl.BlockSpec((1, d), lambda b, i: (0, 0)),
                  pl.BlockSpec((1, d), lambda b, i: (0, 0)),
                  pl.BlockSpec((1, d), lambda b, i: (0, 0)),
                  pl.BlockSpec((d, d), lambda b, i: (0, 0)),
                  pl.BlockSpec((1, d), lambda b, i: (0, 0)),
                  pl.BlockSpec((1, ts, d), lambda b, i: (b, i, 0))],
        out_specs=pl.BlockSpec((1, ts, d), lambda b, i: (b, i, 0)),
        scratch_shapes=[pltpu.VMEM((ts + C_HALO, d), F32), pltpu.VMEM((ts, d), F32)],
        compiler_params=_cp("parallel", "parallel"),
        name="c_conv",
    )(y1, y1, w_dw, vec(b_dw), vec(ln_g), vec(ln_b), w2, vec(b2), resid)


def _mixer_c(h, g, w_pw1, b_pw1, w_dw, b_dw, ln_g, ln_b, w_pw2, b_pw2, bsz, s_len):
    n, d = h.shape
    y1 = _norm_matmul(h, g, w_pw1.astype(BF16), b_pw1, mode="glu", out_dtype=F32, tn=512, name="c_pw1")
    out = _c_conv(y1.reshape(bsz, s_len, d), w_dw, b_dw, ln_g, ln_b, w_pw2.astype(BF16), b_pw2,
                  h.reshape(bsz, s_len, d))
    return out.reshape(n, d)


def _dattn_kernel(q_ref, kc_ref, kp_ref, vc_ref, vp_ref, bias_ref, o_ref, lse_ref):
    nb = pl.program_id(2)
    first = jnp.where(nb == 0, NEG, 0.0)
    e = D_HEAD_DIM
    col = lax.broadcasted_iota(I32, (D_BLOCK, 2 * D_BLOCK), 1)
    first_mask = jnp.where(col < D_BLOCK, first, 0.0)
    low = lax.broadcasted_iota(I32, (D_BLOCK, LANES), 1) < e
    for hp in range(D_HEADS // 2):
        ps = slice(hp * LANES, (hp + 1) * LANES)
        q2 = q_ref[:, ps]
        kk = jnp.concatenate([kp_ref[:, ps], kc_ref[:, ps]], axis=0)
        vv = jnp.concatenate([vp_ref[:, ps], vc_ref[:, ps]], axis=0)
        outs, lses = [], []
        for hh in range(2):
            qh = jnp.where(low if hh == 0 else ~low, q2, jnp.zeros_like(q2))
            s = lax.dot_general(qh, kk, _NT, preferred_element_type=F32) * (e ** -0.5) + bias_ref[2 * hp + hh]
            s = s + first_mask
            m = jnp.max(s, axis=-1, keepdims=True)
            p = jnp.exp(s - m)
            l = jnp.sum(p, axis=-1, keepdims=True)
            outs.append(jnp.dot(p.astype(BF16), vv, preferred_element_type=F32) / l)
            lses.append(jnp.broadcast_to(m + jnp.log(l), (D_BLOCK, LANES)))
        o_ref[0, :, ps] = jnp.where(low, outs[0], outs[1])
        lse_ref[0, :, ps] = jnp.where(low, lses[0], lses[1])


def _dproj_kernel(x_ref, g_ref, w_ref, o_ref, xs_ref, *, dil):
    span, d = x_ref.shape
    nslab = d // LANES
    for j in range(nslab):
        xs_ref[j] = x_ref[:, j * LANES:(j + 1) * LANES]
    rows = 4 * D_BLOCK
    for c in range(span // rows):
        xr = jnp.concatenate(
            [jnp.concatenate([xs_ref[j, pl.ds(r, D_BLOCK, stride=dil), :] for j in range(nslab)], axis=1)
             for r in range(4 * c, 4 * c + 4)], axis=0)
        y = jnp.dot(_rms_bf16(xr, g_ref[...]), w_ref[...], preferred_element_type=F32)
        o_ref[c * rows:(c + 1) * rows, :] = y.astype(o_ref.dtype)


def _d_proj(h, g, w, dil):
    n, d = h.shape
    m = w.shape[1]
    span = D_BLOCK * dil
    return pl.pallas_call(
        functools.partial(_dproj_kernel, dil=dil),
        out_shape=jax.ShapeDtypeStruct((n, m), BF16),
        grid=(n // span,),
        in_specs=[pl.BlockSpec((span, d), lambda i: (i, 0)),
                  pl.BlockSpec((1, d), lambda i: (0, 0)),
                  pl.BlockSpec((d, m), lambda i: (0, 0))],
        out_specs=pl.BlockSpec((span, m), lambda i: (i, 0)),
        scratch_shapes=[pltpu.VMEM((d // LANES, span, LANES), F32)],
        compiler_params=_cp("parallel"),
        name=f"d_in_dil{dil}",
    )(h, g.reshape(1, d), w)


def _d_group(proj, bias_g, gidx, dil, bsz, s_len):
    n_sub = s_len // dil
    nb = n_sub // D_BLOCK
    w = D_HEADS * D_HEAD_DIM
    blk = (D_BLOCK, w)
    rb = lambda b, r, n: b * (s_len // D_BLOCK) + n * dil + r
    cur = lambda col: (lambda b, r, n: (rb(b, r, n), col))
    prev = lambda col: (lambda b, r, n: (rb(b, r, jnp.maximum(n - 1, 0)), col))
    out_blk = (1, D_BLOCK, w)
    out_sds = jax.ShapeDtypeStruct((bsz, n_sub, dil * w), F32)
    o, lse = pl.pallas_call(
        _dattn_kernel,
        out_shape=(out_sds, out_sds),
        grid=(bsz, dil, nb),
        in_specs=[pl.BlockSpec(blk, cur(0)),
                  pl.BlockSpec(blk, cur(1)),
                  pl.BlockSpec(blk, prev(1)),
                  pl.BlockSpec(blk, cur(2)),
                  pl.BlockSpec(blk, prev(2)),
                  pl.BlockSpec((D_HEADS, D_BLOCK, 2 * D_BLOCK), lambda b, r, n: (0, 0, 0))],
        out_specs=(pl.BlockSpec(out_blk, lambda b, r, n: (b, n, r)),
                   pl.BlockSpec(out_blk, lambda b, r, n: (b, n, r))),
        compiler_params=_cp("parallel", "parallel", "arbitrary"),
        name=f"d_attn_g{gidx}",
    )(proj, proj, proj, proj, proj, bias_g)
    return o.reshape(bsz * s_len, w), lse.reshape(bsz * s_len, w)


def _dmerge_kernel(o0, o1, o2, l0, l1, l2, w_ref, r_ref, out_ref):
    a0, a1, a2 = l0[...], l1[...], l2[...]
    mx = jnp.maximum(jnp.maximum(a0, a1), a2)
    e0, e1, e2 = jnp.exp(a0 - mx), jnp.exp(a1 - mx), jnp.exp(a2 - mx)
    y = (e0 * o0[...] + e1 * o1[...] + e2 * o2[...]) / (e0 + e1 + e2)
    out_ref[...] = r_ref[...] + jnp.dot(y.astype(BF16), w_ref[...], preferred_element_type=F32)


def _d_merge(outs, lses, w_out, resid, *, tm=512):
    n, w = outs[0].shape
    d = resid.shape[1]
    row = pl.BlockSpec((tm, w), lambda i: (i, 0))
    return pl.pallas_call(
        _dmerge_kernel,
        out_shape=jax.ShapeDtypeStruct((n, d), F32),
        grid=(n // tm,),
        in_specs=[row] * 6 + [pl.BlockSpec((w, d), lambda i: (0, 0)),
                              pl.BlockSpec((tm, d), lambda i: (i, 0))],
        out_specs=pl.BlockSpec((tm, d), lambda i: (i, 0)),
        compiler_params=_cp("parallel"),
        name="d_merge",
    )(*outs, *lses, w_out, resid)


def _mixer_d(h, g, w_in, w_out, rel_table, bsz, s_len):
    w = D_HEADS * D_HEAD_DIM
    ng = len(D_PAIRS)
    p_loc = np.arange(D_BLOCK)[:, None]
    j_loc = np.arange(2 * D_BLOCK)[None, :]
    m = p_loc + D_BLOCK - j_loc
    idx = []
    for window, dil in D_PAIRS:
        steps = window // dil
        idx.append(np.where((m >= 0) & (m <= steps), _rel_bucket_np(m * dil), -1))
    bias = _bias_tiles(rel_table, np.concatenate(idx, 0).astype(np.int32), F32, tile=D_BLOCK, tr=D_BLOCK)
    outs, lses = [], []
    for gi, (_, dil) in enumerate(D_PAIRS):
        cols = [w_in[:, (part * ng + gi) * w:(part * ng + gi + 1) * w] for part in range(3)]
        w_g = jnp.concatenate(cols, axis=1).astype(BF16)
        if dil == 1:
            proj = _norm_matmul(h, g, w_g, out_dtype=BF16, tn=3 * w, name="d_in_dil1")
        else:
            proj = _d_proj(h, g, w_g, dil)
        o, lse = _d_group(proj, bias[gi], gi, dil, bsz, s_len)
        outs.append(o)
        lses.append(lse)
    return _d_merge(outs, lses, w_out.astype(BF16), h)


def kernel(x, norm_g, final_g, ffn_w_in, ffn_w_out, rel_table, a_w_in, a_w_out, b_w_in, b_b_in, b_ln_g, b_ln_b, b_w_sp, b_b_sp, b_w_out, c_w_pw1, c_b_pw1, c_w_dw, c_b_dw, c_ln_g, c_ln_b, c_w_pw2, c_b_pw2, d_w_in, d_w_out):
    bsz, s_len, d = x.shape
    depth = norm_g.shape[0]
    h = x.reshape(bsz * s_len, d)
    for i in range(depth):
        kind, j = i % 4, i // 4
        h = _ffn(h, norm_g[i, 0], ffn_w_in[i, 0].astype(BF16), ffn_w_out[i, 0].astype(BF16))
        g = norm_g[i, 1]
        if kind == 0:
            h = _mixer_a(h, g, a_w_in[j], a_w_out[j], rel_table, bsz, s_len)
        elif kind == 1:
            h = _mixer_b(h, g, b_w_in[j], b_b_in[j], b_ln_g[j], b_ln_b[j], b_w_sp[j], b_b_sp[j], b_w_out[j])
        elif kind == 2:
            h = _mixer_c(h, g, c_w_pw1[j], c_b_pw1[j], c_w_dw[j], c_b_dw[j], c_ln_g[j], c_ln_b[j],
                         c_w_pw2[j], c_b_pw2[j], bsz, s_len)
        else:
            h = _mixer_d(h, g, d_w_in[j], d_w_out[j], rel_table, bsz, s_len)
        h = _ffn(h, norm_g[i, 2], ffn_w_in[i, 1].astype(BF16), ffn_w_out[i, 1].astype(BF16),
                 final_g if i == depth - 1 else None)
    return h.reshape(bsz, s_len, d)
```
